```python
import math
import jax
import jax.numpy as jnp
from jax import lax
import numpy as np

D_MODEL = 1024
BATCH = 8
SEQ = 4096
DEPTH = 4

N_A_LAYERS = DEPTH // 2
N_B_LAYERS = DEPTH - N_A_LAYERS
HEAD_DIM = 64
ROPE_DIMS = HEAD_DIM // 4
ROPE_THETA = 500000.0
NORM_EPS = 1e-5
Q_BLOCK = 128

DIFF_HEADS = 6
DIFF_V_DIM = 2 * HEAD_DIM
DIFF_QK_WIDTH = DIFF_HEADS * 2 * HEAD_DIM
DIFF_V_WIDTH = DIFF_HEADS * DIFF_V_DIM

MEM_TOKENS = 256
MEM_HEADS = 4
MEM_Q_WIDTH = MEM_HEADS * HEAD_DIM

DIL_GROUPS = ((128, 1), (512, 4), (2048, 16))
DIL_HEADS = 12
DIL_BLOCK = 128
DIL_QK_WIDTH = len(DIL_GROUPS) * DIL_HEADS * HEAD_DIM

A_IN_WIDTH = 2 * DIFF_QK_WIDTH + DIFF_V_WIDTH + MEM_Q_WIDTH
B_IN_WIDTH = DIL_QK_WIDTH + MEM_Q_WIDTH
MIX_OUT_WIDTH = DIFF_V_WIDTH + MEM_Q_WIDTH

PEER_HEADS = 8
PEER_N_KEYS = 128
PEER_N_EXPERTS = PEER_N_KEYS ** 2
PEER_TOPK = 16
PEER_QUERY_DIM = 256
PEER_HALF = PEER_QUERY_DIM // 2
PEER_CHUNK = 128

kernel_name = "hybrid_diffattn_dilated_peer_yoco"


def rmsnorm(x, g):
    xf = x.astype(jnp.float32)
    y = xf * lax.rsqrt(jnp.mean(xf * xf, axis=-1, keepdims=True) + NORM_EPS)
    return (y * g.astype(jnp.float32)).astype(x.dtype)


def rope_tables(seq):
    inv = ROPE_THETA ** (-jnp.arange(0, ROPE_DIMS, 2, dtype=jnp.float32) / ROPE_DIMS)
    ang = jnp.arange(seq, dtype=jnp.float32)[:, None] * inv[None, :]
    return jnp.cos(ang), jnp.sin(ang)


def partial_rope(t, cos, sin):
    half = ROPE_DIMS // 2
    rot = t[..., :ROPE_DIMS].astype(jnp.float32)
    x1, x2 = rot[..., :half], rot[..., half:]
    c = cos[None, :, None, :]
    s = sin[None, :, None, :]
    rotated = jnp.concatenate([x1 * c - x2 * s, x2 * c + x1 * s], axis=-1).astype(t.dtype)
    return jnp.concatenate([rotated, t[..., ROPE_DIMS:]], axis=-1)


def diff_attention(q, k, v, lam, subln_g, lambda_init):
    B, S = q.shape[0], q.shape[1]
    qh = q.reshape(B, S, DIFF_HEADS, 2, HEAD_DIM).transpose(0, 2, 3, 1, 4)
    kh = k.reshape(B, S, DIFF_HEADS, 2, HEAD_DIM).transpose(0, 2, 3, 1, 4)
    vh = v.transpose(0, 2, 1, 3)
    scale = HEAD_DIM ** -0.5
    kpos = jnp.arange(S)

    def block(i):
        qb = lax.dynamic_slice_in_dim(qh, i * Q_BLOCK, Q_BLOCK, axis=3)
        s = jnp.einsum('bhpqd,bhpkd->bhpqk', qb, kh).astype(jnp.float32) * scale
        qpos = i * Q_BLOCK + jnp.arange(Q_BLOCK)
        causal = kpos[None, :] <= qpos[:, None]
        p = jax.nn.softmax(jnp.where(causal, s, -jnp.inf), axis=-1)
        a = p[:, :, 0] - lam * p[:, :, 1]
        return jnp.einsum('bhqk,bhkd->bhqd', a.astype(v.dtype), vh)

    o = lax.map(block, jnp.arange(S // Q_BLOCK))
    o = o.transpose(1, 0, 3, 2, 4).reshape(B, S, DIFF_HEADS, DIFF_V_DIM)
    o = rmsnorm(o, subln_g) * (1.0 - lambda_init)
    return o.reshape(B, S, DIFF_V_WIDTH)


def memory_attention(q, mem_k, mem_v):
    B, S = q.shape[0], q.shape[1]
    s = jnp.einsum('bshd,bmhd->bhsm', q, mem_k).astype(jnp.float32) * HEAD_DIM ** -0.5
    p = jax.nn.softmax(s, axis=-1)
    o = jnp.einsum('bhsm,bmhd->bshd', p.astype(q.dtype), mem_v)
    return o.reshape(B, S, MEM_Q_WIDTH)


def dilated_branch(q, k, v, window, dilation):
    B, S, H, Dh = q.shape
    L = S // dilation
    n_steps = window // dilation
    nb = -(-L // DIL_BLOCK)
    Lp = nb * DIL_BLOCK

    def split(t):
        t = t.reshape(B, L, dilation, H, Dh).transpose(0, 2, 3, 1, 4)
        t = jnp.pad(t, ((0, 0), (0, 0), (0, 0), (0, Lp - L), (0, 0)))
        return t.reshape(B, dilation, H, nb, DIL_BLOCK, Dh)

    def with_prev(t):
        prev = jnp.pad(t[:, :, :, :-1], ((0, 0), (0, 0), (0, 0), (1, 0), (0, 0), (0, 0)))
        return jnp.concatenate([prev, t], axis=4)

    qb = split(q)
    kc = with_prev(split(k))
    vc = with_prev(split(v))
    s = jnp.einsum('bdhnqc,bdhnkc->bdhnqk', qb, kc).astype(jnp.float32) * HEAD_DIM ** -0.5
    qi = jnp.arange(DIL_BLOCK)[:, None]
    kj = jnp.arange(2 * DIL_BLOCK)[None, :]
    dist = qi + DIL_BLOCK - kj
    blk = jnp.arange(nb)[:, None, None]
    valid = (dist >= 0) & (dist <= n_steps) & ((blk > 0) | (kj >= DIL_BLOCK))
    s = jnp.where(valid, s, -jnp.inf)
    m = jnp.max(s, axis=-1, keepdims=True)
    e = jnp.exp(s - m)
    den = jnp.sum(e, axis=-1, keepdims=True)
    o = jnp.einsum('bdhnqk,bdhnkc->bdhnqc', (e / den).astype(v.dtype), vc)
    lse = (m + jnp.log(den))[..., 0]

    def merge(t):
        t = t.reshape((B, dilation, H, Lp) + t.shape[5:])[:, :, :, :L]
        t = jnp.moveaxis(t, 3, 1)
        return t.reshape((B, S, H) + t.shape[4:])

    return merge(o), merge(lse)


def dilated_mixture(q_all, k_all, v_all):
    B, S = q_all.shape[0], q_all.shape[1]
    outs, lses = [], []
    for g, (window, dilation) in enumerate(DIL_GROUPS):
        sl = slice(g * DIL_HEADS, (g + 1) * DIL_HEADS)
        o, lse = dilated_branch(q_all[:, :, sl], k_all[:, :, sl], v_all[:, :, sl], window, dilation)
        outs.append(o)
        lses.append(lse)
    alpha = jax.nn.softmax(jnp.stack(lses, axis=0), axis=0)
    o = jnp.sum(alpha[..., None] * jnp.stack(outs, axis=0).astype(jnp.float32), axis=0)
    return o.astype(q_all.dtype).reshape(B, S, DIL_HEADS * HEAD_DIM)


def peer(xn, w_q, sub_keys, u, v):
    B, S, D = xn.shape
    T = B * S
    xt = xn.reshape(T, D)
    q = (xt @ w_q).reshape(T, PEER_HEADS, 2, PEER_HALF)
    s = jnp.einsum('thpc,hpnc->thpn', q, sub_keys).astype(jnp.float32)
    s1, i1 = lax.top_k(s[:, :, 0], PEER_TOPK)
    s2, i2 = lax.top_k(s[:, :, 1], PEER_TOPK)
    cand = (s1[..., :, None] + s2[..., None, :]).reshape(T, PEER_HEADS, PEER_TOPK * PEER_TOPK)
    cidx = (i1[..., :, None] * PEER_N_KEYS + i2[..., None, :]).reshape(T, PEER_HEADS, PEER_TOPK * PEER_TOPK)
    top_s, top_j = lax.top_k(cand, PEER_TOPK)
    expert = jnp.take_along_axis(cidx, top_j, axis=-1)
    gate = jax.nn.softmax(top_s, axis=-1).astype(xn.dtype)
    n_chunks = T // PEER_CHUNK

    def chunk(args):
        xc, ec, gc = args
        h = jax.nn.gelu(jnp.einsum('cd,chkd->chk', xc, u[ec]), approximate=False)
        return jnp.einsum('chk,chkd->cd', gc * h, v[ec])

    out = lax.map(chunk, (xt.reshape(n_chunks, PEER_CHUNK, D),
                          expert.reshape(n_chunks, PEER_CHUNK, PEER_HEADS, PEER_TOPK),
                          gate.reshape(n_chunks, PEER_CHUNK, PEER_HEADS, PEER_TOPK)))
    return out.reshape(B, S, D)


def setup_inputs(seed: int = 0) -> dict:
    key = jax.random.key(seed)
    ks = jax.random.split(key, 18)
    f32 = jnp.float32

    def nrm(k, shape, scale):
        return jax.random.normal(k, shape, f32) * scale

    def gain(k, shape):
        return 1.0 + 0.02 * jax.random.normal(k, shape, f32)

    D = D_MODEL
    return {
        "x": nrm(ks[0], (BATCH, SEQ, D), 1.0),
        "mem": nrm(ks[1], (BATCH, MEM_TOKENS, D), 1.0),
        "norm_mix": gain(ks[2], (DEPTH, D)),
        "a_w_in": nrm(ks[3], (N_A_LAYERS, D, A_IN_WIDTH), D ** -0.5),
        "a_lambda": nrm(ks[4], (N_A_LAYERS, 4, HEAD_DIM), 0.1),
        "a_subln": gain(ks[5], (N_A_LAYERS, DIFF_V_DIM)),
        "b_w_in": nrm(ks[6], (N_B_LAYERS, D, B_IN_WIDTH), D ** -0.5),
        "norm_mem": gain(ks[7], (DEPTH, D)),
        "w_mem_kv": nrm(ks[8], (DEPTH, D, 2 * MEM_Q_WIDTH), D ** -0.5),
        "w_out": nrm(ks[9], (DEPTH, MIX_OUT_WIDTH, D), MIX_OUT_WIDTH ** -0.5),
        "norm_ffn": gain(ks[10], (DEPTH, D)),
        "peer_wq": nrm(ks[11], (DEPTH, D, PEER_HEADS * PEER_QUERY_DIM), D ** -0.5),
        "peer_keys": nrm(ks[12], (DEPTH, PEER_HEADS, 2, PEER_N_KEYS, PEER_HALF), PEER_HALF ** -0.5),
        "peer_u": nrm(ks[13], (DEPTH, PEER_N_EXPERTS, D), D ** -0.5),
        "peer_v": nrm(ks[14], (DEPTH, PEER_N_EXPERTS, D), PEER_TOPK ** -0.5),
        "shared_norm": gain(ks[15], (D,)),
        "shared_w_kv": nrm(ks[16], (D, 2 * DIL_QK_WIDTH), D ** -0.5),
        "final_norm": gain(ks[17], (D,)),
    }


def reference(x, mem, norm_mix, a_w_in, a_lambda, a_subln, b_w_in, norm_mem, w_mem_kv, w_out,
              norm_ffn, peer_wq, peer_keys, peer_u, peer_v, shared_norm, shared_w_kv, final_norm):
    B, S, D = x.shape
    M = mem.shape[1]
    n_dil = len(DIL_GROUPS) * DIL_HEADS
    cos, sin = rope_tables(S)
    shared_k = None
    shared_v = None
    for l in range(DEPTH):
        h = rmsnorm(x, norm_mix[l])
        mkv = (rmsnorm(mem, norm_mem[l]) @ w_mem_kv[l]).reshape(B, M, 2, MEM_HEADS, HEAD_DIM)
        if l < N_A_LAYERS:
            proj = h @ a_w_in[l]
            q, k, v, q_mem = jnp.split(
                proj, [DIFF_QK_WIDTH, 2 * DIFF_QK_WIDTH, 2 * DIFF_QK_WIDTH + DIFF_V_WIDTH], axis=-1)
            q = partial_rope(q.reshape(B, S, 2 * DIFF_HEADS, HEAD_DIM), cos, sin)
            k = partial_rope(k.reshape(B, S, 2 * DIFF_HEADS, HEAD_DIM), cos, sin)
            v = v.reshape(B, S, DIFF_HEADS, DIFF_V_DIM)
            lambda_init = 0.8 - 0.6 * math.exp(-0.3 * l)
            lp = a_lambda[l].astype(jnp.float32)
            lam = jnp.exp(jnp.sum(lp[0] * lp[1])) - jnp.exp(jnp.sum(lp[2] * lp[3])) + lambda_init
            mix = diff_attention(q, k, v, lam, a_subln[l], lambda_init)
        else:
            proj = h @ b_w_in[l - N_A_LAYERS]
            q_dil, q_mem = jnp.split(proj, [DIL_QK_WIDTH], axis=-1)
            q_dil = partial_rope(q_dil.reshape(B, S, n_dil, HEAD_DIM), cos, sin)
            mix = dilated_mixture(q_dil, shared_k, shared_v)
        mo = memory_attention(q_mem.reshape(B, S, MEM_HEADS, HEAD_DIM), mkv[:, :, 0], mkv[:, :, 1])
        x = x + jnp.concatenate([mix, mo], axis=-1) @ w_out[l]
        x = x + peer(rmsnorm(x, norm_ffn[l]), peer_wq[l], peer_keys[l], peer_u[l], peer_v[l])
        if l == N_A_LAYERS - 1:
            kv = (rmsnorm(x, shared_norm) @ shared_w_kv).reshape(B, S, 2, n_dil, HEAD_DIM)
            shared_k = partial_rope(kv[:, :, 0], cos, sin)
            shared_v = kv[:, :, 1]
    return rmsnorm(x, final_norm)
```

```python
import functools
import math

import jax
import jax.numpy as jnp
from jax import lax
from jax.experimental import pallas as pl
from jax.experimental.pallas import tpu as pltpu

F32 = jnp.float32
BF16 = jnp.bfloat16

HEAD_DIM = 64
ROPE_DIMS = HEAD_DIM // 4
ROPE_HALF = ROPE_DIMS // 2
ROPE_THETA = 500000.0
NORM_EPS = 1e-5
DIFF_HEADS = 6
MEM_HEADS = 4
MEM_Q_WIDTH = MEM_HEADS * HEAD_DIM
DIL_GROUPS = ((128, 1), (512, 4), (2048, 16))
DIL_HEADS = 12
DIL_BLOCK = 128
DIL_GROUP_WIDTH = DIL_HEADS * HEAD_DIM
PEER_HEADS = 8
PEER_N_KEYS = 128
PEER_TOPK = 16
PEER_HALF = 128
PEER_SLOTS = PEER_HEADS * PEER_TOPK

LANES = 128
SUBLANES = 8
VMEM_LIMIT_BYTES = 56 * 1024 * 1024

_NT = (((1,), (1,)), ((), ()))


def _params(n_axes):
    return pltpu.CompilerParams(dimension_semantics=("arbitrary",) * n_axes,
                                vmem_limit_bytes=VMEM_LIMIT_BYTES)


def _resident(shape):
    zeros = (0,) * len(shape)
    return pl.BlockSpec(shape, lambda *_: zeros, pipeline_mode=pl.Buffered(1))


def _rms_scale(x, g):
    ms = jnp.mean(x * x, axis=-1, keepdims=True)
    return x * lax.rsqrt(ms + NORM_EPS) * g


def _rope_tables(seq):
    inv = ROPE_THETA ** (-jnp.arange(0, ROPE_DIMS, 2, dtype=F32) / ROPE_DIMS)
    ang = jnp.arange(seq, dtype=F32)[:, None] * inv[None, :]
    cos, sin = jnp.cos(ang), jnp.sin(ang)
    ones = jnp.ones((seq, HEAD_DIM - ROPE_DIMS), F32)
    zeros = jnp.zeros((seq, HEAD_DIM - ROPE_DIMS), F32)
    zh = jnp.zeros((seq, ROPE_HALF), F32)
    c = jnp.concatenate([cos, cos, ones], axis=1)
    sa = jnp.concatenate([-sin, zh, zeros], axis=1)
    sb = jnp.concatenate([zh, sin, zeros], axis=1)
    rep = LANES // HEAD_DIM
    return tuple(jnp.tile(t, (1, rep)) for t in (c, sa, sb))


def _proj_kernel(*refs, n_rope, chunk):
    if n_rope:
        x_ref, g_ref, w_ref, c_ref, sa_ref, sb_ref, o_ref = refs
    else:
        x_ref, g_ref, w_ref, o_ref = refs
    y = _rms_scale(x_ref[...], g_ref[...]).astype(BF16)
    n = o_ref.shape[1]
    for c0 in range(0, n, chunk):
        acc = jnp.dot(y, w_ref[:, c0:c0 + chunk], preferred_element_type=F32)
        if c0 < n_rope:
            for k0 in range(0, chunk, LANES):
                a = acc[:, k0:k0 + LANES]
                a = (a * c_ref[...]
                     + pltpu.roll(a, LANES - ROPE_HALF, 1) * sa_ref[...]
                     + pltpu.roll(a, ROPE_HALF, 1) * sb_ref[...])
                o_ref[:, c0 + k0:c0 + k0 + LANES] = a.astype(o_ref.dtype)
        else:
            o_ref[:, c0:c0 + chunk] = acc.astype(o_ref.dtype)


def _proj(x2d, g, w_bf16, *, tm, n_rope=0, rope=None, seq=None):
    t, d = x2d.shape
    n = w_bf16.shape[1]
    chunk = 256
    assert t % tm == 0 and n % chunk == 0 and n_rope % chunk == 0
    in_specs = [pl.BlockSpec((tm, d), lambda i: (i, 0)),
                _resident((1, d)),
                _resident((d, n))]
    args = [x2d, g.reshape(1, d), w_bf16]
    if n_rope:
        nblk = seq // tm
        spec = pl.BlockSpec((tm, LANES), lambda i: (i % nblk, 0))
        in_specs += [spec, spec, spec]
        args += list(rope)
    return pl.pallas_call(
        functools.partial(_proj_kernel, n_rope=n_rope, chunk=chunk),
        grid=(t // tm,),
        in_specs=in_specs,
        out_specs=pl.BlockSpec((tm, n), lambda i: (i, 0)),
        out_shape=jax.ShapeDtypeStruct((t, n), BF16),
        compiler_params=_params(1),
        name="norm_proj",
    )(*args)


def _online_softmax_step(q, k, v, carry, mask):
    m, l, acc = carry
    s = lax.dot_general(q, k, _NT, preferred_element_type=F32)
    if mask is not None:
        s = jnp.where(mask, s, -jnp.inf)
    m_new = jnp.maximum(m, jnp.max(s, axis=-1, keepdims=True))
    alpha = jnp.exp(m - m_new)
    p = jnp.exp(s - m_new)
    l = alpha * l + jnp.sum(p, axis=-1, keepdims=True)
    acc = alpha * acc + jnp.dot(p.astype(BF16), v, preferred_element_type=F32)
    return m_new, l, acc


def _diff_attn_kernel(q_ref, k_ref, v_ref, lp_ref, g_ref, o_ref, *, tq, lambda_init):
    qi = pl.program_id(2)
    q = q_ref[...]
    lane = lax.broadcasted_iota(jnp.int32, q.shape, 1)
    zero = jnp.zeros_like(q)
    qa = jnp.where(lane < HEAD_DIM, q, zero)
    qb = jnp.where(lane >= HEAD_DIM, q, zero)

    def init():
        return (jnp.full((tq, 1), -jnp.inf, F32), jnp.zeros((tq, 1), F32),
                jnp.zeros((tq, 2 * HEAD_DIM), F32))

    def body(j, carry):
        c1, c2 = carry
        k = k_ref[pl.ds(pl.multiple_of(j * tq, tq), tq), :]
        v = v_ref[pl.ds(pl.multiple_of(j * tq, tq), tq), :]
        return (_online_softmax_step(qa, k, v, c1, None),
                _online_softmax_step(qb, k, v, c2, None))

    c1, c2 = lax.fori_loop(0, qi, body, (init(), init()))
    row = lax.broadcasted_iota(jnp.int32, (tq, tq), 0)
    col = lax.broadcasted_iota(jnp.int32, (tq, tq), 1)
    causal = col <= row
    start = pl.multiple_of(qi * tq, tq)
    k = k_ref[pl.ds(start, tq), :]
    v = v_ref[pl.ds(start, tq), :]
    m1, l1, a1 = _online_softmax_step(qa, k, v, c1, causal)
    m2, l2, a2 = _online_softmax_step(qb, k, v, c2, causal)

    lp = lp_ref[...]
    lam = (jnp.exp(jnp.sum(lp[0:1] * lp[1:2], axis=-1, keepdims=True))
           - jnp.exp(jnp.sum(lp[2:3] * lp[3:4], axis=-1, keepdims=True)) + lambda_init)
    o = a1 / l1 - lam * (a2 / l2)
    o = _rms_scale(o, g_ref[...]) * (1.0 - lambda_init)
    o_ref[...] = o.astype(o_ref.dtype)


def _diff_attention(proj, lp, subln, *, batch, seq, lambda_init, tq):
    nq = seq // tq
    width = 2 * HEAD_DIM
    return pl.pallas_call(
        functools.partial(_diff_attn_kernel, tq=tq, lambda_init=lambda_init),
        grid=(batch, DIFF_HEADS, nq),
        in_specs=[
            pl.BlockSpec((tq, width), lambda b, h, i: (b * nq + i, h)),
            pl.BlockSpec((seq, width), lambda b, h, i: (b, DIFF_HEADS + h)),
            pl.BlockSpec((seq, width), lambda b, h, i: (b, 2 * DIFF_HEADS + h)),
            _resident((4, HEAD_DIM)),
            _resident((1, width)),
        ],
        out_specs=pl.BlockSpec((tq, width), lambda b, h, i: (b * nq + i, h)),
        out_shape=jax.ShapeDtypeStruct((batch * seq, DIFF_HEADS * width), BF16),
        compiler_params=_params(3),
        name="diff_attention",
    )(proj, proj, proj, lp, subln.reshape(1, width))


def _dilated_kernel(q_ref, kc_ref, kp_ref, vc_ref, vp_ref, o_ref, st_ref, *, tq, n_steps):
    n = pl.program_id(2)
    hp = pl.program_id(3)
    blk = DIL_BLOCK

    @pl.when(hp == 0)
    def _():
        st_ref[...] = jnp.zeros_like(st_ref)

    qrow = lax.broadcasted_iota(jnp.int32, (blk, 2 * blk), 0)
    kcol = lax.broadcasted_iota(jnp.int32, (blk, 2 * blk), 1)
    dist = qrow + blk - kcol
    band = (dist >= 0) & (dist <= n_steps)
    lane_q = lax.broadcasted_iota(jnp.int32, (blk, LANES), 1)
    lane_s = lax.broadcasted_iota(jnp.int32, (blk, LANES), 1)

    for i in range(tq // blk):
        rows = slice(i * blk, (i + 1) * blk)
        q = q_ref[rows, :]
        if i == 0:
            k_prev, v_prev = kp_ref[...], vp_ref[...]
            valid = band & ((n > 0) | (kcol >= blk))
        else:
            prev = slice((i - 1) * blk, i * blk)
            k_prev, v_prev = kc_ref[prev, :], vc_ref[prev, :]
            valid = band
        keys = jnp.concatenate([k_prev, kc_ref[rows, :]], axis=0)
        vals = jnp.concatenate([v_prev, vc_ref[rows, :]], axis=0)
        out = jnp.zeros((blk, LANES), F32)
        stats = st_ref[rows, :]
        for e in range(LANES // HEAD_DIM):
            head_lanes = (lane_q // HEAD_DIM) == e
            qe = jnp.where(head_lanes, q, jnp.zeros_like(q))
            s = lax.dot_general(qe, keys, _NT, preferred_element_type=F32)
            s = jnp.where(valid, s, -jnp.inf)
            m = jnp.max(s, axis=-1, keepdims=True)
            p = jnp.exp(s - m)
            den = jnp.sum(p, axis=-1, keepdims=True)
            oe = jnp.dot((p / den).astype(BF16), vals, preferred_element_type=F32)
            out = jnp.where(head_lanes, oe, out)
            lse = m + jnp.log(den)
            stats = jnp.where(lane_s == hp * (LANES // HEAD_DIM) + e, lse, stats)
        o_ref[rows, :] = out.astype(o_ref.dtype)
        st_ref[rows, :] = stats


def _dilated_group(q4, k4, v4, *, q_off, k_off, v_off, n_steps):
    b, d, length, _ = q4.shape
    tq = min(512, length)
    assert length % tq == 0 and tq % DIL_BLOCK == 0
    sub = tq // DIL_BLOCK
    pairs = DIL_GROUP_WIDTH // LANES

    def cur(off):
        return pl.BlockSpec((None, None, tq, LANES), lambda bi, r, n, h: (bi, r, n, off + h))

    def prev(off):
        return pl.BlockSpec((None, None, DIL_BLOCK, LANES),
                            lambda bi, r, n, h: (bi, r, jnp.maximum(n * sub - 1, 0), off + h))

    return pl.pallas_call(
        functools.partial(_dilated_kernel, tq=tq, n_steps=n_steps),
        grid=(b, d, length // tq, pairs),
        in_specs=[cur(q_off), cur(k_off), prev(k_off), cur(v_off), prev(v_off)],
        out_specs=[pl.BlockSpec((None, None, tq, LANES), lambda bi, r, n, h: (bi, r, n, h)),
                   pl.BlockSpec((None, None, tq, LANES), lambda bi, r, n, h: (bi, r, n, 0))],
        out_shape=[jax.ShapeDtypeStruct((b, d, length, DIL_GROUP_WIDTH), BF16),
                   jax.ShapeDtypeStruct((b, d, length, LANES), F32)],
        compiler_params=_params(4),
        name="dilated_attention",
    )(q4, k4, k4, v4, v4)


def _memory_attention(q, k, v):
    lane = lax.broadcasted_iota(jnp.int32, q.shape, 1)
    out = jnp.zeros(q.shape, F32)
    for h in range(MEM_HEADS):
        head_lanes = (lane // HEAD_DIM) == h
        qh = jnp.where(head_lanes, q, jnp.zeros_like(q))
        s = lax.dot_general(qh, k, _NT, preferred_element_type=F32)
        m = jnp.max(s, axis=-1, keepdims=True)
        p = jnp.exp(s - m)
        den = jnp.sum(p, axis=-1, keepdims=True)
        oh = jnp.dot((p / den).astype(BF16), v, preferred_element_type=F32)
        out = jnp.where(head_lanes, oh, out)
    return out


def _split_bf16(x):
    hi = x.astype(BF16)
    lo = (x - hi.astype(F32)).astype(BF16)
    return hi, lo


def _outproj_kernel(*refs, n_groups):
    if n_groups:
        x_ref, qm_ref, mk_ref, mv_ref, w_ref = refs[:5]
        o_refs = refs[5:5 + n_groups]
        st_refs = refs[5 + n_groups:5 + 2 * n_groups]
        ex_ref, out_ref = refs[5 + 2 * n_groups:]
        lses = [r[...] for r in st_refs]
        top = functools.reduce(jnp.maximum, lses)
        es = [jnp.exp(l - top) for l in lses]
        den = functools.reduce(lambda a, b: a + b, es)
        mix = None
        for e, o_ref in zip(es, o_refs):
            hi, lo = _split_bf16(e / den)
            wide = (jnp.dot(hi, ex_ref[...], preferred_element_type=F32)
                    + jnp.dot(lo, ex_ref[...], preferred_element_type=F32))
            term = wide * o_ref[...].astype(F32)
            mix = term if mix is None else mix + term
        mix = mix.astype(BF16)
    else:
        x_ref, qm_ref, mk_ref, mv_ref, w_ref, mix_ref, out_ref = refs
        mix = mix_ref[...]
    mo = _memory_attention(qm_ref[...], mk_ref[...], mv_ref[...]).astype(BF16)
    k_mix = mix.shape[1]
    acc = jnp.dot(mix, w_ref[:k_mix, :], preferred_element_type=F32)
    acc += jnp.dot(mo, w_ref[k_mix:, :], preferred_element_type=F32)
    out_ref[...] = x_ref[...] + acc


def _outproj(x2d, proj, memkv, w_bf16, *, seq, mem_tokens, tm, mix=None, group_out=None,
             group_stats=None, expand=None):
    t, d = x2d.shape
    per_batch = seq // tm
    qm_block = (proj.shape[1] - MEM_Q_WIDTH) // MEM_Q_WIDTH
    row = lambda i: (i, 0)
    in_specs = [pl.BlockSpec((tm, d), row),
                pl.BlockSpec((tm, MEM_Q_WIDTH), lambda i: (i, qm_block)),
                pl.BlockSpec((mem_tokens, MEM_Q_WIDTH), lambda i: (i // per_batch, 0)),
                pl.BlockSpec((mem_tokens, MEM_Q_WIDTH), lambda i: (i // per_batch, 1)),
                _resident(w_bf16.shape)]
    args = [x2d, proj, memkv, memkv, w_bf16]
    if mix is not None:
        n_groups = 0
        in_specs.append(pl.BlockSpec((tm, mix.shape[1]), row))
        args.append(mix)
    else:
        n_groups = len(group_out)
        in_specs += [pl.BlockSpec((tm, DIL_GROUP_WIDTH), row)] * n_groups
        in_specs += [pl.BlockSpec((tm, LANES), row)] * n_groups
        in_specs.append(_resident(expand.shape))
        args += list(group_out) + list(group_stats) + [expand]
    return pl.pallas_call(
        functools.partial(_outproj_kernel, n_groups=n_groups),
        grid=(t // tm,),
        in_specs=in_specs,
        out_specs=pl.BlockSpec((tm, d), row),
        out_shape=jax.ShapeDtypeStruct((t, d), F32),
        compiler_params=_params(1),
        name="mix_outproj",
    )(*args)


def _topk_rows(s, k):
    n = s.shape[0]
    row = lax.broadcasted_iota(jnp.int32, s.shape, 0)
    vals, idxs = [], []
    for _ in range(k):
        m = jnp.max(s, axis=0, keepdims=True)
        i = jnp.min(jnp.where(s == m, row, n), axis=0, keepdims=True)
        vals.append(m)
        idxs.append(i)
        s = jnp.where(row == i, -jnp.inf, s)
    return jnp.concatenate(vals, axis=0), jnp.concatenate(idxs, axis=0)


def _select_rows(table, idx):
    row = lax.broadcasted_iota(jnp.int32, table.shape, 0)
    return jnp.sum(jnp.where(row == idx, table, jnp.zeros_like(table)), axis=0, keepdims=True)


def _peer_route_kernel(x_ref, g_ref, wq_ref, keys_ref, xn_ref, pair_ref, shift_ref, gate_ref):
    xn = _rms_scale(x_ref[...], g_ref[...])
    xn_ref[...] = xn
    q = jnp.dot(xn.astype(BF16), wq_ref[...], preferred_element_type=F32).astype(BF16)
    kk = PEER_TOPK
    for h in range(PEER_HEADS):
        tops = []
        for half in range(2):
            c0 = (2 * h + half) * PEER_HALF
            s = lax.dot_general(keys_ref[2 * h + half], q[:, c0:c0 + PEER_HALF], _NT,
                                preferred_element_type=F32)
            tops.append(_topk_rows(s, kk))
        (s1, i1), (s2, i2) = tops
        cand = jnp.concatenate([s1[a:a + 1] + s2 for a in range(kk)], axis=0)
        top_s, top_j = _topk_rows(cand, kk)
        ids = []
        for r in range(kk):
            j = top_j[r:r + 1]
            ids.append(_select_rows(i1, j // kk) * PEER_N_KEYS + _select_rows(i2, j % kk))
        expert = jnp.concatenate(ids, axis=0)
        e = jnp.exp(top_s - top_s[0:1])
        gate = e / jnp.sum(e, axis=0, keepdims=True)
        rows = slice(h * kk, (h + 1) * kk)
        pair_ref[rows, :] = expert // 2
        shift_ref[rows, :] = (expert % 2) * 16
        gate_ref[rows, :] = gate


def _peer_route(x2d, g, wq_bf16, keys_bf16, *, tm):
    t, d = x2d.shape
    col = lambda i: (0, i)
    slot_shape = jax.ShapeDtypeStruct((PEER_SLOTS, t), jnp.int32)
    return pl.pallas_call(
        _peer_route_kernel,
        grid=(t // tm,),
        in_specs=[pl.BlockSpec((tm, d), lambda i: (i, 0)),
                  _resident((1, d)),
                  _resident(wq_bf16.shape),
                  _resident(keys_bf16.shape)],
        out_specs=[pl.BlockSpec((tm, d), lambda i: (i, 0)),
                   pl.BlockSpec((PEER_SLOTS, tm), col),
                   pl.BlockSpec((PEER_SLOTS, tm), col),
                   pl.BlockSpec((PEER_SLOTS, tm), col)],
        out_shape=[jax.ShapeDtypeStruct((t, d), F32), slot_shape, slot_shape,
                   jax.ShapeDtypeStruct((PEER_SLOTS, t), F32)],
        compiler_params=_params(1),
        name="peer_route",
    )(x2d, g.reshape(1, d), wq_bf16, keys_bf16)


_HI_MASK = 0xFFFF0000


def _pack_expert_table(u):
    e, d = u.shape
    bits = lax.bitcast_convert_type(u.astype(BF16), jnp.uint16).astype(jnp.uint32)
    bits = bits.reshape(e // 2, 2, d)
    packed = (bits[:, 0] << 16) | bits[:, 1]
    return packed.reshape(e // 2, d // LANES, LANES)


def _expert_row(tab_ref, pair, shift):
    slab = tab_ref[pair]
    bits = (slab << shift.astype(jnp.uint32)) & jnp.uint32(_HI_MASK)
    return pltpu.bitcast(bits, F32)


def _peer_up_kernel(pair_ref, shift_ref, x_ref, tab_ref, gate_ref, fold_ref, w_ref):
    tm = x_ref.shape[0]
    sub = lax.broadcasted_iota(jnp.int32, (SUBLANES, LANES), 0)

    def token(t, partial):
        x = x_ref[t]
        prods = [_expert_row(tab_ref, pair_ref[t, j], shift_ref[t, j]) * x
                 for j in range(PEER_SLOTS)]
        pmat = jnp.concatenate(prods, axis=0).astype(BF16)
        onehot = (sub == (t % SUBLANES)).astype(BF16)
        return partial + lax.dot_general(onehot, pmat, _NT, preferred_element_type=F32)

    def group(gi, _):
        base = pl.multiple_of(gi * SUBLANES, SUBLANES)
        partial = lax.fori_loop(base, base + SUBLANES, token,
                                jnp.zeros((SUBLANES, PEER_SLOTS * SUBLANES), F32))
        hi, lo = _split_bf16(partial)
        h = (jnp.dot(hi, fold_ref[...], preferred_element_type=F32)
             + jnp.dot(lo, fold_ref[...], preferred_element_type=F32))
        act = 0.5 * h * (1.0 + lax.erf(h * (1.0 / math.sqrt(2.0))))
        w_ref[pl.ds(base, SUBLANES), :] = gate_ref[pl.ds(base, SUBLANES), :] * act
        return 0

    lax.fori_loop(0, tm // SUBLANES, group, 0)


def _peer_down_kernel(pair_ref, shift_ref, w_ref, x_ref, tab_ref, o_ref):
    tm = x_ref.shape[0]
    n_acc = 4

    def token(t, _):
        accs = [x_ref[t]] + [jnp.zeros((SUBLANES, LANES), F32)] * (n_acc - 1)
        for j in range(PEER_SLOTS):
            row = _expert_row(tab_ref, pair_ref[t, j], shift_ref[t, j])
            accs[j % n_acc] = accs[j % n_acc] + w_ref[t, j] * row
        o_ref[t] = (accs[0] + accs[1]) + (accs[2] + accs[3])
        return 0

    lax.fori_loop(0, tm, token, 0)


def _smem_rows(tm):
    return pl.BlockSpec((tm, PEER_SLOTS), lambda i: (i, 0), memory_space=pltpu.SMEM)


def _peer_up(pair, shift, xn3, table, gate, fold, *, tm):
    t = xn3.shape[0]
    rows = pl.BlockSpec((tm, PEER_SLOTS), lambda i: (i, 0))
    return pl.pallas_call(
        _peer_up_kernel,
        grid=(t // tm,),
        in_specs=[_smem_rows(tm), _smem_rows(tm),
                  pl.BlockSpec((tm, SUBLANES, LANES), lambda i: (i, 0, 0)),
                  _resident(table.shape), rows, _resident(fold.shape)],
        out_specs=rows,
        out_shape=jax.ShapeDtypeStruct((t, PEER_SLOTS), F32),
        compiler_params=_params(1),
        name="peer_up",
    )(pair, shift, xn3, table, gate, fold)


def _peer_down(pair, shift, w, x3, table, *, tm):
    t = x3.shape[0]
    tile = pl.BlockSpec((tm, SUBLANES, LANES), lambda i: (i, 0, 0))
    return pl.pallas_call(
        _peer_down_kernel,
        grid=(t // tm,),
        in_specs=[_smem_rows(tm), _smem_rows(tm), _smem_rows(tm), tile, _resident(table.shape)],
        out_specs=tile,
        out_shape=jax.ShapeDtypeStruct(x3.shape, F32),
        compiler_params=_params(1),
        name="peer_down",
    )(pair, shift, w, x3, table)


def _peer(x2d, g, wq_bf16, keys_bf16, u_packed, v_packed, fold, *, tm_route, tm_expert):
    t, d = x2d.shape
    xn, pair, shift, gate = _peer_route(x2d, g, wq_bf16, keys_bf16, tm=tm_route)
    pair, shift, gate = pair.T, shift.T, gate.T
    xn3 = xn.reshape(t, d // LANES, LANES)
    w = _peer_up(pair, shift, xn3, u_packed, gate, fold, tm=tm_expert)
    out3 = _peer_down(pair, shift, w, x2d.reshape(t, d // LANES, LANES), v_packed, tm=tm_expert)
    return out3.reshape(t, d)


def _final_norm_kernel(x_ref, g_ref, o_ref):
    o_ref[...] = _rms_scale(x_ref[...], g_ref[...])


def _final_norm(x2d, g, *, tm):
    t, d = x2d.shape
    return pl.pallas_call(
        _final_norm_kernel,
        grid=(t // tm,),
        in_specs=[pl.BlockSpec((tm, d), lambda i: (i, 0)), _resident((1, d))],
        out_specs=pl.BlockSpec((tm, d), lambda i: (i, 0)),
        out_shape=jax.ShapeDtypeStruct((t, d), F32),
        compiler_params=_params(1),
        name="final_norm",
    )(x2d, g.reshape(1, d))


def _tiles(seq):
    return dict(tm_proj=min(256, seq), tm_out=min(512, seq), tq=min(256, seq),
                tm_route=LANES, tm_expert=LANES, tm_norm=min(512, seq))


def _to_residue_major(a2d, batch, seq, dilation, c0, c1):
    cols = a2d[:, c0:c1].reshape(batch, seq // dilation, dilation, c1 - c0)
    return cols.transpose(0, 2, 1, 3)


def _from_residue_major(a4):
    b, d, length, c = a4.shape
    return a4.transpose(0, 2, 1, 3).reshape(b * d * length, c)


def kernel(x, mem, norm_mix, a_w_in, a_lambda, a_subln, b_w_in, norm_mem, w_mem_kv, w_out,
           norm_ffn, peer_wq, peer_keys, peer_u, peer_v, shared_norm, shared_w_kv, final_norm):
    batch, seq, d = x.shape
    mem_tokens = mem.shape[1]
    depth = norm_mix.shape[0]
    n_a = a_w_in.shape[0]
    t = batch * seq
    tiles = _tiles(seq)
    rope = _rope_tables(seq)
    scale = HEAD_DIM ** -0.5

    diff_qk = DIFF_HEADS * 2 * HEAD_DIM
    dil_qk = len(DIL_GROUPS) * DIL_GROUP_WIDTH
    fold = jnp.repeat(jnp.eye(PEER_SLOTS, dtype=BF16), SUBLANES, axis=0)
    expand = jnp.repeat(jnp.eye(LANES, DIL_HEADS, dtype=BF16), HEAD_DIM, axis=1)

    x2d = x.reshape(t, d)
    mem2d = mem.reshape(batch * mem_tokens, d)
    shared = None

    for l in range(depth):
        memkv = _proj(mem2d, norm_mem[l], w_mem_kv[l].astype(BF16), tm=mem_tokens)
        w_o = w_out[l].astype(BF16)
        if l < n_a:
            col_scale = jnp.concatenate([jnp.full((diff_qk,), scale, F32),
                                         jnp.ones((a_w_in.shape[2] - diff_qk - MEM_Q_WIDTH,), F32),
                                         jnp.full((MEM_Q_WIDTH,), scale, F32)])
            w_in = (a_w_in[l] * col_scale).astype(BF16)
            proj = _proj(x2d, norm_mix[l], w_in, tm=tiles["tm_proj"], n_rope=2 * diff_qk,
                         rope=rope, seq=seq)
            lambda_init = 0.8 - 0.6 * math.exp(-0.3 * l)
            mix = _diff_attention(proj, a_lambda[l], a_subln[l], batch=batch, seq=seq,
                                  lambda_init=lambda_init, tq=tiles["tq"])
            x2d = _outproj(x2d, proj, memkv, w_o, seq=seq, mem_tokens=mem_tokens,
                           tm=tiles["tm_out"], mix=mix)
        else:
            w_in = (b_w_in[l - n_a] * scale).astype(BF16)
            proj = _proj(x2d, norm_mix[l], w_in, tm=tiles["tm_proj"], n_rope=dil_qk,
                         rope=rope, seq=seq)
            outs, stats = [], []
            for gi, (window, dilation) in enumerate(DIL_GROUPS):
                c0 = gi * DIL_GROUP_WIDTH
                blk0 = c0 // LANES
                if dilation == 1:
                    q4, q_off = proj.reshape(batch, 1, seq, proj.shape[1]), blk0
                else:
                    q4, q_off = _to_residue_major(proj, batch, seq, dilation,
                                                  c0, c0 + DIL_GROUP_WIDTH), 0
                k4, k_off, v4, v_off = shared[gi]
                o4, st4 = _dilated_group(q4, k4, v4, q_off=q_off, k_off=k_off, v_off=v_off,
                                         n_steps=window // dilation)
                outs.append(_from_residue_major(o4))
                stats.append(_from_residue_major(st4))
            x2d = _outproj(x2d, proj, memkv, w_o, seq=seq, mem_tokens=mem_tokens,
                           tm=tiles["tm_out"], group_out=outs, group_stats=stats, expand=expand)

        x2d = _peer(x2d, norm_ffn[l], peer_wq[l].astype(BF16),
                    peer_keys[l].reshape(2 * PEER_HEADS, PEER_N_KEYS, PEER_HALF).astype(BF16),
                    _pack_expert_table(peer_u[l]), _pack_expert_table(peer_v[l]), fold,
                    tm_route=tiles["tm_route"], tm_expert=tiles["tm_expert"])

        if l == n_a - 1:
            kv = _proj(x2d, shared_norm, shared_w_kv.astype(BF16), tm=tiles["tm_proj"],
                       n_rope=dil_qk, rope=rope, seq=seq)
            shared = []
            for gi, (window, dilation) in enumerate(DIL_GROUPS):
                c0 = gi * DIL_GROUP_WIDTH
                if dilation == 1:
                    kv4 = kv.reshape(batch, 1, seq, kv.shape[1])
                    shared.append((kv4, c0 // LANES, kv4, (dil_qk + c0) // LANES))
                else:
                    k4 = _to_residue_major(kv, batch, seq, dilation, c0, c0 + DIL_GROUP_WIDTH)
                    v4 = _to_residue_major(kv, batch, seq, dilation, dil_qk + c0,
                                           dil_qk + c0 + DIL_GROUP_WIDTH)
                    shared.append((k4, 0, v4, 0))

    return _final_norm(x2d, final_norm, tm=tiles["tm_norm"]).reshape(batch, seq, d)
```

```python
import functools
import math

import jax
import jax.numpy as jnp
from jax import lax
from jax.experimental import pallas as pl
from jax.experimental.pallas import tpu as pltpu

F32 = jnp.float32
BF16 = jnp.bfloat16

HEAD_DIM = 64
ROPE_DIMS = HEAD_DIM // 4
ROPE_HALF = ROPE_DIMS // 2
ROPE_THETA = 500000.0
NORM_EPS = 1e-5
DIFF_HEADS = 6
MEM_HEADS = 4
MEM_Q_WIDTH = MEM_HEADS * HEAD_DIM
DIL_GROUPS = ((128, 1), (512, 4), (2048, 16))
DIL_HEADS = 12
DIL_BLOCK = 128
DIL_GROUP_WIDTH = DIL_HEADS * HEAD_DIM
PEER_HEADS = 8
PEER_N_KEYS = 128
PEER_TOPK = 16
PEER_HALF = 128
PEER_SLOTS = PEER_HEADS * PEER_TOPK

LANES = 128
SUBLANES = 8
VMEM_LIMIT_BYTES = 56 * 1024 * 1024

_NT = (((1,), (1,)), ((), ()))


def _params(n_axes):
    return pltpu.CompilerParams(dimension_semantics=("arbitrary",) * n_axes,
                                vmem_limit_bytes=VMEM_LIMIT_BYTES)


def _resident(shape):
    zeros = (0,) * len(shape)
    return pl.BlockSpec(shape, lambda *_: zeros, pipeline_mode=pl.Buffered(1))


def _rms_scale(x, g):
    ms = jnp.mean(x * x, axis=-1, keepdims=True)
    return x * lax.rsqrt(ms + NORM_EPS) * g


def _rope_tables(seq):
    inv = ROPE_THETA ** (-jnp.arange(0, ROPE_DIMS, 2, dtype=F32) / ROPE_DIMS)
    ang = jnp.arange(seq, dtype=F32)[:, None] * inv[None, :]
    cos, sin = jnp.cos(ang), jnp.sin(ang)
    ones = jnp.ones((seq, HEAD_DIM - ROPE_DIMS), F32)
    zeros = jnp.zeros((seq, HEAD_DIM - ROPE_DIMS), F32)
    zh = jnp.zeros((seq, ROPE_HALF), F32)
    c = jnp.concatenate([cos, cos, ones], axis=1)
    sa = jnp.concatenate([-sin, zh, zeros], axis=1)
    sb = jnp.concatenate([zh, sin, zeros], axis=1)
    rep = LANES // HEAD_DIM
    return tuple(jnp.tile(t, (1, rep)) for t in (c, sa, sb))


def _proj_kernel(*refs, n_rope, chunk):
    if n_rope:
        x_ref, g_ref, w_ref, c_ref, sa_ref, sb_ref, o_ref = refs
    else:
        x_ref, g_ref, w_ref, o_ref = refs
    y = _rms_scale(x_ref[...], g_ref[...]).astype(BF16)
    n = o_ref.shape[1]
    for c0 in range(0, n, chunk):
        acc = jnp.dot(y, w_ref[:, c0:c0 + chunk], preferred_element_type=F32)
        if c0 < n_rope:
            for k0 in range(0, chunk, LANES):
                a = acc[:, k0:k0 + LANES]
                a = (a * c_ref[...]
                     + pltpu.roll(a, LANES - ROPE_HALF, 1) * sa_ref[...]
                     + pltpu.roll(a, ROPE_HALF, 1) * sb_ref[...])
                o_ref[:, c0 + k0:c0 + k0 + LANES] = a.astype(o_ref.dtype)
        else:
            o_ref[:, c0:c0 + chunk] = acc.astype(o_ref.dtype)


def _proj(x2d, g, w_bf16, *, tm, n_rope=0, rope=None, seq=None):
    t, d = x2d.shape
    n = w_bf16.shape[1]
    chunk = 256
    assert t % tm == 0 and n % chunk == 0 and n_rope % chunk == 0
    in_specs = [pl.BlockSpec((tm, d), lambda i: (i, 0)),
                _resident((1, d)),
                _resident((d, n))]
    args = [x2d, g.reshape(1, d), w_bf16]
    if n_rope:
        nblk = seq // tm
        spec = pl.BlockSpec((tm, LANES), lambda i: (i % nblk, 0))
        in_specs += [spec, spec, spec]
        args += list(rope)
    return pl.pallas_call(
        functools.partial(_proj_kernel, n_rope=n_rope, chunk=chunk),
        grid=(t // tm,),
        in_specs=in_specs,
        out_specs=pl.BlockSpec((tm, n), lambda i: (i, 0)),
        out_shape=jax.ShapeDtypeStruct((t, n), BF16),
        compiler_params=_params(1),
        name="norm_proj",
    )(*args)


def _online_softmax_step(q, k, v, carry, mask):
    m, l, acc = carry
    s = lax.dot_general(q, k, _NT, preferred_element_type=F32)
    if mask is not None:
        s = jnp.where(mask, s, -jnp.inf)
    m_new = jnp.maximum(m, jnp.max(s, axis=-1, keepdims=True))
    alpha = jnp.exp(m - m_new)
    p = jnp.exp(s - m_new)
    l = alpha * l + jnp.sum(p, axis=-1, keepdims=True)
    acc = alpha * acc + jnp.dot(p.astype(BF16), v, preferred_element_type=F32)
    return m_new, l, acc


def _diff_attn_kernel(q_ref, k_ref, v_ref, lp_ref, g_ref, o_ref, *, tq, lambda_init):
    qi = pl.program_id(2)
    q = q_ref[...]
    lane = lax.broadcasted_iota(jnp.int32, q.shape, 1)
    zero = jnp.zeros_like(q)
    qa = jnp.where(lane < HEAD_DIM, q, zero)
    qb = jnp.where(lane >= HEAD_DIM, q, zero)

    def init():
        return (jnp.full((tq, 1), -jnp.inf, F32), jnp.zeros((tq, 1), F32),
                jnp.zeros((tq, 2 * HEAD_DIM), F32))

    def body(j, carry):
        c1, c2 = carry
        k = k_ref[pl.ds(pl.multiple_of(j * tq, tq), tq), :]
        v = v_ref[pl.ds(pl.multiple_of(j * tq, tq), tq), :]
        return (_online_softmax_step(qa, k, v, c1, None),
                _online_softmax_step(qb, k, v, c2, None))

    c1, c2 = lax.fori_loop(0, qi, body, (init(), init()))
    row = lax.broadcasted_iota(jnp.int32, (tq, tq), 0)
    col = lax.broadcasted_iota(jnp.int32, (tq, tq), 1)
    causal = col <= row
    start = pl.multiple_of(qi * tq, tq)
    k = k_ref[pl.ds(start, tq), :]
    v = v_ref[pl.ds(start, tq), :]
    m1, l1, a1 = _online_softmax_step(qa, k, v, c1, causal)
    m2, l2, a2 = _online_softmax_step(qb, k, v, c2, causal)

    lp = lp_ref[...]
    lam = (jnp.exp(jnp.sum(lp[0:1] * lp[1:2], axis=-1, keepdims=True))
           - jnp.exp(jnp.sum(lp[2:3] * lp[3:4], axis=-1, keepdims=True)) + lambda_init)
    o = a1 / l1 - lam * (a2 / l2)
    o = _rms_scale(o, g_ref[...]) * (1.0 - lambda_init)
    o_ref[...] = o.astype(o_ref.dtype)


def _diff_attention(proj, lp, subln, *, batch, seq, lambda_init, tq):
    nq = seq // tq
    width = 2 * HEAD_DIM
    return pl.pallas_call(
        functools.partial(_diff_attn_kernel, tq=tq, lambda_init=lambda_init),
        grid=(batch, DIFF_HEADS, nq),
        in_specs=[
            pl.BlockSpec((tq, width), lambda b, h, i: (b * nq + i, h)),
            pl.BlockSpec((seq, width), lambda b, h, i: (b, DIFF_HEADS + h)),
            pl.BlockSpec((seq, width), lambda b, h, i: (b, 2 * DIFF_HEADS + h)),
            _resident((4, HEAD_DIM)),
            _resident((1, width)),
        ],
        out_specs=pl.BlockSpec((tq, width), lambda b, h, i: (b * nq + i, h)),
        out_shape=jax.ShapeDtypeStruct((batch * seq, DIFF_HEADS * width), BF16),
        compiler_params=_params(3),
        name="diff_attention",
    )(proj, proj, proj, lp, subln.reshape(1, width))


def _dilated_kernel(q_ref, kc_ref, kp_ref, vc_ref, vp_ref, o_ref, st_ref, *, tq, n_steps):
    n = pl.program_id(2)
    hp = pl.program_id(3)
    blk = DIL_BLOCK

    @pl.when(hp == 0)
    def _():
        st_ref[...] = jnp.zeros_like(st_ref)

    qrow = lax.broadcasted_iota(jnp.int32, (blk, 2 * blk), 0)
    kcol = lax.broadcasted_iota(jnp.int32, (blk, 2 * blk), 1)
    dist = qrow + blk - kcol
    band = (dist >= 0) & (dist <= n_steps)
    lane_q = lax.broadcasted_iota(jnp.int32, (blk, LANES), 1)
    lane_s = lax.broadcasted_iota(jnp.int32, (blk, LANES), 1)

    for i in range(tq // blk):
        rows = slice(i * blk, (i + 1) * blk)
        q = q_ref[rows, :]
        if i == 0:
            k_prev, v_prev = kp_ref[...], vp_ref[...]
            valid = band & ((n > 0) | (kcol >= blk))
        else:
            prev = slice((i - 1) * blk, i * blk)
            k_prev, v_prev = kc_ref[prev, :], vc_ref[prev, :]
            valid = band
        keys = jnp.concatenate([k_prev, kc_ref[rows, :]], axis=0)
        vals = jnp.concatenate([v_prev, vc_ref[rows, :]], axis=0)
        out = jnp.zeros((blk, LANES), F32)
        stats = st_ref[rows, :]
        for e in range(LANES // HEAD_DIM):
            head_lanes = (lane_q // HEAD_DIM) == e
            qe = jnp.where(head_lanes, q, jnp.zeros_like(q))
            s = lax.dot_general(qe, keys, _NT, preferred_element_type=F32)
            s = jnp.where(valid, s, -jnp.inf)
            m = jnp.max(s, axis=-1, keepdims=True)
            p = jnp.exp(s - m)
            den = jnp.sum(p, axis=-1, keepdims=True)
            oe = jnp.dot((p / den).astype(BF16), vals, preferred_element_type=F32)
            out = jnp.where(head_lanes, oe, out)
            lse = m + jnp.log(den)
            stats = jnp.where(lane_s == hp * (LANES // HEAD_DIM) + e, lse, stats)
        o_ref[rows, :] = out.astype(o_ref.dtype)
        st_ref[rows, :] = stats


def _dilated_group(q4, k4, v4, *, q_off, k_off, v_off, n_steps):
    b, d, length, _ = q4.shape
    tq = min(512, length)
    assert length % tq == 0 and tq % DIL_BLOCK == 0
    sub = tq // DIL_BLOCK
    pairs = DIL_GROUP_WIDTH // LANES

    def cur(off):
        return pl.BlockSpec((None, None, tq, LANES), lambda bi, r, n, h: (bi, r, n, off + h))

    def prev(off):
        return pl.BlockSpec((None, None, DIL_BLOCK, LANES),
                            lambda bi, r, n, h: (bi, r, jnp.maximum(n * sub - 1, 0), off + h))

    return pl.pallas_call(
        functools.partial(_dilated_kernel, tq=tq, n_steps=n_steps),
        grid=(b, d, length // tq, pairs),
        in_specs=[cur(q_off), cur(k_off), prev(k_off), cur(v_off), prev(v_off)],
        out_specs=[pl.BlockSpec((None, None, tq, LANES), lambda bi, r, n, h: (bi, r, n, h)),
                   pl.BlockSpec((None, None, tq, LANES), lambda bi, r, n, h: (bi, r, n, 0))],
        out_shape=[jax.ShapeDtypeStruct((b, d, length, DIL_GROUP_WIDTH), BF16),
                   jax.ShapeDtypeStruct((b, d, length, LANES), F32)],
        compiler_params=_params(4),
        name="dilated_attention",
    )(q4, k4, k4, v4, v4)


def _memory_attention(q, k, v):
    lane = lax.broadcasted_iota(jnp.int32, q.shape, 1)
    out = jnp.zeros(q.shape, F32)
    for h in range(MEM_HEADS):
        head_lanes = (lane // HEAD_DIM) == h
        qh = jnp.where(head_lanes, q, jnp.zeros_like(q))
        s = lax.dot_general(qh, k, _NT, preferred_element_type=F32)
        m = jnp.max(s, axis=-1, keepdims=True)
        p = jnp.exp(s - m)
        den = jnp.sum(p, axis=-1, keepdims=True)
        oh = jnp.dot((p / den).astype(BF16), v, preferred_element_type=F32)
        out = jnp.where(head_lanes, oh, out)
    return out


def _split_bf16(x):
    hi = x.astype(BF16)
    lo = (x - hi.astype(F32)).astype(BF16)
    return hi, lo


def _outproj_kernel(*refs, n_groups):
    if n_groups:
        x_ref, qm_ref, mk_ref, mv_ref, w_ref = refs[:5]
        o_refs = refs[5:5 + n_groups]
        st_refs = refs[5 + n_groups:5 + 2 * n_groups]
        ex_ref, out_ref = refs[5 + 2 * n_groups:]
        lses = [r[...] for r in st_refs]
        top = functools.reduce(jnp.maximum, lses)
        es = [jnp.exp(l - top) for l in lses]
        den = functools.reduce(lambda a, b: a + b, es)
        mix = None
        for e, o_ref in zip(es, o_refs):
            hi, lo = _split_bf16(e / den)
            wide = (jnp.dot(hi, ex_ref[...], preferred_element_type=F32)
                    + jnp.dot(lo, ex_ref[...], preferred_element_type=F32))
            term = wide * o_ref[...].astype(F32)
            mix = term if mix is None else mix + term
        mix = mix.astype(BF16)
    else:
        x_ref, qm_ref, mk_ref, mv_ref, w_ref, mix_ref, out_ref = refs
        mix = mix_ref[...]
    mo = _memory_attention(qm_ref[...], mk_ref[...], mv_ref[...]).astype(BF16)
    k_mix = mix.shape[1]
    acc = jnp.dot(mix, w_ref[:k_mix, :], preferred_element_type=F32)
    acc += jnp.dot(mo, w_ref[k_mix:, :], preferred_element_type=F32)
    out_ref[...] = x_ref[...] + acc


def _outproj(x2d, proj, memkv, w_bf16, *, seq, mem_tokens, tm, mix=None, group_out=None,
             group_stats=None, expand=None):
    t, d = x2d.shape
    per_batch = seq // tm
    qm_block = (proj.shape[1] - MEM_Q_WIDTH) // MEM_Q_WIDTH
    row = lambda i: (i, 0)
    in_specs = [pl.BlockSpec((tm, d), row),
                pl.BlockSpec((tm, MEM_Q_WIDTH), lambda i: (i, qm_block)),
                pl.BlockSpec((mem_tokens, MEM_Q_WIDTH), lambda i: (i // per_batch, 0)),
                pl.BlockSpec((mem_tokens, MEM_Q_WIDTH), lambda i: (i // per_batch, 1)),
                _resident(w_bf16.shape)]
    args = [x2d, proj, memkv, memkv, w_bf16]
    if mix is not None:
        n_groups = 0
        in_specs.append(pl.BlockSpec((tm, mix.shape[1]), row))
        args.append(mix)
    else:
        n_groups = len(group_out)
        in_specs += [pl.BlockSpec((tm, DIL_GROUP_WIDTH), row)] * n_groups
        in_specs += [pl.BlockSpec((tm, LANES), row)] * n_groups
        in_specs.append(_resident(expand.shape))
        args += list(group_out) + list(group_stats) + [expand]
    return pl.pallas_call(
        functools.partial(_outproj_kernel, n_groups=n_groups),
        grid=(t // tm,),
        in_specs=in_specs,
        out_specs=pl.BlockSpec((tm, d), row),
        out_shape=jax.ShapeDtypeStruct((t, d), F32),
        compiler_params=_params(1),
        name="mix_outproj",
    )(*args)


def _topk_rows(s, k):
    n = s.shape[0]
    row = lax.broadcasted_iota(jnp.int32, s.shape, 0)
    vals, idxs = [], []
    for _ in range(k):
        m = jnp.max(s, axis=0, keepdims=True)
        i = jnp.min(jnp.where(s == m, row, n), axis=0, keepdims=True)
        vals.append(m)
        idxs.append(i)
        s = jnp.where(row == i, -jnp.inf, s)
    return jnp.concatenate(vals, axis=0), jnp.concatenate(idxs, axis=0)


def _select_rows(table, idx):
    row = lax.broadcasted_iota(jnp.int32, table.shape, 0)
    return jnp.sum(jnp.where(row == idx, table, jnp.zeros_like(table)), axis=0, keepdims=True)


def _peer_route_kernel(x_ref, g_ref, wq_ref, keys_ref, xn_ref, pair_ref, shift_ref, gate_ref):
    xn = _rms_scale(x_ref[...], g_ref[...])
    xn_ref[...] = xn
    q = jnp.dot(xn.astype(BF16), wq_ref[...], preferred_element_type=F32).astype(BF16)
    kk = PEER_TOPK
    for h in range(PEER_HEADS):
        tops = []
        for half in range(2):
            c0 = (2 * h + half) * PEER_HALF
            s = lax.dot_general(keys_ref[2 * h + half], q[:, c0:c0 + PEER_HALF], _NT,
                                preferred_element_type=F32)
            tops.append(_topk_rows(s, kk))
        (s1, i1), (s2, i2) = tops
        cand = jnp.concatenate([s1[a:a + 1] + s2 for a in range(kk)], axis=0)
        top_s, top_j = _topk_rows(cand, kk)
        ids = []
        for r in range(kk):
            j = top_j[r:r + 1]
            ids.append(_select_rows(i1, j // kk) * PEER_N_KEYS + _select_rows(i2, j % kk))
        expert = jnp.concatenate(ids, axis=0)
        e = jnp.exp(top_s - top_s[0:1])
        gate = e / jnp.sum(e, axis=0, keepdims=True)
        rows = slice(h * kk, (h + 1) * kk)
        pair_ref[rows, :] = expert // 2
        shift_ref[rows, :] = (expert % 2) * 16
        gate_ref[rows, :] = gate


def _peer_route(x2d, g, wq_bf16, keys_bf16, *, tm):
    t, d = x2d.shape
    col = lambda i: (0, i)
    slot_shape = jax.ShapeDtypeStruct((PEER_SLOTS, t), jnp.int32)
    return pl.pallas_call(
        _peer_route_kernel,
        grid=(t // tm,),
        in_specs=[pl.BlockSpec((tm, d), lambda i: (i, 0)),
                  _resident((1, d)),
                  _resident(wq_bf16.shape),
                  _resident(keys_bf16.shape)],
        out_specs=[pl.BlockSpec((tm, d), lambda i: (i, 0)),
                   pl.BlockSpec((PEER_SLOTS, tm), col),
                   pl.BlockSpec((PEER_SLOTS, tm), col),
                   pl.BlockSpec((PEER_SLOTS, tm), col)],
        out_shape=[jax.ShapeDtypeStruct((t, d), F32), slot_shape, slot_shape,
                   jax.ShapeDtypeStruct((PEER_SLOTS, t), F32)],
        compiler_params=_params(1),
        name="peer_route",
    )(x2d, g.reshape(1, d), wq_bf16, keys_bf16)


_HI_MASK = 0xFFFF0000


def _pack_expert_table(u):
    e, d = u.shape
    bits = lax.bitcast_convert_type(u.astype(BF16), jnp.uint16).astype(jnp.uint32)
    bits = bits.reshape(e // 2, 2, d)
    packed = (bits[:, 0] << 16) | bits[:, 1]
    return packed.reshape(e // 2, d // LANES, LANES)


def _expert_row(tab_ref, pair, shift):
    slab = tab_ref[pair]
    bits = (slab << shift.astype(jnp.uint32)) & jnp.uint32(_HI_MASK)
    return pltpu.bitcast(bits, F32)


def _peer_up_kernel(pair_ref, shift_ref, x_ref, tab_ref, gate_ref, par_ref, fold_ref, spread_ref,
                    wexp_ref, w2_ref):
    tm = x_ref.shape[0]
    sub = lax.broadcasted_iota(jnp.int32, (SUBLANES, LANES), 0)

    def token(t, partial):
        x = x_ref[t]
        prods = [_expert_row(tab_ref, pair_ref[t, j], shift_ref[t, j]) * x
                 for j in range(PEER_SLOTS)]
        pmat = jnp.concatenate(prods, axis=0).astype(BF16)
        onehot = (sub == (t % SUBLANES)).astype(BF16)
        return partial + lax.dot_general(onehot, pmat, _NT, preferred_element_type=F32)

    def group(gi, _):
        base = pl.multiple_of(gi * SUBLANES, SUBLANES)
        partial = lax.fori_loop(base, base + SUBLANES, token,
                                jnp.zeros((SUBLANES, PEER_SLOTS * SUBLANES), F32))
        hi, lo = _split_bf16(partial)
        h = (jnp.dot(hi, fold_ref[...], preferred_element_type=F32)
             + jnp.dot(lo, fold_ref[...], preferred_element_type=F32))
        act = 0.5 * h * (1.0 + lax.erf(h * (1.0 / math.sqrt(2.0))))
        w = gate_ref[pl.ds(base, SUBLANES), :] * act
        odd = par_ref[pl.ds(base, SUBLANES), :] != 0
        zero = jnp.zeros_like(w)
        w2_ref[pl.ds(base, SUBLANES), 0:PEER_SLOTS] = jnp.where(odd, zero, w)
        w2_ref[pl.ds(base, SUBLANES), PEER_SLOTS:2 * PEER_SLOTS] = jnp.where(odd, w, zero)
        return 0

    lax.fori_loop(0, tm // SUBLANES, group, 0)
    hi, lo = _split_bf16(w2_ref[...])
    wexp_ref[...] = (jnp.dot(hi, spread_ref[...], preferred_element_type=F32)
                     + jnp.dot(lo, spread_ref[...], preferred_element_type=F32))


PAIR_ROWS = 2 * SUBLANES
DOWN_K = PEER_SLOTS * PAIR_ROWS
DOWN_PAIRS_PER_STEP = 4


def _peer_down_kernel(pair_ref, wexp_ref, diag_ref, x_ref, tab_ref, o_ref):
    tm = x_ref.shape[0]

    def left_rows(t):
        row = wexp_ref[pl.ds(t, 1), :]
        hi, lo = _split_bf16(row * diag_ref[...])
        return [hi, lo]

    def token_pair(ta):
        tb = ta + 1
        rhs = jnp.concatenate(
            [jnp.concatenate([tab_ref[pair_ref[ta, j]], tab_ref[pair_ref[tb, j]]], axis=1)
             for j in range(PEER_SLOTS)], axis=0)
        lhs = jnp.concatenate(left_rows(ta) + left_rows(tb), axis=0)
        out = jnp.dot(lhs, rhs, preferred_element_type=F32)
        s = SUBLANES
        o_ref[ta] = x_ref[ta] + (out[0:s, :LANES] + out[s:2 * s, :LANES])
        o_ref[tb] = x_ref[tb] + (out[2 * s:3 * s, LANES:] + out[3 * s:4 * s, LANES:])

    def step(i, _):
        for k in range(DOWN_PAIRS_PER_STEP):
            token_pair(2 * (DOWN_PAIRS_PER_STEP * i + k))
        return 0

    lax.fori_loop(0, tm // (2 * DOWN_PAIRS_PER_STEP), step, 0)


def _smem_rows(tm):
    return pl.BlockSpec((tm, PEER_SLOTS), lambda i: (i, 0), memory_space=pltpu.SMEM)


def _peer_up(pair, shift, xn3, table, gate, fold, spread, *, tm):
    t = xn3.shape[0]
    rows = pl.BlockSpec((tm, PEER_SLOTS), lambda i: (i, 0))
    return pl.pallas_call(
        _peer_up_kernel,
        grid=(t // tm,),
        in_specs=[_smem_rows(tm), _smem_rows(tm),
                  pl.BlockSpec((tm, SUBLANES, LANES), lambda i: (i, 0, 0)),
                  _resident(table.shape), rows, rows, _resident(fold.shape),
                  _resident(spread.shape)],
        out_specs=pl.BlockSpec((tm, DOWN_K), lambda i: (i, 0)),
        out_shape=jax.ShapeDtypeStruct((t, DOWN_K), F32),
        scratch_shapes=[pltpu.VMEM((tm, 2 * PEER_SLOTS), F32)],
        compiler_params=_params(1),
        name="peer_up",
    )(pair, shift, xn3, table, gate, shift, fold, spread)


def _peer_down(pair, wexp, diag, x3, table, *, tm):
    t = x3.shape[0]
    tile = pl.BlockSpec((tm, SUBLANES, LANES), lambda i: (i, 0, 0))
    return pl.pallas_call(
        _peer_down_kernel,
        grid=(t // tm,),
        in_specs=[_smem_rows(tm), pl.BlockSpec((tm, DOWN_K), lambda i: (i, 0)),
                  _resident(diag.shape), tile, _resident(table.shape)],
        out_specs=tile,
        out_shape=jax.ShapeDtypeStruct(x3.shape, F32),
        compiler_params=_params(1),
        name="peer_down",
    )(pair, wexp, diag, x3, table)


def _peer_constants():
    fold = jnp.repeat(jnp.eye(PEER_SLOTS, dtype=BF16), SUBLANES, axis=0)
    col = jnp.arange(DOWN_K)
    src = ((col // SUBLANES) % 2) * PEER_SLOTS + col // PAIR_ROWS
    spread = (jnp.arange(2 * PEER_SLOTS)[:, None] == src[None, :]).astype(BF16)
    diag = (col[None, :] % SUBLANES == jnp.arange(SUBLANES)[:, None]).astype(F32)
    return fold, spread, diag


def _pair_slabs(v):
    e, d = v.shape
    return v.astype(BF16).reshape(e // 2, 2 * d // LANES, LANES)


def _peer(x2d, g, wq_bf16, keys_bf16, u_packed, v_slabs, consts, *, tm_route, tm_expert):
    t, d = x2d.shape
    fold, spread, diag = consts
    xn, pair, shift, gate = _peer_route(x2d, g, wq_bf16, keys_bf16, tm=tm_route)
    pair, shift, gate = pair.T, shift.T, gate.T
    xn3 = xn.reshape(t, d // LANES, LANES)
    wexp = _peer_up(pair, shift, xn3, u_packed, gate, fold, spread, tm=tm_expert)
    out3 = _peer_down(pair, wexp, diag, x2d.reshape(t, d // LANES, LANES), v_slabs, tm=tm_expert)
    return out3.reshape(t, d)


def _final_norm_kernel(x_ref, g_ref, o_ref):
    o_ref[...] = _rms_scale(x_ref[...], g_ref[...])


def _final_norm(x2d, g, *, tm):
    t, d = x2d.shape
    return pl.pallas_call(
        _final_norm_kernel,
        grid=(t // tm,),
        in_specs=[pl.BlockSpec((tm, d), lambda i: (i, 0)), _resident((1, d))],
        out_specs=pl.BlockSpec((tm, d), lambda i: (i, 0)),
        out_shape=jax.ShapeDtypeStruct((t, d), F32),
        compiler_params=_params(1),
        name="final_norm",
    )(x2d, g.reshape(1, d))


def _tiles(seq):
    return dict(tm_proj=min(256, seq), tm_out=min(512, seq), tq=min(256, seq),
                tm_route=LANES, tm_expert=LANES, tm_norm=min(512, seq))


def _to_residue_major(a2d, batch, seq, dilation, c0, c1):
    cols = a2d[:, c0:c1].reshape(batch, seq // dilation, dilation, c1 - c0)
    return cols.transpose(0, 2, 1, 3)


def _from_residue_major(a4):
    b, d, length, c = a4.shape
    return a4.transpose(0, 2, 1, 3).reshape(b * d * length, c)


def kernel(x, mem, norm_mix, a_w_in, a_lambda, a_subln, b_w_in, norm_mem, w_mem_kv, w_out,
           norm_ffn, peer_wq, peer_keys, peer_u, peer_v, shared_norm, shared_w_kv, final_norm):
    batch, seq, d = x.shape
    mem_tokens = mem.shape[1]
    depth = norm_mix.shape[0]
    n_a = a_w_in.shape[0]
    t = batch * seq
    tiles = _tiles(seq)
    rope = _rope_tables(seq)
    scale = HEAD_DIM ** -0.5

    diff_qk = DIFF_HEADS * 2 * HEAD_DIM
    dil_qk = len(DIL_GROUPS) * DIL_GROUP_WIDTH
    peer_consts = _peer_constants()
    expand = jnp.repeat(jnp.eye(LANES, DIL_HEADS, dtype=BF16), HEAD_DIM, axis=1)

    x2d = x.reshape(t, d)
    mem2d = mem.reshape(batch * mem_tokens, d)
    shared = None

    for l in range(depth):
        memkv = _proj(mem2d, norm_mem[l], w_mem_kv[l].astype(BF16), tm=mem_tokens)
        w_o = w_out[l].astype(BF16)
        if l < n_a:
            col_scale = jnp.concatenate([jnp.full((diff_qk,), scale, F32),
                                         jnp.ones((a_w_in.shape[2] - diff_qk - MEM_Q_WIDTH,), F32),
                                         jnp.full((MEM_Q_WIDTH,), scale, F32)])
            w_in = (a_w_in[l] * col_scale).astype(BF16)
            proj = _proj(x2d, norm_mix[l], w_in, tm=tiles["tm_proj"], n_rope=2 * diff_qk,
                         rope=rope, seq=seq)
            lambda_init = 0.8 - 0.6 * math.exp(-0.3 * l)
            mix = _diff_attention(proj, a_lambda[l], a_subln[l], batch=batch, seq=seq,
                                  lambda_init=lambda_init, tq=tiles["tq"])
            x2d = _outproj(x2d, proj, memkv, w_o, seq=seq, mem_tokens=mem_tokens,
                           tm=tiles["tm_out"], mix=mix)
        else:
            w_in = (b_w_in[l - n_a] * scale).astype(BF16)
            proj = _proj(x2d, norm_mix[l], w_in, tm=tiles["tm_proj"], n_rope=dil_qk,
                         rope=rope, seq=seq)
            outs, stats = [], []
            for gi, (window, dilation) in enumerate(DIL_GROUPS):
                c0 = gi * DIL_GROUP_WIDTH
                blk0 = c0 // LANES
                if dilation == 1:
                    q4, q_off = proj.reshape(batch, 1, seq, proj.shape[1]), blk0
                else:
                    q4, q_off = _to_residue_major(proj, batch, seq, dilation,
                                                  c0, c0 + DIL_GROUP_WIDTH), 0
                k4, k_off, v4, v_off = shared[gi]
                o4, st4 = _dilated_group(q4, k4, v4, q_off=q_off, k_off=k_off, v_off=v_off,
                                         n_steps=window // dilation)
                outs.append(_from_residue_major(o4))
                stats.append(_from_residue_major(st4))
            x2d = _outproj(x2d, proj, memkv, w_o, seq=seq, mem_tokens=mem_tokens,
                           tm=tiles["tm_out"], group_out=outs, group_stats=stats, expand=expand)

        x2d = _peer(x2d, norm_ffn[l], peer_wq[l].astype(BF16),
                    peer_keys[l].reshape(2 * PEER_HEADS, PEER_N_KEYS, PEER_HALF).astype(BF16),
                    _pack_expert_table(peer_u[l]), _pair_slabs(peer_v[l]), peer_consts,
                    tm_route=tiles["tm_route"], tm_expert=tiles["tm_expert"])

        if l == n_a - 1:
            kv = _proj(x2d, shared_norm, shared_w_kv.astype(BF16), tm=tiles["tm_proj"],
                       n_rope=dil_qk, rope=rope, seq=seq)
            shared = []
            for gi, (window, dilation) in enumerate(DIL_GROUPS):
                c0 = gi * DIL_GROUP_WIDTH
                if dilation == 1:
                    kv4 = kv.reshape(batch, 1, seq, kv.shape[1])
                    shared.append((kv4, c0 // LANES, kv4, (dil_qk + c0) // LANES))
                else:
                    k4 = _to_residue_major(kv, batch, seq, dilation, c0, c0 + DIL_GROUP_WIDTH)
                    v4 = _to_residue_major(kv, batch, seq, dilation, dil_qk + c0,
                                           dil_qk + c0 + DIL_GROUP_WIDTH)
                    shared.append((k4, 0, v4, 0))

    return _final_norm(x2d, final_norm, tm=tiles["tm_norm"]).reshape(batch, seq, d)
```

```python
import functools
import math

import jax
import jax.numpy as jnp
from jax import lax
from jax.experimental import pallas as pl
from jax.experimental.pallas import tpu as pltpu

F32 = jnp.float32
BF16 = jnp.bfloat16

HEAD_DIM = 64
ROPE_DIMS = HEAD_DIM // 4
ROPE_HALF = ROPE_DIMS // 2
ROPE_THETA = 500000.0
NORM_EPS = 1e-5
DIFF_HEADS = 6
MEM_HEADS = 4
MEM_Q_WIDTH = MEM_HEADS * HEAD_DIM
DIL_GROUPS = ((128, 1), (512, 4), (2048, 16))
DIL_HEADS = 12
DIL_BLOCK = 128
DIL_GROUP_WIDTH = DIL_HEADS * HEAD_DIM
PEER_HEADS = 8
PEER_N_KEYS = 128
PEER_TOPK = 16
PEER_HALF = 128
PEER_SLOTS = PEER_HEADS * PEER_TOPK

LANES = 128
SUBLANES = 8
VMEM_LIMIT_BYTES = 56 * 1024 * 1024

_NT = (((1,), (1,)), ((), ()))


def _params(n_axes):
    return pltpu.CompilerParams(dimension_semantics=("arbitrary",) * n_axes,
                                vmem_limit_bytes=VMEM_LIMIT_BYTES)


def _resident(shape):
    zeros = (0,) * len(shape)
    return pl.BlockSpec(shape, lambda *_: zeros, pipeline_mode=pl.Buffered(1))


def _rms_scale(x, g):
    ms = jnp.mean(x * x, axis=-1, keepdims=True)
    return x * lax.rsqrt(ms + NORM_EPS) * g


def _rope_tables(seq):
    inv = ROPE_THETA ** (-jnp.arange(0, ROPE_DIMS, 2, dtype=F32) / ROPE_DIMS)
    ang = jnp.arange(seq, dtype=F32)[:, None] * inv[None, :]
    cos, sin = jnp.cos(ang), jnp.sin(ang)
    ones = jnp.ones((seq, HEAD_DIM - ROPE_DIMS), F32)
    zeros = jnp.zeros((seq, HEAD_DIM - ROPE_DIMS), F32)
    zh = jnp.zeros((seq, ROPE_HALF), F32)
    c = jnp.concatenate([cos, cos, ones], axis=1)
    sa = jnp.concatenate([-sin, zh, zeros], axis=1)
    sb = jnp.concatenate([zh, sin, zeros], axis=1)
    rep = LANES // HEAD_DIM
    return tuple(jnp.tile(t, (1, rep)) for t in (c, sa, sb))


def _proj_kernel(*refs, n_rope, chunk):
    if n_rope:
        x_ref, g_ref, w_ref, c_ref, sa_ref, sb_ref, o_ref = refs
    else:
        x_ref, g_ref, w_ref, o_ref = refs
    y = _rms_scale(x_ref[...], g_ref[...]).astype(BF16)
    n = o_ref.shape[1]
    for c0 in range(0, n, chunk):
        acc = jnp.dot(y, w_ref[:, c0:c0 + chunk], preferred_element_type=F32)
        if c0 < n_rope:
            for k0 in range(0, chunk, LANES):
                a = acc[:, k0:k0 + LANES]
                a = (a * c_ref[...]
                     + pltpu.roll(a, LANES - ROPE_HALF, 1) * sa_ref[...]
                     + pltpu.roll(a, ROPE_HALF, 1) * sb_ref[...])
                o_ref[:, c0 + k0:c0 + k0 + LANES] = a.astype(o_ref.dtype)
        else:
            o_ref[:, c0:c0 + chunk] = acc.astype(o_ref.dtype)


def _proj(x2d, g, w_bf16, *, tm, n_rope=0, rope=None, seq=None):
    t, d = x2d.shape
    n = w_bf16.shape[1]
    chunk = 256
    assert t % tm == 0 and n % chunk == 0 and n_rope % chunk == 0
    in_specs = [pl.BlockSpec((tm, d), lambda i: (i, 0)),
                _resident((1, d)),
                _resident((d, n))]
    args = [x2d, g.reshape(1, d), w_bf16]
    if n_rope:
        nblk = seq // tm
        spec = pl.BlockSpec((tm, LANES), lambda i: (i % nblk, 0))
        in_specs += [spec, spec, spec]
        args += list(rope)
    return pl.pallas_call(
        functools.partial(_proj_kernel, n_rope=n_rope, chunk=chunk),
        grid=(t // tm,),
        in_specs=in_specs,
        out_specs=pl.BlockSpec((tm, n), lambda i: (i, 0)),
        out_shape=jax.ShapeDtypeStruct((t, n), BF16),
        compiler_params=_params(1),
        name="norm_proj",
    )(*args)


def _online_softmax_step(q, k, v, carry, mask):
    m, l, acc = carry
    s = lax.dot_general(q, k, _NT, preferred_element_type=F32)
    if mask is not None:
        s = jnp.where(mask, s, -jnp.inf)
    m_new = jnp.maximum(m, jnp.max(s, axis=-1, keepdims=True))
    alpha = jnp.exp(m - m_new)
    p = jnp.exp(s - m_new)
    l = alpha * l + jnp.sum(p, axis=-1, keepdims=True)
    acc = alpha * acc + jnp.dot(p.astype(BF16), v, preferred_element_type=F32)
    return m_new, l, acc


def _diff_attn_kernel(q_ref, k_ref, v_ref, lp_ref, g_ref, o_ref, *, tq, lambda_init):
    qi = pl.program_id(2)
    q = q_ref[...]
    lane = lax.broadcasted_iota(jnp.int32, q.shape, 1)
    zero = jnp.zeros_like(q)
    qa = jnp.where(lane < HEAD_DIM, q, zero)
    qb = jnp.where(lane >= HEAD_DIM, q, zero)

    def init():
        return (jnp.full((tq, 1), -jnp.inf, F32), jnp.zeros((tq, 1), F32),
                jnp.zeros((tq, 2 * HEAD_DIM), F32))

    def body(j, carry):
        c1, c2 = carry
        k = k_ref[pl.ds(pl.multiple_of(j * tq, tq), tq), :]
        v = v_ref[pl.ds(pl.multiple_of(j * tq, tq), tq), :]
        return (_online_softmax_step(qa, k, v, c1, None),
                _online_softmax_step(qb, k, v, c2, None))

    c1, c2 = lax.fori_loop(0, qi, body, (init(), init()))
    row = lax.broadcasted_iota(jnp.int32, (tq, tq), 0)
    col = lax.broadcasted_iota(jnp.int32, (tq, tq), 1)
    causal = col <= row
    start = pl.multiple_of(qi * tq, tq)
    k = k_ref[pl.ds(start, tq), :]
    v = v_ref[pl.ds(start, tq), :]
    m1, l1, a1 = _online_softmax_step(qa, k, v, c1, causal)
    m2, l2, a2 = _online_softmax_step(qb, k, v, c2, causal)

    lp = lp_ref[...]
    lam = (jnp.exp(jnp.sum(lp[0:1] * lp[1:2], axis=-1, keepdims=True))
           - jnp.exp(jnp.sum(lp[2:3] * lp[3:4], axis=-1, keepdims=True)) + lambda_init)
    o = a1 / l1 - lam * (a2 / l2)
    o = _rms_scale(o, g_ref[...]) * (1.0 - lambda_init)
    o_ref[...] = o.astype(o_ref.dtype)


def _diff_attention(proj, lp, subln, *, batch, seq, lambda_init, tq):
    nq = seq // tq
    width = 2 * HEAD_DIM
    return pl.pallas_call(
        functools.partial(_diff_attn_kernel, tq=tq, lambda_init=lambda_init),
        grid=(batch, DIFF_HEADS, nq),
        in_specs=[
            pl.BlockSpec((tq, width), lambda b, h, i: (b * nq + i, h)),
            pl.BlockSpec((seq, width), lambda b, h, i: (b, DIFF_HEADS + h)),
            pl.BlockSpec((seq, width), lambda b, h, i: (b, 2 * DIFF_HEADS + h)),
            _resident((4, HEAD_DIM)),
            _resident((1, width)),
        ],
        out_specs=pl.BlockSpec((tq, width), lambda b, h, i: (b * nq + i, h)),
        out_shape=jax.ShapeDtypeStruct((batch * seq, DIFF_HEADS * width), BF16),
        compiler_params=_params(3),
        name="diff_attention",
    )(proj, proj, proj, lp, subln.reshape(1, width))


def _dilated_kernel(q_ref, kc_ref, kp_ref, vc_ref, vp_ref, o_ref, st_ref, *, tq, n_steps):
    n = pl.program_id(2)
    hp = pl.program_id(3)
    blk = DIL_BLOCK

    @pl.when(hp == 0)
    def _():
        st_ref[...] = jnp.zeros_like(st_ref)

    qrow = lax.broadcasted_iota(jnp.int32, (blk, 2 * blk), 0)
    kcol = lax.broadcasted_iota(jnp.int32, (blk, 2 * blk), 1)
    dist = qrow + blk - kcol
    band = (dist >= 0) & (dist <= n_steps)
    lane_q = lax.broadcasted_iota(jnp.int32, (blk, LANES), 1)
    lane_s = lax.broadcasted_iota(jnp.int32, (blk, LANES), 1)

    for i in range(tq // blk):
        rows = slice(i * blk, (i + 1) * blk)
        q = q_ref[rows, :]
        if i == 0:
            k_prev, v_prev = kp_ref[...], vp_ref[...]
            valid = band & ((n > 0) | (kcol >= blk))
        else:
            prev = slice((i - 1) * blk, i * blk)
            k_prev, v_prev = kc_ref[prev, :], vc_ref[prev, :]
            valid = band
        keys = jnp.concatenate([k_prev, kc_ref[rows, :]], axis=0)
        vals = jnp.concatenate([v_prev, vc_ref[rows, :]], axis=0)
        out = jnp.zeros((blk, LANES), F32)
        stats = st_ref[rows, :]
        for e in range(LANES // HEAD_DIM):
            head_lanes = (lane_q // HEAD_DIM) == e
            qe = jnp.where(head_lanes, q, jnp.zeros_like(q))
            s = lax.dot_general(qe, keys, _NT, preferred_element_type=F32)
            s = jnp.where(valid, s, -jnp.inf)
            m = jnp.max(s, axis=-1, keepdims=True)
            p = jnp.exp(s - m)
            den = jnp.sum(p, axis=-1, keepdims=True)
            oe = jnp.dot((p / den).astype(BF16), vals, preferred_element_type=F32)
            out = jnp.where(head_lanes, oe, out)
            lse = m + jnp.log(den)
            stats = jnp.where(lane_s == hp * (LANES // HEAD_DIM) + e, lse, stats)
        o_ref[rows, :] = out.astype(o_ref.dtype)
        st_ref[rows, :] = stats


def _dilated_group(q4, k4, v4, *, q_off, k_off, v_off, n_steps):
    b, d, length, _ = q4.shape
    tq = min(512, length)
    assert length % tq == 0 and tq % DIL_BLOCK == 0
    sub = tq // DIL_BLOCK
    pairs = DIL_GROUP_WIDTH // LANES

    def cur(off):
        return pl.BlockSpec((None, None, tq, LANES), lambda bi, r, n, h: (bi, r, n, off + h))

    def prev(off):
        return pl.BlockSpec((None, None, DIL_BLOCK, LANES),
                            lambda bi, r, n, h: (bi, r, jnp.maximum(n * sub - 1, 0), off + h))

    return pl.pallas_call(
        functools.partial(_dilated_kernel, tq=tq, n_steps=n_steps),
        grid=(b, d, length // tq, pairs),
        in_specs=[cur(q_off), cur(k_off), prev(k_off), cur(v_off), prev(v_off)],
        out_specs=[pl.BlockSpec((None, None, tq, LANES), lambda bi, r, n, h: (bi, r, n, h)),
                   pl.BlockSpec((None, None, tq, LANES), lambda bi, r, n, h: (bi, r, n, 0))],
        out_shape=[jax.ShapeDtypeStruct((b, d, length, DIL_GROUP_WIDTH), BF16),
                   jax.ShapeDtypeStruct((b, d, length, LANES), F32)],
        compiler_params=_params(4),
        name="dilated_attention",
    )(q4, k4, k4, v4, v4)


def _memory_attention(q, k, v):
    lane = lax.broadcasted_iota(jnp.int32, q.shape, 1)
    out = jnp.zeros(q.shape, F32)
    for h in range(MEM_HEADS):
        head_lanes = (lane // HEAD_DIM) == h
        qh = jnp.where(head_lanes, q, jnp.zeros_like(q))
        s = lax.dot_general(qh, k, _NT, preferred_element_type=F32)
        m = jnp.max(s, axis=-1, keepdims=True)
        p = jnp.exp(s - m)
        den = jnp.sum(p, axis=-1, keepdims=True)
        oh = jnp.dot((p / den).astype(BF16), v, preferred_element_type=F32)
        out = jnp.where(head_lanes, oh, out)
    return out


def _split_bf16(x):
    hi = x.astype(BF16)
    lo = (x - hi.astype(F32)).astype(BF16)
    return hi, lo


def _outproj_kernel(*refs, n_groups):
    if n_groups:
        x_ref, qm_ref, mk_ref, mv_ref, w_ref = refs[:5]
        o_refs = refs[5:5 + n_groups]
        st_refs = refs[5 + n_groups:5 + 2 * n_groups]
        ex_ref, out_ref = refs[5 + 2 * n_groups:]
        lses = [r[...] for r in st_refs]
        top = functools.reduce(jnp.maximum, lses)
        es = [jnp.exp(l - top) for l in lses]
        den = functools.reduce(lambda a, b: a + b, es)
        mix = None
        for e, o_ref in zip(es, o_refs):
            hi, lo = _split_bf16(e / den)
            wide = (jnp.dot(hi, ex_ref[...], preferred_element_type=F32)
                    + jnp.dot(lo, ex_ref[...], preferred_element_type=F32))
            term = wide * o_ref[...].astype(F32)
            mix = term if mix is None else mix + term
        mix = mix.astype(BF16)
    else:
        x_ref, qm_ref, mk_ref, mv_ref, w_ref, mix_ref, out_ref = refs
        mix = mix_ref[...]
    mo = _memory_attention(qm_ref[...], mk_ref[...], mv_ref[...]).astype(BF16)
    k_mix = mix.shape[1]
    acc = jnp.dot(mix, w_ref[:k_mix, :], preferred_element_type=F32)
    acc += jnp.dot(mo, w_ref[k_mix:, :], preferred_element_type=F32)
    out_ref[...] = x_ref[...] + acc


def _outproj(x2d, proj, memkv, w_bf16, *, seq, mem_tokens, tm, mix=None, group_out=None,
             group_stats=None, expand=None):
    t, d = x2d.shape
    per_batch = seq // tm
    qm_block = (proj.shape[1] - MEM_Q_WIDTH) // MEM_Q_WIDTH
    row = lambda i: (i, 0)
    in_specs = [pl.BlockSpec((tm, d), row),
                pl.BlockSpec((tm, MEM_Q_WIDTH), lambda i: (i, qm_block)),
                pl.BlockSpec((mem_tokens, MEM_Q_WIDTH), lambda i: (i // per_batch, 0)),
                pl.BlockSpec((mem_tokens, MEM_Q_WIDTH), lambda i: (i // per_batch, 1)),
                _resident(w_bf16.shape)]
    args = [x2d, proj, memkv, memkv, w_bf16]
    if mix is not None:
        n_groups = 0
        in_specs.append(pl.BlockSpec((tm, mix.shape[1]), row))
        args.append(mix)
    else:
        n_groups = len(group_out)
        in_specs += [pl.BlockSpec((tm, DIL_GROUP_WIDTH), row)] * n_groups
        in_specs += [pl.BlockSpec((tm, LANES), row)] * n_groups
        in_specs.append(_resident(expand.shape))
        args += list(group_out) + list(group_stats) + [expand]
    return pl.pallas_call(
        functools.partial(_outproj_kernel, n_groups=n_groups),
        grid=(t // tm,),
        in_specs=in_specs,
        out_specs=pl.BlockSpec((tm, d), row),
        out_shape=jax.ShapeDtypeStruct((t, d), F32),
        compiler_params=_params(1),
        name="mix_outproj",
    )(*args)


def _topk_rows(s, k):
    n = s.shape[0]
    row = lax.broadcasted_iota(jnp.int32, s.shape, 0)
    vals, idxs = [], []
    for _ in range(k):
        m = jnp.max(s, axis=0, keepdims=True)
        i = jnp.min(jnp.where(s == m, row, n), axis=0, keepdims=True)
        vals.append(m)
        idxs.append(i)
        s = jnp.where(row == i, -jnp.inf, s)
    return jnp.concatenate(vals, axis=0), jnp.concatenate(idxs, axis=0)


def _select_rows(table, idx):
    row = lax.broadcasted_iota(jnp.int32, table.shape, 0)
    return jnp.sum(jnp.where(row == idx, table, jnp.zeros_like(table)), axis=0, keepdims=True)


def _peer_route_kernel(x_ref, g_ref, wq_ref, keys_ref, xn_ref, pair_ref, parity_ref, gate_ref):
    xn = _rms_scale(x_ref[...], g_ref[...])
    xn_ref[...] = xn
    q = jnp.dot(xn.astype(BF16), wq_ref[...], preferred_element_type=F32).astype(BF16)
    kk = PEER_TOPK
    for h in range(PEER_HEADS):
        tops = []
        for half in range(2):
            c0 = (2 * h + half) * PEER_HALF
            s = lax.dot_general(keys_ref[2 * h + half], q[:, c0:c0 + PEER_HALF], _NT,
                                preferred_element_type=F32)
            tops.append(_topk_rows(s, kk))
        (s1, i1), (s2, i2) = tops
        cand = jnp.concatenate([s1[a:a + 1] + s2 for a in range(kk)], axis=0)
        top_s, top_j = _topk_rows(cand, kk)
        ids = []
        for r in range(kk):
            j = top_j[r:r + 1]
            ids.append(_select_rows(i1, j // kk) * PEER_N_KEYS + _select_rows(i2, j % kk))
        expert = jnp.concatenate(ids, axis=0)
        e = jnp.exp(top_s - top_s[0:1])
        gate = e / jnp.sum(e, axis=0, keepdims=True)
        rows = slice(h * kk, (h + 1) * kk)
        pair_ref[rows, :] = expert >> 1
        parity_ref[rows, :] = expert & 1
        gate_ref[rows, :] = gate


def _peer_route(x2d, g, wq_bf16, keys_bf16, *, tm):
    t, d = x2d.shape
    col = lambda i: (0, i)
    slot_shape = jax.ShapeDtypeStruct((PEER_SLOTS, t), jnp.int32)
    return pl.pallas_call(
        _peer_route_kernel,
        grid=(t // tm,),
        in_specs=[pl.BlockSpec((tm, d), lambda i: (i, 0)),
                  _resident((1, d)),
                  _resident(wq_bf16.shape),
                  _resident(keys_bf16.shape)],
        out_specs=[pl.BlockSpec((tm, d), lambda i: (i, 0)),
                   pl.BlockSpec((PEER_SLOTS, tm), col),
                   pl.BlockSpec((PEER_SLOTS, tm), col),
                   pl.BlockSpec((PEER_SLOTS, tm), col)],
        out_shape=[jax.ShapeDtypeStruct((t, d), F32), slot_shape, slot_shape,
                   jax.ShapeDtypeStruct((PEER_SLOTS, t), F32)],
        compiler_params=_params(1),
        name="peer_route",
    )(x2d, g.reshape(1, d), wq_bf16, keys_bf16)


_HI_MASK = 0xFFFF0000


def _pack_expert_table(u):
    e, d = u.shape
    bits = lax.bitcast_convert_type(u.astype(BF16), jnp.uint16).astype(jnp.uint32)
    bits = bits.reshape(e // 2, 2, d)
    packed = (bits[:, 0] << 16) | bits[:, 1]
    return packed.reshape(e // 2, d // LANES, LANES)


_BIT_REVERSED = (0, 4, 2, 6, 1, 5, 3, 7)


def _bf16_pair_words(x):
    bits = pltpu.bitcast(x, jnp.uint32)
    top = (bits + jnp.uint32(0x7FFF) + ((bits >> 16) & jnp.uint32(1))) >> 16
    return (top << 16) | top


def _packed_add(a, b):
    return pltpu.bitcast(pltpu.bitcast(a, BF16) + pltpu.bitcast(b, BF16), jnp.uint32)


def _sublane_sums(words, sub):
    level = [words[i] for i in _BIT_REVERSED]
    for k in (4, 2, 1):
        low = (sub & k) == 0
        merged = []
        for a, b in zip(level[0::2], level[1::2]):
            if k == 4:
                merged.append(_packed_add(jnp.where(low, a, b),
                                          pltpu.roll(jnp.where(low, b, a), k, 0)))
            else:
                merged.append(jnp.where(low,
                                        _packed_add(a, pltpu.roll(a, SUBLANES - k, 0)),
                                        _packed_add(b, pltpu.roll(b, k, 0))))
        level = merged
    return level[0]


def _peer_up_kernel(pair_ref, x_ref, tab_ref, gate_ref, par_ref, spread_ref, wexp_ref, w2_ref):
    tm = x_ref.shape[0]
    sub = lax.broadcasted_iota(jnp.int32, (SUBLANES, LANES), 0)
    hi_mask = jnp.uint32(_HI_MASK)

    def token_sums(t, m):
        xx = pltpu.bitcast(_bf16_pair_words(x_ref[t]), BF16)
        even, odd = [], []
        for g in range(PEER_SLOTS // SUBLANES):
            words = []
            for i in range(SUBLANES):
                slab = pltpu.bitcast(tab_ref[pair_ref[t, g * SUBLANES + i]], BF16)
                words.append(pltpu.bitcast(slab * xx, jnp.uint32))
            q = _sublane_sums(words, sub)
            even.append(pltpu.bitcast(q & hi_mask, F32))
            odd.append(pltpu.bitcast(q << 16, F32))
        lane_parts = jnp.concatenate(even + odd, axis=0).astype(BF16)
        onehot = (sub == m).astype(BF16)
        return lax.dot_general(onehot, lane_parts, _NT, preferred_element_type=F32)

    def group(gi, _):
        base = pl.multiple_of(gi * SUBLANES, SUBLANES)
        h2 = token_sums(base, 0)
        for m in range(1, SUBLANES):
            h2 = h2 + token_sums(base + m, m)
        odd_slot = par_ref[pl.ds(base, SUBLANES), :] != 0
        h = jnp.where(odd_slot, h2[:, PEER_SLOTS:], h2[:, :PEER_SLOTS])
        act = 0.5 * h * (1.0 + lax.erf(h * (1.0 / math.sqrt(2.0))))
        w = gate_ref[pl.ds(base, SUBLANES), :] * act
        zero = jnp.zeros_like(w)
        w2_ref[pl.ds(base, SUBLANES), 0:PEER_SLOTS] = jnp.where(odd_slot, zero, w)
        w2_ref[pl.ds(base, SUBLANES), PEER_SLOTS:2 * PEER_SLOTS] = jnp.where(odd_slot, w, zero)
        return 0

    lax.fori_loop(0, tm // SUBLANES, group, 0)
    hi, lo = _split_bf16(w2_ref[...])
    wexp_ref[...] = (jnp.dot(hi, spread_ref[...], preferred_element_type=F32)
                     + jnp.dot(lo, spread_ref[...], preferred_element_type=F32))


PAIR_ROWS = 2 * SUBLANES
DOWN_K = PEER_SLOTS * PAIR_ROWS
DOWN_PAIRS_PER_STEP = 4


def _peer_down_kernel(pair_ref, wexp_ref, diag_ref, x_ref, tab_ref, o_ref):
    tm = x_ref.shape[0]

    def left_rows(t):
        row = wexp_ref[pl.ds(t, 1), :]
        hi, lo = _split_bf16(row * diag_ref[...])
        return [hi, lo]

    def token_pair(ta):
        tb = ta + 1
        rhs = jnp.concatenate(
            [jnp.concatenate([tab_ref[pair_ref[ta, j]], tab_ref[pair_ref[tb, j]]], axis=1)
             for j in range(PEER_SLOTS)], axis=0)
        lhs = jnp.concatenate(left_rows(ta) + left_rows(tb), axis=0)
        out = jnp.dot(lhs, rhs, preferred_element_type=F32)
        s = SUBLANES
        o_ref[ta] = x_ref[ta] + (out[0:s, :LANES] + out[s:2 * s, :LANES])
        o_ref[tb] = x_ref[tb] + (out[2 * s:3 * s, LANES:] + out[3 * s:4 * s, LANES:])

    def step(i, _):
        for k in range(DOWN_PAIRS_PER_STEP):
            token_pair(2 * (DOWN_PAIRS_PER_STEP * i + k))
        return 0

    lax.fori_loop(0, tm // (2 * DOWN_PAIRS_PER_STEP), step, 0)


def _smem_rows(tm):
    return pl.BlockSpec((tm, PEER_SLOTS), lambda i: (i, 0), memory_space=pltpu.SMEM)


def _peer_up(pair, parity, xn3, table, gate, spread, *, tm):
    t = xn3.shape[0]
    rows = pl.BlockSpec((tm, PEER_SLOTS), lambda i: (i, 0))
    return pl.pallas_call(
        _peer_up_kernel,
        grid=(t // tm,),
        in_specs=[_smem_rows(tm),
                  pl.BlockSpec((tm, SUBLANES, LANES), lambda i: (i, 0, 0)),
                  _resident(table.shape), rows, rows, _resident(spread.shape)],
        out_specs=pl.BlockSpec((tm, DOWN_K), lambda i: (i, 0)),
        out_shape=jax.ShapeDtypeStruct((t, DOWN_K), F32),
        scratch_shapes=[pltpu.VMEM((tm, 2 * PEER_SLOTS), F32)],
        compiler_params=_params(1),
        name="peer_up",
    )(pair, xn3, table, gate, parity, spread)


def _peer_down(pair, wexp, diag, x3, table, *, tm):
    t = x3.shape[0]
    tile = pl.BlockSpec((tm, SUBLANES, LANES), lambda i: (i, 0, 0))
    return pl.pallas_call(
        _peer_down_kernel,
        grid=(t // tm,),
        in_specs=[_smem_rows(tm), pl.BlockSpec((tm, DOWN_K), lambda i: (i, 0)),
                  _resident(diag.shape), tile, _resident(table.shape)],
        out_specs=tile,
        out_shape=jax.ShapeDtypeStruct(x3.shape, F32),
        compiler_params=_params(1),
        name="peer_down",
    )(pair, wexp, diag, x3, table)


def _peer_constants():
    col = jnp.arange(DOWN_K)
    src = ((col // SUBLANES) % 2) * PEER_SLOTS + col // PAIR_ROWS
    spread = (jnp.arange(2 * PEER_SLOTS)[:, None] == src[None, :]).astype(BF16)
    diag = (col[None, :] % SUBLANES == jnp.arange(SUBLANES)[:, None]).astype(F32)
    return spread, diag


def _pair_slabs(v):
    e, d = v.shape
    return v.astype(BF16).reshape(e // 2, 2 * d // LANES, LANES)


def _peer(x2d, g, wq_bf16, keys_bf16, u_packed, v_slabs, consts, *, tm_route, tm_expert):
    t, d = x2d.shape
    spread, diag = consts
    xn, pair, parity, gate = _peer_route(x2d, g, wq_bf16, keys_bf16, tm=tm_route)
    pair, parity, gate = pair.T, parity.T, gate.T
    xn3 = xn.reshape(t, d // LANES, LANES)
    wexp = _peer_up(pair, parity, xn3, u_packed, gate, spread, tm=tm_expert)
    out3 = _peer_down(pair, wexp, diag, x2d.reshape(t, d // LANES, LANES), v_slabs, tm=tm_expert)
    return out3.reshape(t, d)


def _final_norm_kernel(x_ref, g_ref, o_ref):
    o_ref[...] = _rms_scale(x_ref[...], g_ref[...])


def _final_norm(x2d, g, *, tm):
    t, d = x2d.shape
    return pl.pallas_call(
        _final_norm_kernel,
        grid=(t // tm,),
        in_specs=[pl.BlockSpec((tm, d), lambda i: (i, 0)), _resident((1, d))],
        out_specs=pl.BlockSpec((tm, d), lambda i: (i, 0)),
        out_shape=jax.ShapeDtypeStruct((t, d), F32),
        compiler_params=_params(1),
        name="final_norm",
    )(x2d, g.reshape(1, d))


def _tiles(seq):
    return dict(tm_proj=min(256, seq), tm_out=min(512, seq), tq=min(256, seq),
                tm_route=LANES, tm_expert=LANES, tm_norm=min(512, seq))


def _to_residue_major(a2d, batch, seq, dilation, c0, c1):
    cols = a2d[:, c0:c1].reshape(batch, seq // dilation, dilation, c1 - c0)
    return cols.transpose(0, 2, 1, 3)


def _from_residue_major(a4):
    b, d, length, c = a4.shape
    return a4.transpose(0, 2, 1, 3).reshape(b * d * length, c)


def kernel(x, mem, norm_mix, a_w_in, a_lambda, a_subln, b_w_in, norm_mem, w_mem_kv, w_out,
           norm_ffn, peer_wq, peer_keys, peer_u, peer_v, shared_norm, shared_w_kv, final_norm):
    batch, seq, d = x.shape
    mem_tokens = mem.shape[1]
    depth = norm_mix.shape[0]
    n_a = a_w_in.shape[0]
    t = batch * seq
    tiles = _tiles(seq)
    rope = _rope_tables(seq)
    scale = HEAD_DIM ** -0.5

    diff_qk = DIFF_HEADS * 2 * HEAD_DIM
    dil_qk = len(DIL_GROUPS) * DIL_GROUP_WIDTH
    peer_consts = _peer_constants()
    expand = jnp.repeat(jnp.eye(LANES, DIL_HEADS, dtype=BF16), HEAD_DIM, axis=1)

    x2d = x.reshape(t, d)
    mem2d = mem.reshape(batch * mem_tokens, d)
    shared = None

    for l in range(depth):
        memkv = _proj(mem2d, norm_mem[l], w_mem_kv[l].astype(BF16), tm=mem_tokens)
        w_o = w_out[l].astype(BF16)
        if l < n_a:
            col_scale = jnp.concatenate([jnp.full((diff_qk,), scale, F32),
                                         jnp.ones((a_w_in.shape[2] - diff_qk - MEM_Q_WIDTH,), F32),
                                         jnp.full((MEM_Q_WIDTH,), scale, F32)])
            w_in = (a_w_in[l] * col_scale).astype(BF16)
            proj = _proj(x2d, norm_mix[l], w_in, tm=tiles["tm_proj"], n_rope=2 * diff_qk,
                         rope=rope, seq=seq)
            lambda_init = 0.8 - 0.6 * math.exp(-0.3 * l)
            mix = _diff_attention(proj, a_lambda[l], a_subln[l], batch=batch, seq=seq,
                                  lambda_init=lambda_init, tq=tiles["tq"])
            x2d = _outproj(x2d, proj, memkv, w_o, seq=seq, mem_tokens=mem_tokens,
                           tm=tiles["tm_out"], mix=mix)
        else:
            w_in = (b_w_in[l - n_a] * scale).astype(BF16)
            proj = _proj(x2d, norm_mix[l], w_in, tm=tiles["tm_proj"], n_rope=dil_qk,
                         rope=rope, seq=seq)
            outs, stats = [], []
            for gi, (window, dilation) in enumerate(DIL_GROUPS):
                c0 = gi * DIL_GROUP_WIDTH
                blk0 = c0 // LANES
                if dilation == 1:
                    q4, q_off = proj.reshape(batch, 1, seq, proj.shape[1]), blk0
                else:
                    q4, q_off = _to_residue_major(proj, batch, seq, dilation,
                                                  c0, c0 + DIL_GROUP_WIDTH), 0
                k4, k_off, v4, v_off = shared[gi]
                o4, st4 = _dilated_group(q4, k4, v4, q_off=q_off, k_off=k_off, v_off=v_off,
                                         n_steps=window // dilation)
                outs.append(_from_residue_major(o4))
                stats.append(_from_residue_major(st4))
            x2d = _outproj(x2d, proj, memkv, w_o, seq=seq, mem_tokens=mem_tokens,
                           tm=tiles["tm_out"], group_out=outs, group_stats=stats, expand=expand)

        x2d = _peer(x2d, norm_ffn[l], peer_wq[l].astype(BF16),
                    peer_keys[l].reshape(2 * PEER_HEADS, PEER_N_KEYS, PEER_HALF).astype(BF16),
                    _pack_expert_table(peer_u[l]), _pair_slabs(peer_v[l]), peer_consts,
                    tm_route=tiles["tm_route"], tm_expert=tiles["tm_expert"])

        if l == n_a - 1:
            kv = _proj(x2d, shared_norm, shared_w_kv.astype(BF16), tm=tiles["tm_proj"],
                       n_rope=dil_qk, rope=rope, seq=seq)
            shared = []
            for gi, (window, dilation) in enumerate(DIL_GROUPS):
                c0 = gi * DIL_GROUP_WIDTH
                if dilation == 1:
                    kv4 = kv.reshape(batch, 1, seq, kv.shape[1])
                    shared.append((kv4, c0 // LANES, kv4, (dil_qk + c0) // LANES))
                else:
                    k4 = _to_residue_major(kv, batch, seq, dilation, c0, c0 + DIL_GROUP_WIDTH)
                    v4 = _to_residue_major(kv, batch, seq, dilation, dil_qk + c0,
                                           dil_qk + c0 + DIL_GROUP_WIDTH)
                    shared.append((k4, 0, v4, 0))

    return _final_norm(x2d, final_norm, tm=tiles["tm_norm"]).reshape(batch, seq, d)
```

```python
import functools
import math

import jax
import jax.numpy as jnp
from jax import lax
from jax.experimental import pallas as pl
from jax.experimental.pallas import tpu as pltpu

F32 = jnp.float32
BF16 = jnp.bfloat16

HEAD_DIM = 64
ROPE_DIMS = HEAD_DIM // 4
ROPE_HALF = ROPE_DIMS // 2
ROPE_THETA = 500000.0
NORM_EPS = 1e-5
DIFF_HEADS = 6
MEM_HEADS = 4
MEM_Q_WIDTH = MEM_HEADS * HEAD_DIM
DIL_GROUPS = ((128, 1), (512, 4), (2048, 16))
DIL_HEADS = 12
DIL_BLOCK = 128
DIL_GROUP_WIDTH = DIL_HEADS * HEAD_DIM
PEER_HEADS = 8
PEER_N_KEYS = 128
PEER_TOPK = 16
PEER_HALF = 128
PEER_SLOTS = PEER_HEADS * PEER_TOPK

LANES = 128
SUBLANES = 8
VMEM_LIMIT_BYTES = 56 * 1024 * 1024

_NT = (((1,), (1,)), ((), ()))


def _params(n_axes):
    return pltpu.CompilerParams(dimension_semantics=("arbitrary",) * n_axes,
                                vmem_limit_bytes=VMEM_LIMIT_BYTES)


def _resident(shape):
    zeros = (0,) * len(shape)
    return pl.BlockSpec(shape, lambda *_: zeros, pipeline_mode=pl.Buffered(1))


def _rms_scale(x, g):
    ms = jnp.mean(x * x, axis=-1, keepdims=True)
    return x * lax.rsqrt(ms + NORM_EPS) * g


def _rope_tables(seq):
    inv = ROPE_THETA ** (-jnp.arange(0, ROPE_DIMS, 2, dtype=F32) / ROPE_DIMS)
    ang = jnp.arange(seq, dtype=F32)[:, None] * inv[None, :]
    cos, sin = jnp.cos(ang), jnp.sin(ang)
    ones = jnp.ones((seq, HEAD_DIM - ROPE_DIMS), F32)
    zeros = jnp.zeros((seq, HEAD_DIM - ROPE_DIMS), F32)
    zh = jnp.zeros((seq, ROPE_HALF), F32)
    c = jnp.concatenate([cos, cos, ones], axis=1)
    sa = jnp.concatenate([-sin, zh, zeros], axis=1)
    sb = jnp.concatenate([zh, sin, zeros], axis=1)
    rep = LANES // HEAD_DIM
    return tuple(jnp.tile(t, (1, rep)) for t in (c, sa, sb))


def _proj_kernel(*refs, n_rope, chunk):
    if n_rope:
        x_ref, g_ref, w_ref, c_ref, sa_ref, sb_ref, o_ref = refs
    else:
        x_ref, g_ref, w_ref, o_ref = refs
    y = _rms_scale(x_ref[...], g_ref[...]).astype(BF16)
    n = o_ref.shape[1]
    for c0 in range(0, n, chunk):
        acc = jnp.dot(y, w_ref[:, c0:c0 + chunk], preferred_element_type=F32)
        if c0 < n_rope:
            for k0 in range(0, chunk, LANES):
                a = acc[:, k0:k0 + LANES]
                a = (a * c_ref[...]
                     + pltpu.roll(a, LANES - ROPE_HALF, 1) * sa_ref[...]
                     + pltpu.roll(a, ROPE_HALF, 1) * sb_ref[...])
                o_ref[:, c0 + k0:c0 + k0 + LANES] = a.astype(o_ref.dtype)
        else:
            o_ref[:, c0:c0 + chunk] = acc.astype(o_ref.dtype)


def _proj(x2d, g, w_bf16, *, tm, n_rope=0, rope=None, seq=None):
    t, d = x2d.shape
    n = w_bf16.shape[1]
    chunk = 256
    assert t % tm == 0 and n % chunk == 0 and n_rope % chunk == 0
    in_specs = [pl.BlockSpec((tm, d), lambda i: (i, 0)),
                _resident((1, d)),
                _resident((d, n))]
    args = [x2d, g.reshape(1, d), w_bf16]
    if n_rope:
        nblk = seq // tm
        spec = pl.BlockSpec((tm, LANES), lambda i: (i % nblk, 0))
        in_specs += [spec, spec, spec]
        args += list(rope)
    return pl.pallas_call(
        functools.partial(_proj_kernel, n_rope=n_rope, chunk=chunk),
        grid=(t // tm,),
        in_specs=in_specs,
        out_specs=pl.BlockSpec((tm, n), lambda i: (i, 0)),
        out_shape=jax.ShapeDtypeStruct((t, n), BF16),
        compiler_params=_params(1),
        name="norm_proj",
    )(*args)


def _online_softmax_step(q, k, v, carry, mask):
    m, l, acc = carry
    s = lax.dot_general(q, k, _NT, preferred_element_type=F32)
    if mask is not None:
        s = jnp.where(mask, s, -jnp.inf)
    m_new = jnp.maximum(m, jnp.max(s, axis=-1, keepdims=True))
    alpha = jnp.exp(m - m_new)
    p = jnp.exp(s - m_new)
    l = alpha * l + jnp.sum(p, axis=-1, keepdims=True)
    acc = alpha * acc + jnp.dot(p.astype(BF16), v, preferred_element_type=F32)
    return m_new, l, acc


def _diff_attn_kernel(q_ref, k_ref, v_ref, lp_ref, g_ref, o_ref, *, tq, tk, lambda_init):
    qi = pl.program_id(2)
    q = q_ref[...]
    lane = lax.broadcasted_iota(jnp.int32, q.shape, 1)
    zero = jnp.zeros_like(q)
    qa = jnp.where(lane < HEAD_DIM, q, zero)
    qb = jnp.where(lane >= HEAD_DIM, q, zero)

    def init():
        return (jnp.full((tq, 1), -jnp.inf, F32), jnp.zeros((tq, 1), F32),
                jnp.zeros((tq, 2 * HEAD_DIM), F32))

    def body(j, carry):
        c1, c2 = carry
        k = k_ref[pl.ds(pl.multiple_of(j * tk, tk), tk), :]
        v = v_ref[pl.ds(pl.multiple_of(j * tk, tk), tk), :]
        return (_online_softmax_step(qa, k, v, c1, None),
                _online_softmax_step(qb, k, v, c2, None))

    n_full = (qi * tq) // tk
    c1, c2 = lax.fori_loop(0, n_full, body, (init(), init()))
    row = qi * tq + lax.broadcasted_iota(jnp.int32, (tq, tk), 0)
    col = n_full * tk + lax.broadcasted_iota(jnp.int32, (tq, tk), 1)
    causal = col <= row
    start = pl.multiple_of(n_full * tk, tk)
    k = k_ref[pl.ds(start, tk), :]
    v = v_ref[pl.ds(start, tk), :]
    m1, l1, a1 = _online_softmax_step(qa, k, v, c1, causal)
    m2, l2, a2 = _online_softmax_step(qb, k, v, c2, causal)

    lp = lp_ref[...]
    lam = (jnp.exp(jnp.sum(lp[0:1] * lp[1:2], axis=-1, keepdims=True))
           - jnp.exp(jnp.sum(lp[2:3] * lp[3:4], axis=-1, keepdims=True)) + lambda_init)
    o = a1 / l1 - lam * (a2 / l2)
    o = _rms_scale(o, g_ref[...]) * (1.0 - lambda_init)
    o_ref[...] = o.astype(o_ref.dtype)


def _diff_attention(proj, lp, subln, *, batch, seq, lambda_init, tq, tk):
    assert tk % tq == 0 and seq % tk == 0
    nq = seq // tq
    width = 2 * HEAD_DIM
    return pl.pallas_call(
        functools.partial(_diff_attn_kernel, tq=tq, tk=tk, lambda_init=lambda_init),
        grid=(batch, DIFF_HEADS, nq),
        in_specs=[
            pl.BlockSpec((tq, width), lambda b, h, i: (b * nq + i, h)),
            pl.BlockSpec((seq, width), lambda b, h, i: (b, DIFF_HEADS + h)),
            pl.BlockSpec((seq, width), lambda b, h, i: (b, 2 * DIFF_HEADS + h)),
            _resident((4, HEAD_DIM)),
            _resident((1, width)),
        ],
        out_specs=pl.BlockSpec((tq, width), lambda b, h, i: (b * nq + i, h)),
        out_shape=jax.ShapeDtypeStruct((batch * seq, DIFF_HEADS * width), BF16),
        compiler_params=_params(3),
        name="diff_attention",
    )(proj, proj, proj, lp, subln.reshape(1, width))


def _dilated_kernel(q_ref, kc_ref, kp_ref, vc_ref, vp_ref, o_ref, st_ref, *, tq, n_steps):
    n = pl.program_id(2)
    hp = pl.program_id(3)
    blk = DIL_BLOCK

    @pl.when(hp == 0)
    def _():
        st_ref[...] = jnp.zeros_like(st_ref)

    qrow = lax.broadcasted_iota(jnp.int32, (blk, 2 * blk), 0)
    kcol = lax.broadcasted_iota(jnp.int32, (blk, 2 * blk), 1)
    dist = qrow + blk - kcol
    band = (dist >= 0) & (dist <= n_steps)
    lane_q = lax.broadcasted_iota(jnp.int32, (blk, LANES), 1)
    lane_s = lax.broadcasted_iota(jnp.int32, (blk, LANES), 1)

    for i in range(tq // blk):
        rows = slice(i * blk, (i + 1) * blk)
        q = q_ref[rows, :]
        if i == 0:
            k_prev, v_prev = kp_ref[...], vp_ref[...]
            valid = band & ((n > 0) | (kcol >= blk))
        else:
            prev = slice((i - 1) * blk, i * blk)
            k_prev, v_prev = kc_ref[prev, :], vc_ref[prev, :]
            valid = band
        keys = jnp.concatenate([k_prev, kc_ref[rows, :]], axis=0)
        vals = jnp.concatenate([v_prev, vc_ref[rows, :]], axis=0)
        out = jnp.zeros((blk, LANES), F32)
        stats = st_ref[rows, :]
        for e in range(LANES // HEAD_DIM):
            head_lanes = (lane_q // HEAD_DIM) == e
            qe = jnp.where(head_lanes, q, jnp.zeros_like(q))
            s = lax.dot_general(qe, keys, _NT, preferred_element_type=F32)
            s = jnp.where(valid, s, -jnp.inf)
            m = jnp.max(s, axis=-1, keepdims=True)
            p = jnp.exp(s - m)
            den = jnp.sum(p, axis=-1, keepdims=True)
            oe = jnp.dot((p / den).astype(BF16), vals, preferred_element_type=F32)
            out = jnp.where(head_lanes, oe, out)
            lse = m + jnp.log(den)
            stats = jnp.where(lane_s == hp * (LANES // HEAD_DIM) + e, lse, stats)
        o_ref[rows, :] = out.astype(o_ref.dtype)
        st_ref[rows, :] = stats


def _dilated_group(q4, k4, v4, *, q_off, k_off, v_off, n_steps):
    b, d, length, _ = q4.shape
    tq = min(512, length)
    assert length % tq == 0 and tq % DIL_BLOCK == 0
    sub = tq // DIL_BLOCK
    pairs = DIL_GROUP_WIDTH // LANES

    def cur(off):
        return pl.BlockSpec((None, None, tq, LANES), lambda bi, r, n, h: (bi, r, n, off + h))

    def prev(off):
        return pl.BlockSpec((None, None, DIL_BLOCK, LANES),
                            lambda bi, r, n, h: (bi, r, jnp.maximum(n * sub - 1, 0), off + h))

    return pl.pallas_call(
        functools.partial(_dilated_kernel, tq=tq, n_steps=n_steps),
        grid=(b, d, length // tq, pairs),
        in_specs=[cur(q_off), cur(k_off), prev(k_off), cur(v_off), prev(v_off)],
        out_specs=[pl.BlockSpec((None, None, tq, LANES), lambda bi, r, n, h: (bi, r, n, h)),
                   pl.BlockSpec((None, None, tq, LANES), lambda bi, r, n, h: (bi, r, n, 0))],
        out_shape=[jax.ShapeDtypeStruct((b, d, length, DIL_GROUP_WIDTH), BF16),
                   jax.ShapeDtypeStruct((b, d, length, LANES), F32)],
        compiler_params=_params(4),
        name="dilated_attention",
    )(q4, k4, k4, v4, v4)


def _memory_attention(q, k, v):
    lane = lax.broadcasted_iota(jnp.int32, q.shape, 1)
    out = jnp.zeros(q.shape, F32)
    for h in range(MEM_HEADS):
        head_lanes = (lane // HEAD_DIM) == h
        qh = jnp.where(head_lanes, q, jnp.zeros_like(q))
        s = lax.dot_general(qh, k, _NT, preferred_element_type=F32)
        m = jnp.max(s, axis=-1, keepdims=True)
        p = jnp.exp(s - m)
        den = jnp.sum(p, axis=-1, keepdims=True)
        oh = jnp.dot((p / den).astype(BF16), v, preferred_element_type=F32)
        out = jnp.where(head_lanes, oh, out)
    return out


def _split_bf16(x):
    hi = x.astype(BF16)
    lo = (x - hi.astype(F32)).astype(BF16)
    return hi, lo


def _outproj_kernel(*refs, n_groups):
    if n_groups:
        x_ref, qm_ref, mk_ref, mv_ref, w_ref = refs[:5]
        o_refs = refs[5:5 + n_groups]
        st_refs = refs[5 + n_groups:5 + 2 * n_groups]
        ex_ref, out_ref = refs[5 + 2 * n_groups:]
        lses = [r[...] for r in st_refs]
        top = functools.reduce(jnp.maximum, lses)
        es = [jnp.exp(l - top) for l in lses]
        den = functools.reduce(lambda a, b: a + b, es)
        mix = None
        for e, o_ref in zip(es, o_refs):
            hi, lo = _split_bf16(e / den)
            wide = (jnp.dot(hi, ex_ref[...], preferred_element_type=F32)
                    + jnp.dot(lo, ex_ref[...], preferred_element_type=F32))
            term = wide * o_ref[...].astype(F32)
            mix = term if mix is None else mix + term
        mix = mix.astype(BF16)
    else:
        x_ref, qm_ref, mk_ref, mv_ref, w_ref, mix_ref, out_ref = refs
        mix = mix_ref[...]
    mo = _memory_attention(qm_ref[...], mk_ref[...], mv_ref[...]).astype(BF16)
    k_mix = mix.shape[1]
    acc = jnp.dot(mix, w_ref[:k_mix, :], preferred_element_type=F32)
    acc += jnp.dot(mo, w_ref[k_mix:, :], preferred_element_type=F32)
    out_ref[...] = x_ref[...] + acc


def _outproj(x2d, proj, memkv, w_bf16, *, seq, mem_tokens, tm, mix=None, group_out=None,
             group_stats=None, expand=None):
    t, d = x2d.shape
    per_batch = seq // tm
    qm_block = (proj.shape[1] - MEM_Q_WIDTH) // MEM_Q_WIDTH
    row = lambda i: (i, 0)
    in_specs = [pl.BlockSpec((tm, d), row),
                pl.BlockSpec((tm, MEM_Q_WIDTH), lambda i: (i, qm_block)),
                pl.BlockSpec((mem_tokens, MEM_Q_WIDTH), lambda i: (i // per_batch, 0)),
                pl.BlockSpec((mem_tokens, MEM_Q_WIDTH), lambda i: (i // per_batch, 1)),
                _resident(w_bf16.shape)]
    args = [x2d, proj, memkv, memkv, w_bf16]
    if mix is not None:
        n_groups = 0
        in_specs.append(pl.BlockSpec((tm, mix.shape[1]), row))
        args.append(mix)
    else:
        n_groups = len(group_out)
        in_specs += [pl.BlockSpec((tm, DIL_GROUP_WIDTH), row)] * n_groups
        in_specs += [pl.BlockSpec((tm, LANES), row)] * n_groups
        in_specs.append(_resident(expand.shape))
        args += list(group_out) + list(group_stats) + [expand]
    return pl.pallas_call(
        functools.partial(_outproj_kernel, n_groups=n_groups),
        grid=(t // tm,),
        in_specs=in_specs,
        out_specs=pl.BlockSpec((tm, d), row),
        out_shape=jax.ShapeDtypeStruct((t, d), F32),
        compiler_params=_params(1),
        name="mix_outproj",
    )(*args)


def _topk_rows(s, k):
    n = s.shape[0]
    row = lax.broadcasted_iota(jnp.int32, s.shape, 0)
    vals, idxs = [], []
    for _ in range(k):
        m = jnp.max(s, axis=0, keepdims=True)
        i = jnp.min(jnp.where(s == m, row, n), axis=0, keepdims=True)
        vals.append(m)
        idxs.append(i)
        s = jnp.where(row == i, -jnp.inf, s)
    return jnp.concatenate(vals, axis=0), jnp.concatenate(idxs, axis=0)


def _select_rows(table, idx):
    row = lax.broadcasted_iota(jnp.int32, table.shape, 0)
    return jnp.sum(jnp.where(row == idx, table, jnp.zeros_like(table)), axis=0, keepdims=True)


def _peer_route_kernel(x_ref, g_ref, wq_ref, keys_ref, xn_ref, pair_ref, parity_ref, gate_ref):
    xn = _rms_scale(x_ref[...], g_ref[...])
    xn_ref[...] = xn
    q = jnp.dot(xn.astype(BF16), wq_ref[...], preferred_element_type=F32).astype(BF16)
    kk = PEER_TOPK
    for h in range(PEER_HEADS):
        tops = []
        for half in range(2):
            c0 = (2 * h + half) * PEER_HALF
            s = lax.dot_general(keys_ref[2 * h + half], q[:, c0:c0 + PEER_HALF], _NT,
                                preferred_element_type=F32)
            tops.append(_topk_rows(s, kk))
        (s1, i1), (s2, i2) = tops
        half = kk // 2
        sub = lax.broadcasted_iota(jnp.int32, (half, s1.shape[1]), 0)
        blocks = [s1[0:1] + s2]
        experts = [i1[0:1] * PEER_N_KEYS + i2]
        for a in range(1, half):
            blocks.append(jnp.where(sub < kk // (a + 1), s1[a:a + 1] + s2[:half], -jnp.inf))
            experts.append(i1[a:a + 1] * PEER_N_KEYS + i2[:half])
        blocks.append(s1[half:] + s2[0:1])
        experts.append(i1[half:] * PEER_N_KEYS + i2[0:1])
        cand = jnp.concatenate(blocks, axis=0)
        cand_expert = jnp.concatenate(experts, axis=0)
        top_s, top_j = _topk_rows(cand, kk)
        expert = jnp.concatenate([_select_rows(cand_expert, top_j[r:r + 1]) for r in range(kk)],
                                 axis=0)
        e = jnp.exp(top_s - top_s[0:1])
        gate = e / jnp.sum(e, axis=0, keepdims=True)
        rows = slice(h * kk, (h + 1) * kk)
        pair_ref[rows, :] = expert >> 1
        parity_ref[rows, :] = expert & 1
        gate_ref[rows, :] = gate


def _peer_route(x2d, g, wq_bf16, keys_bf16, *, tm):
    t, d = x2d.shape
    col = lambda i: (0, i)
    slot_shape = jax.ShapeDtypeStruct((PEER_SLOTS, t), jnp.int32)
    return pl.pallas_call(
        _peer_route_kernel,
        grid=(t // tm,),
        in_specs=[pl.BlockSpec((tm, d), lambda i: (i, 0)),
                  _resident((1, d)),
                  _resident(wq_bf16.shape),
                  _resident(keys_bf16.shape)],
        out_specs=[pl.BlockSpec((tm, d), lambda i: (i, 0)),
                   pl.BlockSpec((PEER_SLOTS, tm), col),
                   pl.BlockSpec((PEER_SLOTS, tm), col),
                   pl.BlockSpec((PEER_SLOTS, tm), col)],
        out_shape=[jax.ShapeDtypeStruct((t, d), F32), slot_shape, slot_shape,
                   jax.ShapeDtypeStruct((PEER_SLOTS, t), F32)],
        compiler_params=_params(1),
        name="peer_route",
    )(x2d, g.reshape(1, d), wq_bf16, keys_bf16)


_HI_MASK = 0xFFFF0000


def _pack_expert_table(u):
    e, d = u.shape
    bits = lax.bitcast_convert_type(u.astype(BF16), jnp.uint16).astype(jnp.uint32)
    bits = bits.reshape(e // 2, 2, d)
    packed = (bits[:, 0] << 16) | bits[:, 1]
    return packed.reshape(e // 2, d // LANES, LANES)


_BIT_REVERSED = (0, 4, 2, 6, 1, 5, 3, 7)


def _bf16_pair_words(x):
    bits = pltpu.bitcast(x, jnp.uint32)
    top = (bits + jnp.uint32(0x7FFF) + ((bits >> 16) & jnp.uint32(1))) >> 16
    return (top << 16) | top


def _packed_add(a, b):
    return pltpu.bitcast(pltpu.bitcast(a, BF16) + pltpu.bitcast(b, BF16), jnp.uint32)


def _sublane_sums(words, sub):
    level = [words[i] for i in _BIT_REVERSED]
    for k in (4, 2, 1):
        low = (sub & k) == 0
        merged = []
        for a, b in zip(level[0::2], level[1::2]):
            if k == 4:
                merged.append(_packed_add(jnp.where(low, a, b),
                                          pltpu.roll(jnp.where(low, b, a), k, 0)))
            else:
                merged.append(jnp.where(low,
                                        _packed_add(a, pltpu.roll(a, SUBLANES - k, 0)),
                                        _packed_add(b, pltpu.roll(b, k, 0))))
        level = merged
    return level[0]


def _peer_up_kernel(pair_ref, x_ref, tab_ref, gate_ref, par_ref, spread_ref, wexp_ref, w2_ref):
    tm = x_ref.shape[0]
    sub = lax.broadcasted_iota(jnp.int32, (SUBLANES, LANES), 0)
    hi_mask = jnp.uint32(_HI_MASK)

    def token_sums(t, m):
        xx = pltpu.bitcast(_bf16_pair_words(x_ref[t]), BF16)
        even, odd = [], []
        for g in range(PEER_SLOTS // SUBLANES):
            words = []
            for i in range(SUBLANES):
                slab = pltpu.bitcast(tab_ref[pair_ref[t, g * SUBLANES + i]], BF16)
                words.append(pltpu.bitcast(slab * xx, jnp.uint32))
            q = _sublane_sums(words, sub)
            even.append(pltpu.bitcast(q & hi_mask, F32))
            odd.append(pltpu.bitcast(q << 16, F32))
        lane_parts = jnp.concatenate(even + odd, axis=0).astype(BF16)
        onehot = (sub == m).astype(BF16)
        return lax.dot_general(onehot, lane_parts, _NT, preferred_element_type=F32)

    def group(gi, _):
        base = pl.multiple_of(gi * SUBLANES, SUBLANES)
        h2 = token_sums(base, 0)
        for m in range(1, SUBLANES):
            h2 = h2 + token_sums(base + m, m)
        odd_slot = par_ref[pl.ds(base, SUBLANES), :] != 0
        h = jnp.where(odd_slot, h2[:, PEER_SLOTS:], h2[:, :PEER_SLOTS])
        act = 0.5 * h * (1.0 + lax.erf(h * (1.0 / math.sqrt(2.0))))
        w = gate_ref[pl.ds(base, SUBLANES), :] * act
        zero = jnp.zeros_like(w)
        w2_ref[pl.ds(base, SUBLANES), 0:PEER_SLOTS] = jnp.where(odd_slot, zero, w)
        w2_ref[pl.ds(base, SUBLANES), PEER_SLOTS:2 * PEER_SLOTS] = jnp.where(odd_slot, w, zero)
        return 0

    lax.fori_loop(0, tm // SUBLANES, group, 0)
    hi, lo = _split_bf16(w2_ref[...])
    wexp_ref[...] = (jnp.dot(hi, spread_ref[...], preferred_element_type=F32)
                     + jnp.dot(lo, spread_ref[...], preferred_element_type=F32))


PAIR_ROWS = 2 * SUBLANES
DOWN_K = PEER_SLOTS * PAIR_ROWS
DOWN_PAIRS_PER_STEP = 4


def _peer_down_kernel(pair_ref, wexp_ref, diag_ref, x_ref, tab_ref, o_ref):
    tm = x_ref.shape[0]

    def left_rows(t):
        row = wexp_ref[pl.ds(t, 1), :]
        hi, lo = _split_bf16(row * diag_ref[...])
        return [hi, lo]

    def token_pair(ta):
        tb = ta + 1
        rhs = jnp.concatenate(
            [jnp.concatenate([tab_ref[pair_ref[ta, j]], tab_ref[pair_ref[tb, j]]], axis=1)
             for j in range(PEER_SLOTS)], axis=0)
        lhs = jnp.concatenate(left_rows(ta) + left_rows(tb), axis=0)
        out = jnp.dot(lhs, rhs, preferred_element_type=F32)
        s = SUBLANES
        o_ref[ta] = x_ref[ta] + (out[0:s, :LANES] + out[s:2 * s, :LANES])
        o_ref[tb] = x_ref[tb] + (out[2 * s:3 * s, LANES:] + out[3 * s:4 * s, LANES:])

    def step(i, _):
        for k in range(DOWN_PAIRS_PER_STEP):
            token_pair(2 * (DOWN_PAIRS_PER_STEP * i + k))
        return 0

    lax.fori_loop(0, tm // (2 * DOWN_PAIRS_PER_STEP), step, 0)


def _smem_rows(tm):
    return pl.BlockSpec((tm, PEER_SLOTS), lambda i: (i, 0), memory_space=pltpu.SMEM)


def _peer_up(pair, parity, xn3, table, gate, spread, *, tm):
    t = xn3.shape[0]
    rows = pl.BlockSpec((tm, PEER_SLOTS), lambda i: (i, 0))
    return pl.pallas_call(
        _peer_up_kernel,
        grid=(t // tm,),
        in_specs=[_smem_rows(tm),
                  pl.BlockSpec((tm, SUBLANES, LANES), lambda i: (i, 0, 0)),
                  _resident(table.shape), rows, rows, _resident(spread.shape)],
        out_specs=pl.BlockSpec((tm, DOWN_K), lambda i: (i, 0)),
        out_shape=jax.ShapeDtypeStruct((t, DOWN_K), F32),
        scratch_shapes=[pltpu.VMEM((tm, 2 * PEER_SLOTS), F32)],
        compiler_params=_params(1),
        name="peer_up",
    )(pair, xn3, table, gate, parity, spread)


def _peer_down(pair, wexp, diag, x3, table, *, tm):
    t = x3.shape[0]
    tile = pl.BlockSpec((tm, SUBLANES, LANES), lambda i: (i, 0, 0))
    return pl.pallas_call(
        _peer_down_kernel,
        grid=(t // tm,),
        in_specs=[_smem_rows(tm), pl.BlockSpec((tm, DOWN_K), lambda i: (i, 0)),
                  _resident(diag.shape), tile, _resident(table.shape)],
        out_specs=tile,
        out_shape=jax.ShapeDtypeStruct(x3.shape, F32),
        compiler_params=_params(1),
        name="peer_down",
    )(pair, wexp, diag, x3, table)


def _peer_constants():
    col = jnp.arange(DOWN_K)
    src = ((col // SUBLANES) % 2) * PEER_SLOTS + col // PAIR_ROWS
    spread = (jnp.arange(2 * PEER_SLOTS)[:, None] == src[None, :]).astype(BF16)
    diag = (col[None, :] % SUBLANES == jnp.arange(SUBLANES)[:, None]).astype(F32)
    return spread, diag


def _pair_slabs(v):
    e, d = v.shape
    return v.astype(BF16).reshape(e // 2, 2 * d // LANES, LANES)


def _peer(x2d, g, wq_bf16, keys_bf16, u_packed, v_slabs, consts, *, tm_route, tm_expert):
    t, d = x2d.shape
    spread, diag = consts
    xn, pair, parity, gate = _peer_route(x2d, g, wq_bf16, keys_bf16, tm=tm_route)
    pair, parity, gate = pair.T, parity.T, gate.T
    xn3 = xn.reshape(t, d // LANES, LANES)
    wexp = _peer_up(pair, parity, xn3, u_packed, gate, spread, tm=tm_expert)
    out3 = _peer_down(pair, wexp, diag, x2d.reshape(t, d // LANES, LANES), v_slabs, tm=tm_expert)
    return out3.reshape(t, d)


def _final_norm_kernel(x_ref, g_ref, o_ref):
    o_ref[...] = _rms_scale(x_ref[...], g_ref[...])


def _final_norm(x2d, g, *, tm):
    t, d = x2d.shape
    return pl.pallas_call(
        _final_norm_kernel,
        grid=(t // tm,),
        in_specs=[pl.BlockSpec((tm, d), lambda i: (i, 0)), _resident((1, d))],
        out_specs=pl.BlockSpec((tm, d), lambda i: (i, 0)),
        out_shape=jax.ShapeDtypeStruct((t, d), F32),
        compiler_params=_params(1),
        name="final_norm",
    )(x2d, g.reshape(1, d))


def _tiles(seq):
    return dict(tm_proj=min(256, seq), tm_out=min(512, seq), tq=min(512, seq), tk=min(512, seq),
                tm_route=LANES, tm_expert=LANES, tm_norm=min(512, seq))


def _to_residue_major(a2d, batch, seq, dilation, c0, c1):
    cols = a2d[:, c0:c1].reshape(batch, seq // dilation, dilation, c1 - c0)
    return cols.transpose(0, 2, 1, 3)


def _from_residue_major(a4):
    b, d, length, c = a4.shape
    return a4.transpose(0, 2, 1, 3).reshape(b * d * length, c)


def kernel(x, mem, norm_mix, a_w_in, a_lambda, a_subln, b_w_in, norm_mem, w_mem_kv, w_out,
           norm_ffn, peer_wq, peer_keys, peer_u, peer_v, shared_norm, shared_w_kv, final_norm):
    batch, seq, d = x.shape
    mem_tokens = mem.shape[1]
    depth = norm_mix.shape[0]
    n_a = a_w_in.shape[0]
    t = batch * seq
    tiles = _tiles(seq)
    rope = _rope_tables(seq)
    scale = HEAD_DIM ** -0.5

    diff_qk = DIFF_HEADS * 2 * HEAD_DIM
    dil_qk = len(DIL_GROUPS) * DIL_GROUP_WIDTH
    peer_consts = _peer_constants()
    expand = jnp.repeat(jnp.eye(LANES, DIL_HEADS, dtype=BF16), HEAD_DIM, axis=1)

    x2d = x.reshape(t, d)
    mem2d = mem.reshape(batch * mem_tokens, d)
    shared = None

    for l in range(depth):
        memkv = _proj(mem2d, norm_mem[l], w_mem_kv[l].astype(BF16), tm=mem_tokens)
        w_o = w_out[l].astype(BF16)
        if l < n_a:
            col_scale = jnp.concatenate([jnp.full((diff_qk,), scale, F32),
                                         jnp.ones((a_w_in.shape[2] - diff_qk - MEM_Q_WIDTH,), F32),
                                         jnp.full((MEM_Q_WIDTH,), scale, F32)])
            w_in = (a_w_in[l] * col_scale).astype(BF16)
            proj = _proj(x2d, norm_mix[l], w_in, tm=tiles["tm_proj"], n_rope=2 * diff_qk,
                         rope=rope, seq=seq)
            lambda_init = 0.8 - 0.6 * math.exp(-0.3 * l)
            mix = _diff_attention(proj, a_lambda[l], a_subln[l], batch=batch, seq=seq,
                                  lambda_init=lambda_init, tq=tiles["tq"], tk=tiles["tk"])
            x2d = _outproj(x2d, proj, memkv, w_o, seq=seq, mem_tokens=mem_tokens,
                           tm=tiles["tm_out"], mix=mix)
        else:
            w_in = (b_w_in[l - n_a] * scale).astype(BF16)
            proj = _proj(x2d, norm_mix[l], w_in, tm=tiles["tm_proj"], n_rope=dil_qk,
                         rope=rope, seq=seq)
            outs, stats = [], []
            for gi, (window, dilation) in enumerate(DIL_GROUPS):
                c0 = gi * DIL_GROUP_WIDTH
                blk0 = c0 // LANES
                if dilation == 1:
                    q4, q_off = proj.reshape(batch, 1, seq, proj.shape[1]), blk0
                else:
                    q4, q_off = _to_residue_major(proj, batch, seq, dilation,
                                                  c0, c0 + DIL_GROUP_WIDTH), 0
                k4, k_off, v4, v_off = shared[gi]
                o4, st4 = _dilated_group(q4, k4, v4, q_off=q_off, k_off=k_off, v_off=v_off,
                                         n_steps=window // dilation)
                outs.append(_from_residue_major(o4))
                stats.append(_from_residue_major(st4))
            x2d = _outproj(x2d, proj, memkv, w_o, seq=seq, mem_tokens=mem_tokens,
                           tm=tiles["tm_out"], group_out=outs, group_stats=stats, expand=expand)

        x2d = _peer(x2d, norm_ffn[l], peer_wq[l].astype(BF16),
                    peer_keys[l].reshape(2 * PEER_HEADS, PEER_N_KEYS, PEER_HALF).astype(BF16),
                    _pack_expert_table(peer_u[l]), _pair_slabs(peer_v[l]), peer_consts,
                    tm_route=tiles["tm_route"], tm_expert=tiles["tm_expert"])

        if l == n_a - 1:
            kv = _proj(x2d, shared_norm, shared_w_kv.astype(BF16), tm=tiles["tm_proj"],
                       n_rope=dil_qk, rope=rope, seq=seq)
            shared = []
            for gi, (window, dilation) in enumerate(DIL_GROUPS):
                c0 = gi * DIL_GROUP_WIDTH
                if dilation == 1:
                    kv4 = kv.reshape(batch, 1, seq, kv.shape[1])
                    shared.append((kv4, c0 // LANES, kv4, (dil_qk + c0) // LANES))
                else:
                    k4 = _to_residue_major(kv, batch, seq, dilation, c0, c0 + DIL_GROUP_WIDTH)
                    v4 = _to_residue_major(kv, batch, seq, dilation, dil_qk + c0,
                                           dil_qk + c0 + DIL_GROUP_WIDTH)
                    shared.append((k4, 0, v4, 0))

    return _final_norm(x2d, final_norm, tm=tiles["tm_norm"]).reshape(batch, seq, d)
```

```python
import functools
import math

import jax
import jax.numpy as jnp
from jax import lax
from jax.experimental import pallas as pl
from jax.experimental.pallas import tpu as pltpu
from jax.experimental.pallas import tpu_sc as plsc

F32 = jnp.float32
BF16 = jnp.bfloat16

HEAD_DIM = 64
ROPE_DIMS = HEAD_DIM // 4
ROPE_HALF = ROPE_DIMS // 2
ROPE_THETA = 500000.0
NORM_EPS = 1e-5
DIFF_HEADS = 6
MEM_HEADS = 4
MEM_Q_WIDTH = MEM_HEADS * HEAD_DIM
DIL_GROUPS = ((128, 1), (512, 4), (2048, 16))
DIL_HEADS = 12
DIL_BLOCK = 128
DIL_GROUP_WIDTH = DIL_HEADS * HEAD_DIM
PEER_HEADS = 8
PEER_N_KEYS = 128
PEER_TOPK = 16
PEER_HALF = 128
PEER_SLOTS = PEER_HEADS * PEER_TOPK

LANES = 128
SUBLANES = 8
VMEM_LIMIT_BYTES = 56 * 1024 * 1024

_NT = (((1,), (1,)), ((), ()))


def _params(n_axes):
    return pltpu.CompilerParams(dimension_semantics=("arbitrary",) * n_axes,
                                vmem_limit_bytes=VMEM_LIMIT_BYTES)


def _resident(shape):
    zeros = (0,) * len(shape)
    return pl.BlockSpec(shape, lambda *_: zeros, pipeline_mode=pl.Buffered(1))


def _rms_scale(x, g):
    ms = jnp.mean(x * x, axis=-1, keepdims=True)
    return x * lax.rsqrt(ms + NORM_EPS) * g


def _rope_tables(seq):
    inv = ROPE_THETA ** (-jnp.arange(0, ROPE_DIMS, 2, dtype=F32) / ROPE_DIMS)
    ang = jnp.arange(seq, dtype=F32)[:, None] * inv[None, :]
    cos, sin = jnp.cos(ang), jnp.sin(ang)
    ones = jnp.ones((seq, HEAD_DIM - ROPE_DIMS), F32)
    zeros = jnp.zeros((seq, HEAD_DIM - ROPE_DIMS), F32)
    zh = jnp.zeros((seq, ROPE_HALF), F32)
    c = jnp.concatenate([cos, cos, ones], axis=1)
    sa = jnp.concatenate([-sin, zh, zeros], axis=1)
    sb = jnp.concatenate([zh, sin, zeros], axis=1)
    rep = LANES // HEAD_DIM
    return tuple(jnp.tile(t, (1, rep)) for t in (c, sa, sb))


def _proj_kernel(*refs, n_rope, chunk):
    if n_rope:
        x_ref, g_ref, w_ref, c_ref, sa_ref, sb_ref, o_ref = refs
    else:
        x_ref, g_ref, w_ref, o_ref = refs
    y = _rms_scale(x_ref[...], g_ref[...]).astype(BF16)
    n = o_ref.shape[1]
    for c0 in range(0, n, chunk):
        acc = jnp.dot(y, w_ref[:, c0:c0 + chunk], preferred_element_type=F32)
        if c0 < n_rope:
            for k0 in range(0, chunk, LANES):
                a = acc[:, k0:k0 + LANES]
                a = (a * c_ref[...]
                     + pltpu.roll(a, LANES - ROPE_HALF, 1) * sa_ref[...]
                     + pltpu.roll(a, ROPE_HALF, 1) * sb_ref[...])
                o_ref[:, c0 + k0:c0 + k0 + LANES] = a.astype(o_ref.dtype)
        else:
            o_ref[:, c0:c0 + chunk] = acc.astype(o_ref.dtype)


def _proj(x2d, g, w_bf16, *, tm, n_rope=0, rope=None, seq=None):
    t, d = x2d.shape
    n = w_bf16.shape[1]
    chunk = 256
    assert t % tm == 0 and n % chunk == 0 and n_rope % chunk == 0
    in_specs = [pl.BlockSpec((tm, d), lambda i: (i, 0)),
                _resident((1, d)),
                _resident((d, n))]
    args = [x2d, g.reshape(1, d), w_bf16]
    if n_rope:
        nblk = seq // tm
        spec = pl.BlockSpec((tm, LANES), lambda i: (i % nblk, 0))
        in_specs += [spec, spec, spec]
        args += list(rope)
    return pl.pallas_call(
        functools.partial(_proj_kernel, n_rope=n_rope, chunk=chunk),
        grid=(t // tm,),
        in_specs=in_specs,
        out_specs=pl.BlockSpec((tm, n), lambda i: (i, 0)),
        out_shape=jax.ShapeDtypeStruct((t, n), BF16),
        compiler_params=_params(1),
        name="norm_proj",
    )(*args)


def _online_softmax_step(q, k, v, carry, mask):
    m, l, acc = carry
    s = lax.dot_general(q, k, _NT, preferred_element_type=F32)
    if mask is not None:
        s = jnp.where(mask, s, -jnp.inf)
    m_new = jnp.maximum(m, jnp.max(s, axis=-1, keepdims=True))
    alpha = jnp.exp(m - m_new)
    p = jnp.exp(s - m_new)
    l = alpha * l + jnp.sum(p, axis=-1, keepdims=True)
    acc = alpha * acc + jnp.dot(p.astype(BF16), v, preferred_element_type=F32)
    return m_new, l, acc


def _diff_attn_kernel(q_ref, k_ref, v_ref, lp_ref, g_ref, o_ref, *, tq, tk, lambda_init):
    qi = pl.program_id(2)
    q = q_ref[...]
    lane = lax.broadcasted_iota(jnp.int32, q.shape, 1)
    zero = jnp.zeros_like(q)
    qa = jnp.where(lane < HEAD_DIM, q, zero)
    qb = jnp.where(lane >= HEAD_DIM, q, zero)

    def init():
        return (jnp.full((tq, 1), -jnp.inf, F32), jnp.zeros((tq, 1), F32),
                jnp.zeros((tq, 2 * HEAD_DIM), F32))

    def body(j, carry):
        c1, c2 = carry
        k = k_ref[pl.ds(pl.multiple_of(j * tk, tk), tk), :]
        v = v_ref[pl.ds(pl.multiple_of(j * tk, tk), tk), :]
        return (_online_softmax_step(qa, k, v, c1, None),
                _online_softmax_step(qb, k, v, c2, None))

    n_full = (qi * tq) // tk
    c1, c2 = lax.fori_loop(0, n_full, body, (init(), init()))
    row = qi * tq + lax.broadcasted_iota(jnp.int32, (tq, tk), 0)
    col = n_full * tk + lax.broadcasted_iota(jnp.int32, (tq, tk), 1)
    causal = col <= row
    start = pl.multiple_of(n_full * tk, tk)
    k = k_ref[pl.ds(start, tk), :]
    v = v_ref[pl.ds(start, tk), :]
    m1, l1, a1 = _online_softmax_step(qa, k, v, c1, causal)
    m2, l2, a2 = _online_softmax_step(qb, k, v, c2, causal)

    lp = lp_ref[...]
    lam = (jnp.exp(jnp.sum(lp[0:1] * lp[1:2], axis=-1, keepdims=True))
           - jnp.exp(jnp.sum(lp[2:3] * lp[3:4], axis=-1, keepdims=True)) + lambda_init)
    o = a1 / l1 - lam * (a2 / l2)
    o = _rms_scale(o, g_ref[...]) * (1.0 - lambda_init)
    o_ref[...] = o.astype(o_ref.dtype)


def _diff_attention(proj, lp, subln, *, batch, seq, lambda_init, tq, tk):
    assert tk % tq == 0 and seq % tk == 0
    nq = seq // tq
    width = 2 * HEAD_DIM
    return pl.pallas_call(
        functools.partial(_diff_attn_kernel, tq=tq, tk=tk, lambda_init=lambda_init),
        grid=(batch, DIFF_HEADS, nq),
        in_specs=[
            pl.BlockSpec((tq, width), lambda b, h, i: (b * nq + i, h)),
            pl.BlockSpec((seq, width), lambda b, h, i: (b, DIFF_HEADS + h)),
            pl.BlockSpec((seq, width), lambda b, h, i: (b, 2 * DIFF_HEADS + h)),
            _resident((4, HEAD_DIM)),
            _resident((1, width)),
        ],
        out_specs=pl.BlockSpec((tq, width), lambda b, h, i: (b * nq + i, h)),
        out_shape=jax.ShapeDtypeStruct((batch * seq, DIFF_HEADS * width), BF16),
        compiler_params=_params(3),
        name="diff_attention",
    )(proj, proj, proj, lp, subln.reshape(1, width))


def _dilated_kernel(q_ref, kc_ref, kp_ref, vc_ref, vp_ref, o_ref, st_ref, *, tq, n_steps):
    n = pl.program_id(2)
    hp = pl.program_id(3)
    blk = DIL_BLOCK

    @pl.when(hp == 0)
    def _():
        st_ref[...] = jnp.zeros_like(st_ref)

    qrow = lax.broadcasted_iota(jnp.int32, (blk, 2 * blk), 0)
    kcol = lax.broadcasted_iota(jnp.int32, (blk, 2 * blk), 1)
    dist = qrow + blk - kcol
    band = (dist >= 0) & (dist <= n_steps)
    lane_q = lax.broadcasted_iota(jnp.int32, (blk, LANES), 1)
    lane_s = lax.broadcasted_iota(jnp.int32, (blk, LANES), 1)

    for i in range(tq // blk):
        rows = slice(i * blk, (i + 1) * blk)
        q = q_ref[rows, :]
        if i == 0:
            k_prev, v_prev = kp_ref[...], vp_ref[...]
            valid = band & ((n > 0) | (kcol >= blk))
        else:
            prev = slice((i - 1) * blk, i * blk)
            k_prev, v_prev = kc_ref[prev, :], vc_ref[prev, :]
            valid = band
        keys = jnp.concatenate([k_prev, kc_ref[rows, :]], axis=0)
        vals = jnp.concatenate([v_prev, vc_ref[rows, :]], axis=0)
        out = jnp.zeros((blk, LANES), F32)
        stats = st_ref[rows, :]
        for e in range(LANES // HEAD_DIM):
            head_lanes = (lane_q // HEAD_DIM) == e
            qe = jnp.where(head_lanes, q, jnp.zeros_like(q))
            s = lax.dot_general(qe, keys, _NT, preferred_element_type=F32)
            s = jnp.where(valid, s, -jnp.inf)
            m = jnp.max(s, axis=-1, keepdims=True)
            p = jnp.exp(s - m)
            den = jnp.sum(p, axis=-1, keepdims=True)
            oe = jnp.dot((p / den).astype(BF16), vals, preferred_element_type=F32)
            out = jnp.where(head_lanes, oe, out)
            lse = m + jnp.log(den)
            stats = jnp.where(lane_s == hp * (LANES // HEAD_DIM) + e, lse, stats)
        o_ref[rows, :] = out.astype(o_ref.dtype)
        st_ref[rows, :] = stats


def _dilated_group(q4, k4, v4, *, q_off, k_off, v_off, n_steps):
    b, d, length, _ = q4.shape
    tq = min(512, length)
    assert length % tq == 0 and tq % DIL_BLOCK == 0
    sub = tq // DIL_BLOCK
    pairs = DIL_GROUP_WIDTH // LANES

    def cur(off):
        return pl.BlockSpec((None, None, tq, LANES), lambda bi, r, n, h: (bi, r, n, off + h))

    def prev(off):
        return pl.BlockSpec((None, None, DIL_BLOCK, LANES),
                            lambda bi, r, n, h: (bi, r, jnp.maximum(n * sub - 1, 0), off + h))

    return pl.pallas_call(
        functools.partial(_dilated_kernel, tq=tq, n_steps=n_steps),
        grid=(b, d, length // tq, pairs),
        in_specs=[cur(q_off), cur(k_off), prev(k_off), cur(v_off), prev(v_off)],
        out_specs=[pl.BlockSpec((None, None, tq, LANES), lambda bi, r, n, h: (bi, r, n, h)),
                   pl.BlockSpec((None, None, tq, LANES), lambda bi, r, n, h: (bi, r, n, 0))],
        out_shape=[jax.ShapeDtypeStruct((b, d, length, DIL_GROUP_WIDTH), BF16),
                   jax.ShapeDtypeStruct((b, d, length, LANES), F32)],
        compiler_params=_params(4),
        name="dilated_attention",
    )(q4, k4, k4, v4, v4)


def _memory_attention(q, k, v):
    lane = lax.broadcasted_iota(jnp.int32, q.shape, 1)
    out = jnp.zeros(q.shape, F32)
    for h in range(MEM_HEADS):
        head_lanes = (lane // HEAD_DIM) == h
        qh = jnp.where(head_lanes, q, jnp.zeros_like(q))
        s = lax.dot_general(qh, k, _NT, preferred_element_type=F32)
        m = jnp.max(s, axis=-1, keepdims=True)
        p = jnp.exp(s - m)
        den = jnp.sum(p, axis=-1, keepdims=True)
        oh = jnp.dot((p / den).astype(BF16), v, preferred_element_type=F32)
        out = jnp.where(head_lanes, oh, out)
    return out


def _split_bf16(x):
    hi = x.astype(BF16)
    lo = (x - hi.astype(F32)).astype(BF16)
    return hi, lo


def _outproj_kernel(*refs, n_groups):
    if n_groups:
        x_ref, qm_ref, mk_ref, mv_ref, w_ref = refs[:5]
        o_refs = refs[5:5 + n_groups]
        st_refs = refs[5 + n_groups:5 + 2 * n_groups]
        ex_ref, out_ref = refs[5 + 2 * n_groups:]
        lses = [r[...] for r in st_refs]
        top = functools.reduce(jnp.maximum, lses)
        es = [jnp.exp(l - top) for l in lses]
        den = functools.reduce(lambda a, b: a + b, es)
        mix = None
        for e, o_ref in zip(es, o_refs):
            hi, lo = _split_bf16(e / den)
            wide = (jnp.dot(hi, ex_ref[...], preferred_element_type=F32)
                    + jnp.dot(lo, ex_ref[...], preferred_element_type=F32))
            term = wide * o_ref[...].astype(F32)
            mix = term if mix is None else mix + term
        mix = mix.astype(BF16)
    else:
        x_ref, qm_ref, mk_ref, mv_ref, w_ref, mix_ref, out_ref = refs
        mix = mix_ref[...]
    mo = _memory_attention(qm_ref[...], mk_ref[...], mv_ref[...]).astype(BF16)
    k_mix = mix.shape[1]
    acc = jnp.dot(mix, w_ref[:k_mix, :], preferred_element_type=F32)
    acc += jnp.dot(mo, w_ref[k_mix:, :], preferred_element_type=F32)
    out_ref[...] = x_ref[...] + acc


def _outproj(x2d, proj, memkv, w_bf16, *, seq, mem_tokens, tm, mix=None, group_out=None,
             group_stats=None, expand=None):
    t, d = x2d.shape
    per_batch = seq // tm
    qm_block = (proj.shape[1] - MEM_Q_WIDTH) // MEM_Q_WIDTH
    row = lambda i: (i, 0)
    in_specs = [pl.BlockSpec((tm, d), row),
                pl.BlockSpec((tm, MEM_Q_WIDTH), lambda i: (i, qm_block)),
                pl.BlockSpec((mem_tokens, MEM_Q_WIDTH), lambda i: (i // per_batch, 0)),
                pl.BlockSpec((mem_tokens, MEM_Q_WIDTH), lambda i: (i // per_batch, 1)),
                _resident(w_bf16.shape)]
    args = [x2d, proj, memkv, memkv, w_bf16]
    if mix is not None:
        n_groups = 0
        in_specs.append(pl.BlockSpec((tm, mix.shape[1]), row))
        args.append(mix)
    else:
        n_groups = len(group_out)
        in_specs += [pl.BlockSpec((tm, DIL_GROUP_WIDTH), row)] * n_groups
        in_specs += [pl.BlockSpec((tm, LANES), row)] * n_groups
        in_specs.append(_resident(expand.shape))
        args += list(group_out) + list(group_stats) + [expand]
    return pl.pallas_call(
        functools.partial(_outproj_kernel, n_groups=n_groups),
        grid=(t // tm,),
        in_specs=in_specs,
        out_specs=pl.BlockSpec((tm, d), row),
        out_shape=jax.ShapeDtypeStruct((t, d), F32),
        compiler_params=_params(1),
        name="mix_outproj",
    )(*args)


def _topk_rows(s, k):
    n = s.shape[0]
    row = lax.broadcasted_iota(jnp.int32, s.shape, 0)
    vals, idxs = [], []
    for _ in range(k):
        m = jnp.max(s, axis=0, keepdims=True)
        i = jnp.min(jnp.where(s == m, row, n), axis=0, keepdims=True)
        vals.append(m)
        idxs.append(i)
        s = jnp.where(row == i, -jnp.inf, s)
    return jnp.concatenate(vals, axis=0), jnp.concatenate(idxs, axis=0)


def _select_rows(table, idx):
    row = lax.broadcasted_iota(jnp.int32, table.shape, 0)
    return jnp.sum(jnp.where(row == idx, table, jnp.zeros_like(table)), axis=0, keepdims=True)


def _peer_route_kernel(x_ref, g_ref, wq_ref, keys_ref, xn_ref, pair_ref, parity_ref, gate_ref):
    xn = _rms_scale(x_ref[...], g_ref[...])
    xn_ref[...] = xn
    q = jnp.dot(xn.astype(BF16), wq_ref[...], preferred_element_type=F32).astype(BF16)
    kk = PEER_TOPK
    for h in range(PEER_HEADS):
        tops = []
        for half in range(2):
            c0 = (2 * h + half) * PEER_HALF
            s = lax.dot_general(keys_ref[2 * h + half], q[:, c0:c0 + PEER_HALF], _NT,
                                preferred_element_type=F32)
            tops.append(_topk_rows(s, kk))
        (s1, i1), (s2, i2) = tops
        half = kk // 2
        sub = lax.broadcasted_iota(jnp.int32, (half, s1.shape[1]), 0)
        blocks = [s1[0:1] + s2]
        experts = [i1[0:1] * PEER_N_KEYS + i2]
        for a in range(1, half):
            blocks.append(jnp.where(sub < kk // (a + 1), s1[a:a + 1] + s2[:half], -jnp.inf))
            experts.append(i1[a:a + 1] * PEER_N_KEYS + i2[:half])
        blocks.append(s1[half:] + s2[0:1])
        experts.append(i1[half:] * PEER_N_KEYS + i2[0:1])
        cand = jnp.concatenate(blocks, axis=0)
        cand_expert = jnp.concatenate(experts, axis=0)
        top_s, top_j = _topk_rows(cand, kk)
        expert = jnp.concatenate([_select_rows(cand_expert, top_j[r:r + 1]) for r in range(kk)],
                                 axis=0)
        e = jnp.exp(top_s - top_s[0:1])
        gate = e / jnp.sum(e, axis=0, keepdims=True)
        rows = slice(h * kk, (h + 1) * kk)
        pair_ref[rows, :] = expert >> 1
        parity_ref[rows, :] = expert & 1
        gate_ref[rows, :] = gate


def _peer_route(x2d, g, wq_bf16, keys_bf16, *, tm):
    t, d = x2d.shape
    col = lambda i: (0, i)
    slot_shape = jax.ShapeDtypeStruct((PEER_SLOTS, t), jnp.int32)
    return pl.pallas_call(
        _peer_route_kernel,
        grid=(t // tm,),
        in_specs=[pl.BlockSpec((tm, d), lambda i: (i, 0)),
                  _resident((1, d)),
                  _resident(wq_bf16.shape),
                  _resident(keys_bf16.shape)],
        out_specs=[pl.BlockSpec((tm, d), lambda i: (i, 0)),
                   pl.BlockSpec((PEER_SLOTS, tm), col),
                   pl.BlockSpec((PEER_SLOTS, tm), col),
                   pl.BlockSpec((PEER_SLOTS, tm), col)],
        out_shape=[jax.ShapeDtypeStruct((t, d), F32), slot_shape, slot_shape,
                   jax.ShapeDtypeStruct((PEER_SLOTS, t), F32)],
        compiler_params=_params(1),
        name="peer_route",
    )(x2d, g.reshape(1, d), wq_bf16, keys_bf16)


_HI_MASK = 0xFFFF0000


def _pack_expert_table(u):
    e, d = u.shape
    bits = lax.bitcast_convert_type(u.astype(BF16), jnp.uint16).astype(jnp.uint32)
    bits = bits.reshape(e // 2, 2, d)
    packed = (bits[:, 0] << 16) | bits[:, 1]
    return packed.reshape(e // 2, d // LANES, LANES)


_BIT_REVERSED = (0, 4, 2, 6, 1, 5, 3, 7)


def _bf16_pair_words(x):
    bits = pltpu.bitcast(x, jnp.uint32)
    top = (bits + jnp.uint32(0x7FFF) + ((bits >> 16) & jnp.uint32(1))) >> 16
    return (top << 16) | top


def _packed_add(a, b):
    return pltpu.bitcast(pltpu.bitcast(a, BF16) + pltpu.bitcast(b, BF16), jnp.uint32)


def _sublane_sums(words, sub):
    level = [words[i] for i in _BIT_REVERSED]
    for k in (4, 2, 1):
        low = (sub & k) == 0
        merged = []
        for a, b in zip(level[0::2], level[1::2]):
            if k == 4:
                merged.append(_packed_add(jnp.where(low, a, b),
                                          pltpu.roll(jnp.where(low, b, a), k, 0)))
            else:
                merged.append(jnp.where(low,
                                        _packed_add(a, pltpu.roll(a, SUBLANES - k, 0)),
                                        _packed_add(b, pltpu.roll(b, k, 0))))
        level = merged
    return level[0]


def _peer_up_kernel(pair_ref, x_ref, tab_ref, gate_ref, par_ref, spread_ref, wexp_ref, w_ref,
                    w2_ref):
    tm = x_ref.shape[0]
    sub = lax.broadcasted_iota(jnp.int32, (SUBLANES, LANES), 0)
    hi_mask = jnp.uint32(_HI_MASK)

    def token_sums(t, m):
        xx = pltpu.bitcast(_bf16_pair_words(x_ref[t]), BF16)
        even, odd = [], []
        for g in range(PEER_SLOTS // SUBLANES):
            words = []
            for i in range(SUBLANES):
                slab = pltpu.bitcast(tab_ref[pair_ref[t, g * SUBLANES + i]], BF16)
                words.append(pltpu.bitcast(slab * xx, jnp.uint32))
            q = _sublane_sums(words, sub)
            even.append(pltpu.bitcast(q & hi_mask, F32))
            odd.append(pltpu.bitcast(q << 16, F32))
        lane_parts = jnp.concatenate(even + odd, axis=0).astype(BF16)
        onehot = (sub == m).astype(BF16)
        return lax.dot_general(onehot, lane_parts, _NT, preferred_element_type=F32)

    def group(gi, _):
        base = pl.multiple_of(gi * SUBLANES, SUBLANES)
        h2 = token_sums(base, 0)
        for m in range(1, SUBLANES):
            h2 = h2 + token_sums(base + m, m)
        odd_slot = par_ref[pl.ds(base, SUBLANES), :] != 0
        h = jnp.where(odd_slot, h2[:, PEER_SLOTS:], h2[:, :PEER_SLOTS])
        act = 0.5 * h * (1.0 + lax.erf(h * (1.0 / math.sqrt(2.0))))
        w = gate_ref[pl.ds(base, SUBLANES), :] * act
        zero = jnp.zeros_like(w)
        w_ref[pl.ds(base, SUBLANES), :] = w
        w2_ref[pl.ds(base, SUBLANES), 0:PEER_SLOTS] = jnp.where(odd_slot, zero, w)
        w2_ref[pl.ds(base, SUBLANES), PEER_SLOTS:2 * PEER_SLOTS] = jnp.where(odd_slot, w, zero)
        return 0

    lax.fori_loop(0, tm // SUBLANES, group, 0)
    hi, lo = _split_bf16(w2_ref[...])
    wexp_ref[...] = (jnp.dot(hi, spread_ref[...], preferred_element_type=F32)
                     + jnp.dot(lo, spread_ref[...], preferred_element_type=F32))


PAIR_ROWS = 2 * SUBLANES
DOWN_K = PEER_SLOTS * PAIR_ROWS
DOWN_PAIRS_PER_STEP = 4


def _peer_down_kernel(pair_ref, wexp_ref, diag_ref, x_ref, tab_ref, o_ref):
    tm = x_ref.shape[0]

    def left_rows(t):
        row = wexp_ref[pl.ds(t, 1), :]
        hi, lo = _split_bf16(row * diag_ref[...])
        return [hi, lo]

    def token_pair(ta):
        tb = ta + 1
        rhs = jnp.concatenate(
            [jnp.concatenate([tab_ref[pair_ref[ta, j]], tab_ref[pair_ref[tb, j]]], axis=1)
             for j in range(PEER_SLOTS)], axis=0)
        lhs = jnp.concatenate(left_rows(ta) + left_rows(tb), axis=0)
        out = jnp.dot(lhs, rhs, preferred_element_type=F32)
        s = SUBLANES
        o_ref[ta] = x_ref[ta] + (out[0:s, :LANES] + out[s:2 * s, :LANES])
        o_ref[tb] = x_ref[tb] + (out[2 * s:3 * s, LANES:] + out[3 * s:4 * s, LANES:])

    def step(i, _):
        for k in range(DOWN_PAIRS_PER_STEP):
            token_pair(2 * (DOWN_PAIRS_PER_STEP * i + k))
        return 0

    lax.fori_loop(0, tm // (2 * DOWN_PAIRS_PER_STEP), step, 0)


def _smem_rows(tm):
    return pl.BlockSpec((tm, PEER_SLOTS), lambda i: (i, 0), memory_space=pltpu.SMEM)


def _peer_up(pair, parity, xn3, table, gate, spread, *, tm):
    t = xn3.shape[0]
    rows = pl.BlockSpec((tm, PEER_SLOTS), lambda i: (i, 0))
    return pl.pallas_call(
        _peer_up_kernel,
        grid=(t // tm,),
        in_specs=[_smem_rows(tm),
                  pl.BlockSpec((tm, SUBLANES, LANES), lambda i: (i, 0, 0)),
                  _resident(table.shape), rows, rows, _resident(spread.shape)],
        out_specs=[pl.BlockSpec((tm, DOWN_K), lambda i: (i, 0)), rows],
        out_shape=[jax.ShapeDtypeStruct((t, DOWN_K), F32),
                   jax.ShapeDtypeStruct((t, PEER_SLOTS), F32)],
        scratch_shapes=[pltpu.VMEM((tm, 2 * PEER_SLOTS), F32)],
        compiler_params=_params(1),
        name="peer_up",
    )(pair, xn3, table, gate, parity, spread)


def _peer_down(pair, wexp, diag, x3, table, *, tm, n_tokens):
    tile = pl.BlockSpec((tm, SUBLANES, LANES), lambda i: (i, 0, 0))
    return pl.pallas_call(
        _peer_down_kernel,
        grid=(n_tokens // tm,),
        in_specs=[_smem_rows(tm), pl.BlockSpec((tm, DOWN_K), lambda i: (i, 0)),
                  _resident(diag.shape), tile, _resident(table.shape)],
        out_specs=tile,
        out_shape=jax.ShapeDtypeStruct((n_tokens,) + x3.shape[1:], F32),
        compiler_params=_params(1),
        name="peer_down",
    )(pair, wexp, diag, x3, table)


SC_WORKERS = 32
SC_LANES = 16
SC_HALF = PEER_SLOTS // 2
SC_ROWS_PER_STEP = 4
SC_CHUNKS = 8


def _sc_word_table(v):
    e, d = v.shape
    bits = lax.bitcast_convert_type(v.astype(BF16), jnp.uint16).astype(jnp.uint32)
    blk = bits.reshape(e, d // (2 * SC_LANES), 2, SC_LANES)
    words = (blk[:, :, 0] << 16) | blk[:, :, 1]
    return lax.bitcast_convert_type(words.reshape(e, d // 2), jnp.int32)


def _sc_peer_down(expert, w_lanes, x2d, table, *, first, n_tokens):
    t, d = x2d.shape
    per_worker = n_tokens // SC_WORKERS
    words = d // 2
    mesh = plsc.VectorSubcoreMesh(core_axis_name="c", subcore_axis_name="s")

    @functools.partial(
        pl.kernel, mesh=mesh,
        out_type=jax.ShapeDtypeStruct((t, d), F32),
        scratch_types=[
            pltpu.VMEM((PEER_SLOTS,), jnp.int32),
            pltpu.VMEM((PEER_SLOTS, SC_LANES), F32),
            pltpu.VMEM((d,), F32),
            pltpu.VMEM((2, SC_HALF, words), jnp.int32),
            pltpu.SemaphoreType.DMA,
            pltpu.SemaphoreType.DMA,
        ],
        compiler_params=pltpu.CompilerParams(needs_layout_passes=False),
        name="sc_peer_down",
    )
    def run(expert_hbm, w_hbm, x_hbm, table_hbm, out_hbm, idx_v, w_v, out_v, rows_v, sem0, sem1):
        worker = lax.axis_index("s") * 2 + lax.axis_index("c")
        local = worker * per_worker
        base = first + local

        def accumulate(h):
            @pl.loop(0, SC_HALF, step=SC_ROWS_PER_STEP)
            def _(j0):
                weights = [w_v[h * SC_HALF + j0 + r, :] for r in range(SC_ROWS_PER_STEP)]
                for g in range(words // SC_LANES // SC_CHUNKS):
                    loaded = [[rows_v[h, j0 + r, pl.ds((g * SC_CHUNKS + q) * SC_LANES, SC_LANES)]
                               for q in range(SC_CHUNKS)] for r in range(SC_ROWS_PER_STEP)]
                    for q in range(SC_CHUNKS):
                        hi_sum = lo_sum = None
                        for r in range(SC_ROWS_PER_STEP):
                            word = loaded[r][q]
                            hi = weights[r] * lax.bitcast_convert_type(word & jnp.int32(-65536), F32)
                            lo = weights[r] * lax.bitcast_convert_type(word << 16, F32)
                            hi_sum = hi if hi_sum is None else hi_sum + hi
                            lo_sum = lo if lo_sum is None else lo_sum + lo
                        col = (g * SC_CHUNKS + q) * 2 * SC_LANES
                        plsc.addupdate(out_v.at[pl.ds(col, SC_LANES)], hi_sum)
                        plsc.addupdate(out_v.at[pl.ds(col + SC_LANES, SC_LANES)], lo_sum)

        @pl.loop(0, per_worker)
        def _(i):
            tok = base + i
            pltpu.sync_copy(expert_hbm.at[tok], idx_v)
            pltpu.sync_copy(w_hbm.at[local + i], w_v)
            pltpu.sync_copy(x_hbm.at[tok], out_v)
            first_half = pltpu.async_copy(table_hbm.at[idx_v.at[pl.ds(0, SC_HALF)]],
                                          rows_v.at[0], sem0)
            second_half = pltpu.async_copy(table_hbm.at[idx_v.at[pl.ds(SC_HALF, SC_HALF)]],
                                           rows_v.at[1], sem1)
            first_half.wait()
            accumulate(0)
            second_half.wait()
            accumulate(1)
            pltpu.sync_copy(out_v, out_hbm.at[tok])

    return run(expert, w_lanes, x2d, table)


def _peer_constants():
    col = jnp.arange(DOWN_K)
    src = ((col // SUBLANES) % 2) * PEER_SLOTS + col // PAIR_ROWS
    spread = (jnp.arange(2 * PEER_SLOTS)[:, None] == src[None, :]).astype(BF16)
    diag = (col[None, :] % SUBLANES == jnp.arange(SUBLANES)[:, None]).astype(F32)
    return spread, diag


def _pair_slabs(v):
    e, d = v.shape
    return v.astype(BF16).reshape(e // 2, 2 * d // LANES, LANES)


def _peer(x2d, g, wq_bf16, keys_bf16, u_packed, v_slabs, v_words, consts, *, tm_route, tm_expert,
          n_sparsecore):
    t, d = x2d.shape
    spread, diag = consts
    n_tc = t - n_sparsecore
    xn, pair, parity, gate = _peer_route(x2d, g, wq_bf16, keys_bf16, tm=tm_route)
    pair, parity, gate = pair.T, parity.T, gate.T
    xn3 = xn.reshape(t, d // LANES, LANES)
    wexp, w = _peer_up(pair, parity, xn3, u_packed, gate, spread, tm=tm_expert)
    out3 = _peer_down(pair, wexp, diag, x2d.reshape(t, d // LANES, LANES), v_slabs, tm=tm_expert,
                      n_tokens=n_tc)
    if not n_sparsecore:
        return out3.reshape(t, d)
    w_lanes = jnp.broadcast_to(w[n_tc:, :, None], (n_sparsecore, PEER_SLOTS, SC_LANES))
    out = _sc_peer_down(pair * 2 + parity, w_lanes, x2d, v_words, first=n_tc,
                        n_tokens=n_sparsecore)
    return lax.dynamic_update_slice(out, out3.reshape(n_tc, d), (0, 0))


def _final_norm_kernel(x_ref, g_ref, o_ref):
    o_ref[...] = _rms_scale(x_ref[...], g_ref[...])


def _final_norm(x2d, g, *, tm):
    t, d = x2d.shape
    return pl.pallas_call(
        _final_norm_kernel,
        grid=(t // tm,),
        in_specs=[pl.BlockSpec((tm, d), lambda i: (i, 0)), _resident((1, d))],
        out_specs=pl.BlockSpec((tm, d), lambda i: (i, 0)),
        out_shape=jax.ShapeDtypeStruct((t, d), F32),
        compiler_params=_params(1),
        name="final_norm",
    )(x2d, g.reshape(1, d))


def _tiles(seq):
    return dict(tm_proj=min(256, seq), tm_out=min(512, seq), tq=min(512, seq), tk=min(512, seq),
                tm_route=LANES, tm_expert=LANES, tm_norm=min(512, seq))


SC_SHARE = 0.3125


def _sparsecore_tokens(t, tm):
    unit = tm * SC_WORKERS // math.gcd(tm, SC_WORKERS)
    return int(t * SC_SHARE) // unit * unit


def _to_residue_major(a2d, batch, seq, dilation, c0, c1):
    cols = a2d[:, c0:c1].reshape(batch, seq // dilation, dilation, c1 - c0)
    return cols.transpose(0, 2, 1, 3)


def _from_residue_major(a4):
    b, d, length, c = a4.shape
    return a4.transpose(0, 2, 1, 3).reshape(b * d * length, c)


def kernel(x, mem, norm_mix, a_w_in, a_lambda, a_subln, b_w_in, norm_mem, w_mem_kv, w_out,
           norm_ffn, peer_wq, peer_keys, peer_u, peer_v, shared_norm, shared_w_kv, final_norm):
    batch, seq, d = x.shape
    mem_tokens = mem.shape[1]
    depth = norm_mix.shape[0]
    n_a = a_w_in.shape[0]
    t = batch * seq
    tiles = _tiles(seq)
    rope = _rope_tables(seq)
    scale = HEAD_DIM ** -0.5

    diff_qk = DIFF_HEADS * 2 * HEAD_DIM
    dil_qk = len(DIL_GROUPS) * DIL_GROUP_WIDTH
    peer_consts = _peer_constants()
    expand = jnp.repeat(jnp.eye(LANES, DIL_HEADS, dtype=BF16), HEAD_DIM, axis=1)

    x2d = x.reshape(t, d)
    mem2d = mem.reshape(batch * mem_tokens, d)
    shared = None

    for l in range(depth):
        memkv = _proj(mem2d, norm_mem[l], w_mem_kv[l].astype(BF16), tm=mem_tokens)
        w_o = w_out[l].astype(BF16)
        if l < n_a:
            col_scale = jnp.concatenate([jnp.full((diff_qk,), scale, F32),
                                         jnp.ones((a_w_in.shape[2] - diff_qk - MEM_Q_WIDTH,), F32),
                                         jnp.full((MEM_Q_WIDTH,), scale, F32)])
            w_in = (a_w_in[l] * col_scale).astype(BF16)
            proj = _proj(x2d, norm_mix[l], w_in, tm=tiles["tm_proj"], n_rope=2 * diff_qk,
                         rope=rope, seq=seq)
            lambda_init = 0.8 - 0.6 * math.exp(-0.3 * l)
            mix = _diff_attention(proj, a_lambda[l], a_subln[l], batch=batch, seq=seq,
                                  lambda_init=lambda_init, tq=tiles["tq"], tk=tiles["tk"])
            x2d = _outproj(x2d, proj, memkv, w_o, seq=seq, mem_tokens=mem_tokens,
                           tm=tiles["tm_out"], mix=mix)
        else:
            w_in = (b_w_in[l - n_a] * scale).astype(BF16)
            proj = _proj(x2d, norm_mix[l], w_in, tm=tiles["tm_proj"], n_rope=dil_qk,
                         rope=rope, seq=seq)
            outs, stats = [], []
            for gi, (window, dilation) in enumerate(DIL_GROUPS):
                c0 = gi * DIL_GROUP_WIDTH
                blk0 = c0 // LANES
                if dilation == 1:
                    q4, q_off = proj.reshape(batch, 1, seq, proj.shape[1]), blk0
                else:
                    q4, q_off = _to_residue_major(proj, batch, seq, dilation,
                                                  c0, c0 + DIL_GROUP_WIDTH), 0
                k4, k_off, v4, v_off = shared[gi]
                o4, st4 = _dilated_group(q4, k4, v4, q_off=q_off, k_off=k_off, v_off=v_off,
                                         n_steps=window // dilation)
                outs.append(_from_residue_major(o4))
                stats.append(_from_residue_major(st4))
            x2d = _outproj(x2d, proj, memkv, w_o, seq=seq, mem_tokens=mem_tokens,
                           tm=tiles["tm_out"], group_out=outs, group_stats=stats, expand=expand)

        x2d = _peer(x2d, norm_ffn[l], peer_wq[l].astype(BF16),
                    peer_keys[l].reshape(2 * PEER_HEADS, PEER_N_KEYS, PEER_HALF).astype(BF16),
                    _pack_expert_table(peer_u[l]), _pair_slabs(peer_v[l]),
                    _sc_word_table(peer_v[l]), peer_consts,
                    tm_route=tiles["tm_route"], tm_expert=tiles["tm_expert"],
                    n_sparsecore=_sparsecore_tokens(t, tiles["tm_expert"]))

        if l == n_a - 1:
            kv = _proj(x2d, shared_norm, shared_w_kv.astype(BF16), tm=tiles["tm_proj"],
                       n_rope=dil_qk, rope=rope, seq=seq)
            shared = []
            for gi, (window, dilation) in enumerate(DIL_GROUPS):
                c0 = gi * DIL_GROUP_WIDTH
                if dilation == 1:
                    kv4 = kv.reshape(batch, 1, seq, kv.shape[1])
                    shared.append((kv4, c0 // LANES, kv4, (dil_qk + c0) // LANES))
                else:
                    k4 = _to_residue_major(kv, batch, seq, dilation, c0, c0 + DIL_GROUP_WIDTH)
                    v4 = _to_residue_major(kv, batch, seq, dilation, dil_qk + c0,
                                           dil_qk + c0 + DIL_GROUP_WIDTH)
                    shared.append((k4, 0, v4, 0))

    return _final_norm(x2d, final_norm, tm=tiles["tm_norm"]).reshape(batch, seq, d)
```

```python
import functools
import math

import jax
import jax.numpy as jnp
from jax import lax
from jax.experimental import pallas as pl
from jax.experimental.pallas import tpu as pltpu
from jax.experimental.pallas import tpu_sc as plsc

F32 = jnp.float32
BF16 = jnp.bfloat16

HEAD_DIM = 64
ROPE_DIMS = HEAD_DIM // 4
ROPE_HALF = ROPE_DIMS // 2
ROPE_THETA = 500000.0
NORM_EPS = 1e-5
DIFF_HEADS = 6
MEM_HEADS = 4
MEM_Q_WIDTH = MEM_HEADS * HEAD_DIM
DIL_GROUPS = ((128, 1), (512, 4), (2048, 16))
DIL_HEADS = 12
DIL_BLOCK = 128
DIL_GROUP_WIDTH = DIL_HEADS * HEAD_DIM
PEER_HEADS = 8
PEER_N_KEYS = 128
PEER_TOPK = 16
PEER_HALF = 128
PEER_SLOTS = PEER_HEADS * PEER_TOPK

LANES = 128
SUBLANES = 8
VMEM_LIMIT_BYTES = 56 * 1024 * 1024

_NT = (((1,), (1,)), ((), ()))


def _params(n_axes):
    return pltpu.CompilerParams(dimension_semantics=("arbitrary",) * n_axes,
                                vmem_limit_bytes=VMEM_LIMIT_BYTES)


def _resident(shape):
    zeros = (0,) * len(shape)
    return pl.BlockSpec(shape, lambda *_: zeros, pipeline_mode=pl.Buffered(1))


def _rms_scale(x, g):
    ms = jnp.mean(x * x, axis=-1, keepdims=True)
    return x * lax.rsqrt(ms + NORM_EPS) * g


def _rope_tables(seq):
    inv = ROPE_THETA ** (-jnp.arange(0, ROPE_DIMS, 2, dtype=F32) / ROPE_DIMS)
    ang = jnp.arange(seq, dtype=F32)[:, None] * inv[None, :]
    cos, sin = jnp.cos(ang), jnp.sin(ang)
    ones = jnp.ones((seq, HEAD_DIM - ROPE_DIMS), F32)
    zeros = jnp.zeros((seq, HEAD_DIM - ROPE_DIMS), F32)
    zh = jnp.zeros((seq, ROPE_HALF), F32)
    c = jnp.concatenate([cos, cos, ones], axis=1)
    sa = jnp.concatenate([-sin, zh, zeros], axis=1)
    sb = jnp.concatenate([zh, sin, zeros], axis=1)
    rep = LANES // HEAD_DIM
    return tuple(jnp.tile(t, (1, rep)) for t in (c, sa, sb))


def _proj_kernel(*refs, n_rope, chunk):
    if n_rope:
        x_ref, g_ref, w_ref, c_ref, sa_ref, sb_ref, o_ref = refs
    else:
        x_ref, g_ref, w_ref, o_ref = refs
    y = _rms_scale(x_ref[...], g_ref[...]).astype(BF16)
    n = o_ref.shape[1]
    for c0 in range(0, n, chunk):
        acc = jnp.dot(y, w_ref[:, c0:c0 + chunk], preferred_element_type=F32)
        if c0 < n_rope:
            for k0 in range(0, chunk, LANES):
                a = acc[:, k0:k0 + LANES]
                a = (a * c_ref[...]
                     + pltpu.roll(a, LANES - ROPE_HALF, 1) * sa_ref[...]
                     + pltpu.roll(a, ROPE_HALF, 1) * sb_ref[...])
                o_ref[:, c0 + k0:c0 + k0 + LANES] = a.astype(o_ref.dtype)
        else:
            o_ref[:, c0:c0 + chunk] = acc.astype(o_ref.dtype)


def _proj(x2d, g, w_bf16, *, tm, n_rope=0, rope=None, seq=None):
    t, d = x2d.shape
    n = w_bf16.shape[1]
    chunk = 256
    assert t % tm == 0 and n % chunk == 0 and n_rope % chunk == 0
    in_specs = [pl.BlockSpec((tm, d), lambda i: (i, 0)),
                _resident((1, d)),
                _resident((d, n))]
    args = [x2d, g.reshape(1, d), w_bf16]
    if n_rope:
        nblk = seq // tm
        spec = pl.BlockSpec((tm, LANES), lambda i: (i % nblk, 0))
        in_specs += [spec, spec, spec]
        args += list(rope)
    return pl.pallas_call(
        functools.partial(_proj_kernel, n_rope=n_rope, chunk=chunk),
        grid=(t // tm,),
        in_specs=in_specs,
        out_specs=pl.BlockSpec((tm, n), lambda i: (i, 0)),
        out_shape=jax.ShapeDtypeStruct((t, n), BF16),
        compiler_params=_params(1),
        name="norm_proj",
    )(*args)


def _online_softmax_step(q, k, v, carry, mask):
    m, l, acc = carry
    s = lax.dot_general(q, k, _NT, preferred_element_type=F32)
    if mask is not None:
        s = jnp.where(mask, s, -jnp.inf)
    m_new = jnp.maximum(m, jnp.max(s, axis=-1, keepdims=True))
    alpha = jnp.exp(m - m_new)
    p = jnp.exp(s - m_new)
    l = alpha * l + jnp.sum(p, axis=-1, keepdims=True)
    acc = alpha * acc + jnp.dot(p.astype(BF16), v, preferred_element_type=F32)
    return m_new, l, acc


def _diff_attn_kernel(q_ref, k_ref, v_ref, lp_ref, g_ref, o_ref, *, tq, tk, lambda_init):
    qi = pl.program_id(2)
    q = q_ref[...]
    lane = lax.broadcasted_iota(jnp.int32, q.shape, 1)
    zero = jnp.zeros_like(q)
    qa = jnp.where(lane < HEAD_DIM, q, zero)
    qb = jnp.where(lane >= HEAD_DIM, q, zero)

    def init():
        return (jnp.full((tq, 1), -jnp.inf, F32), jnp.zeros((tq, 1), F32),
                jnp.zeros((tq, 2 * HEAD_DIM), F32))

    def body(j, carry):
        c1, c2 = carry
        k = k_ref[pl.ds(pl.multiple_of(j * tk, tk), tk), :]
        v = v_ref[pl.ds(pl.multiple_of(j * tk, tk), tk), :]
        return (_online_softmax_step(qa, k, v, c1, None),
                _online_softmax_step(qb, k, v, c2, None))

    n_full = (qi * tq) // tk
    c1, c2 = lax.fori_loop(0, n_full, body, (init(), init()))
    row = qi * tq + lax.broadcasted_iota(jnp.int32, (tq, tk), 0)
    col = n_full * tk + lax.broadcasted_iota(jnp.int32, (tq, tk), 1)
    causal = col <= row
    start = pl.multiple_of(n_full * tk, tk)
    k = k_ref[pl.ds(start, tk), :]
    v = v_ref[pl.ds(start, tk), :]
    m1, l1, a1 = _online_softmax_step(qa, k, v, c1, causal)
    m2, l2, a2 = _online_softmax_step(qb, k, v, c2, causal)

    lp = lp_ref[...]
    lam = (jnp.exp(jnp.sum(lp[0:1] * lp[1:2], axis=-1, keepdims=True))
           - jnp.exp(jnp.sum(lp[2:3] * lp[3:4], axis=-1, keepdims=True)) + lambda_init)
    o = a1 / l1 - lam * (a2 / l2)
    o = _rms_scale(o, g_ref[...]) * (1.0 - lambda_init)
    o_ref[...] = o.astype(o_ref.dtype)


def _diff_attention(proj, lp, subln, *, batch, seq, lambda_init, tq, tk):
    assert tk % tq == 0 and seq % tk == 0
    nq = seq // tq
    width = 2 * HEAD_DIM
    return pl.pallas_call(
        functools.partial(_diff_attn_kernel, tq=tq, tk=tk, lambda_init=lambda_init),
        grid=(batch, DIFF_HEADS, nq),
        in_specs=[
            pl.BlockSpec((tq, width), lambda b, h, i: (b * nq + i, h)),
            pl.BlockSpec((seq, width), lambda b, h, i: (b, DIFF_HEADS + h)),
            pl.BlockSpec((seq, width), lambda b, h, i: (b, 2 * DIFF_HEADS + h)),
            _resident((4, HEAD_DIM)),
            _resident((1, width)),
        ],
        out_specs=pl.BlockSpec((tq, width), lambda b, h, i: (b * nq + i, h)),
        out_shape=jax.ShapeDtypeStruct((batch * seq, DIFF_HEADS * width), BF16),
        compiler_params=_params(3),
        name="diff_attention",
    )(proj, proj, proj, lp, subln.reshape(1, width))


def _dilated_kernel(q_ref, kc_ref, kp_ref, vc_ref, vp_ref, o_ref, st_ref, *, tq, n_steps):
    n = pl.program_id(2)
    hp = pl.program_id(3)
    blk = DIL_BLOCK

    @pl.when(hp == 0)
    def _():
        st_ref[...] = jnp.zeros_like(st_ref)

    qrow = lax.broadcasted_iota(jnp.int32, (blk, 2 * blk), 0)
    kcol = lax.broadcasted_iota(jnp.int32, (blk, 2 * blk), 1)
    dist = qrow + blk - kcol
    band = (dist >= 0) & (dist <= n_steps)
    lane_q = lax.broadcasted_iota(jnp.int32, (blk, LANES), 1)
    lane_s = lax.broadcasted_iota(jnp.int32, (blk, LANES), 1)

    for i in range(tq // blk):
        rows = slice(i * blk, (i + 1) * blk)
        q = q_ref[rows, :]
        if i == 0:
            k_prev, v_prev = kp_ref[...], vp_ref[...]
            valid = band & ((n > 0) | (kcol >= blk))
        else:
            prev = slice((i - 1) * blk, i * blk)
            k_prev, v_prev = kc_ref[prev, :], vc_ref[prev, :]
            valid = band
        keys = jnp.concatenate([k_prev, kc_ref[rows, :]], axis=0)
        vals = jnp.concatenate([v_prev, vc_ref[rows, :]], axis=0)
        out = jnp.zeros((blk, LANES), F32)
        stats = st_ref[rows, :]
        for e in range(LANES // HEAD_DIM):
            head_lanes = (lane_q // HEAD_DIM) == e
            qe = jnp.where(head_lanes, q, jnp.zeros_like(q))
            s = lax.dot_general(qe, keys, _NT, preferred_element_type=F32)
            s = jnp.where(valid, s, -jnp.inf)
            m = jnp.max(s, axis=-1, keepdims=True)
            p = jnp.exp(s - m)
            den = jnp.sum(p, axis=-1, keepdims=True)
            oe = jnp.dot((p / den).astype(BF16), vals, preferred_element_type=F32)
            out = jnp.where(head_lanes, oe, out)
            lse = m + jnp.log(den)
            stats = jnp.where(lane_s == hp * (LANES // HEAD_DIM) + e, lse, stats)
        o_ref[rows, :] = out.astype(o_ref.dtype)
        st_ref[rows, :] = stats


def _dilated_group(q4, k4, v4, *, q_off, k_off, v_off, n_steps):
    b, d, length, _ = q4.shape
    tq = min(512, length)
    assert length % tq == 0 and tq % DIL_BLOCK == 0
    sub = tq // DIL_BLOCK
    pairs = DIL_GROUP_WIDTH // LANES

    def cur(off):
        return pl.BlockSpec((None, None, tq, LANES), lambda bi, r, n, h: (bi, r, n, off + h))

    def prev(off):
        return pl.BlockSpec((None, None, DIL_BLOCK, LANES),
                            lambda bi, r, n, h: (bi, r, jnp.maximum(n * sub - 1, 0), off + h))

    return pl.pallas_call(
        functools.partial(_dilated_kernel, tq=tq, n_steps=n_steps),
        grid=(b, d, length // tq, pairs),
        in_specs=[cur(q_off), cur(k_off), prev(k_off), cur(v_off), prev(v_off)],
        out_specs=[pl.BlockSpec((None, None, tq, LANES), lambda bi, r, n, h: (bi, r, n, h)),
                   pl.BlockSpec((None, None, tq, LANES), lambda bi, r, n, h: (bi, r, n, 0))],
        out_shape=[jax.ShapeDtypeStruct((b, d, length, DIL_GROUP_WIDTH), BF16),
                   jax.ShapeDtypeStruct((b, d, length, LANES), F32)],
        compiler_params=_params(4),
        name="dilated_attention",
    )(q4, k4, k4, v4, v4)


def _memory_attention(q, k, v):
    lane = lax.broadcasted_iota(jnp.int32, q.shape, 1)
    out = jnp.zeros(q.shape, F32)
    for h in range(MEM_HEADS):
        head_lanes = (lane // HEAD_DIM) == h
        qh = jnp.where(head_lanes, q, jnp.zeros_like(q))
        s = lax.dot_general(qh, k, _NT, preferred_element_type=F32)
        m = jnp.max(s, axis=-1, keepdims=True)
        p = jnp.exp(s - m)
        den = jnp.sum(p, axis=-1, keepdims=True)
        oh = jnp.dot((p / den).astype(BF16), v, preferred_element_type=F32)
        out = jnp.where(head_lanes, oh, out)
    return out


def _split_bf16(x):
    hi = x.astype(BF16)
    lo = (x - hi.astype(F32)).astype(BF16)
    return hi, lo


def _outproj_kernel(*refs, n_groups):
    if n_groups:
        x_ref, qm_ref, mk_ref, mv_ref, w_ref = refs[:5]
        o_refs = refs[5:5 + n_groups]
        st_refs = refs[5 + n_groups:5 + 2 * n_groups]
        ex_ref, out_ref = refs[5 + 2 * n_groups:]
        lses = [r[...] for r in st_refs]
        top = functools.reduce(jnp.maximum, lses)
        es = [jnp.exp(l - top) for l in lses]
        den = functools.reduce(lambda a, b: a + b, es)
        mix = None
        for e, o_ref in zip(es, o_refs):
            hi, lo = _split_bf16(e / den)
            wide = (jnp.dot(hi, ex_ref[...], preferred_element_type=F32)
                    + jnp.dot(lo, ex_ref[...], preferred_element_type=F32))
            term = wide * o_ref[...].astype(F32)
            mix = term if mix is None else mix + term
        mix = mix.astype(BF16)
    else:
        x_ref, qm_ref, mk_ref, mv_ref, w_ref, mix_ref, out_ref = refs
        mix = mix_ref[...]
    mo = _memory_attention(qm_ref[...], mk_ref[...], mv_ref[...]).astype(BF16)
    k_mix = mix.shape[1]
    acc = jnp.dot(mix, w_ref[:k_mix, :], preferred_element_type=F32)
    acc += jnp.dot(mo, w_ref[k_mix:, :], preferred_element_type=F32)
    out_ref[...] = x_ref[...] + acc


def _outproj(x2d, proj, memkv, w_bf16, *, seq, mem_tokens, tm, mix=None, group_out=None,
             group_stats=None, expand=None):
    t, d = x2d.shape
    per_batch = seq // tm
    qm_block = (proj.shape[1] - MEM_Q_WIDTH) // MEM_Q_WIDTH
    row = lambda i: (i, 0)
    in_specs = [pl.BlockSpec((tm, d), row),
                pl.BlockSpec((tm, MEM_Q_WIDTH), lambda i: (i, qm_block)),
                pl.BlockSpec((mem_tokens, MEM_Q_WIDTH), lambda i: (i // per_batch, 0)),
                pl.BlockSpec((mem_tokens, MEM_Q_WIDTH), lambda i: (i // per_batch, 1)),
                _resident(w_bf16.shape)]
    args = [x2d, proj, memkv, memkv, w_bf16]
    if mix is not None:
        n_groups = 0
        in_specs.append(pl.BlockSpec((tm, mix.shape[1]), row))
        args.append(mix)
    else:
        n_groups = len(group_out)
        in_specs += [pl.BlockSpec((tm, DIL_GROUP_WIDTH), row)] * n_groups
        in_specs += [pl.BlockSpec((tm, LANES), row)] * n_groups
        in_specs.append(_resident(expand.shape))
        args += list(group_out) + list(group_stats) + [expand]
    return pl.pallas_call(
        functools.partial(_outproj_kernel, n_groups=n_groups),
        grid=(t // tm,),
        in_specs=in_specs,
        out_specs=pl.BlockSpec((tm, d), row),
        out_shape=jax.ShapeDtypeStruct((t, d), F32),
        compiler_params=_params(1),
        name="mix_outproj",
    )(*args)


def _topk_rows(s, k):
    n = s.shape[0]
    row = lax.broadcasted_iota(jnp.int32, s.shape, 0)
    vals, idxs = [], []
    for _ in range(k):
        m = jnp.max(s, axis=0, keepdims=True)
        i = jnp.min(jnp.where(s == m, row, n), axis=0, keepdims=True)
        vals.append(m)
        idxs.append(i)
        s = jnp.where(row == i, -jnp.inf, s)
    return jnp.concatenate(vals, axis=0), jnp.concatenate(idxs, axis=0)


def _select_rows(table, idx):
    row = lax.broadcasted_iota(jnp.int32, table.shape, 0)
    return jnp.sum(jnp.where(row == idx, table, jnp.zeros_like(table)), axis=0, keepdims=True)


def _peer_route_kernel(x_ref, g_ref, wq_ref, keys_ref, xn_ref, pair_ref, parity_ref, gate_ref):
    xn = _rms_scale(x_ref[...], g_ref[...])
    xn_ref[...] = xn
    q = jnp.dot(xn.astype(BF16), wq_ref[...], preferred_element_type=F32).astype(BF16)
    kk = PEER_TOPK
    for h in range(PEER_HEADS):
        tops = []
        for half in range(2):
            c0 = (2 * h + half) * PEER_HALF
            s = lax.dot_general(keys_ref[2 * h + half], q[:, c0:c0 + PEER_HALF], _NT,
                                preferred_element_type=F32)
            tops.append(_topk_rows(s, kk))
        (s1, i1), (s2, i2) = tops
        half = kk // 2
        sub = lax.broadcasted_iota(jnp.int32, (half, s1.shape[1]), 0)
        blocks = [s1[0:1] + s2]
        experts = [i1[0:1] * PEER_N_KEYS + i2]
        for a in range(1, half):
            blocks.append(jnp.where(sub < kk // (a + 1), s1[a:a + 1] + s2[:half], -jnp.inf))
            experts.append(i1[a:a + 1] * PEER_N_KEYS + i2[:half])
        blocks.append(s1[half:] + s2[0:1])
        experts.append(i1[half:] * PEER_N_KEYS + i2[0:1])
        cand = jnp.concatenate(blocks, axis=0)
        cand_expert = jnp.concatenate(experts, axis=0)
        top_s, top_j = _topk_rows(cand, kk)
        expert = jnp.concatenate([_select_rows(cand_expert, top_j[r:r + 1]) for r in range(kk)],
                                 axis=0)
        e = jnp.exp(top_s - top_s[0:1])
        gate = e / jnp.sum(e, axis=0, keepdims=True)
        rows = slice(h * kk, (h + 1) * kk)
        pair_ref[rows, :] = expert >> 1
        parity_ref[rows, :] = expert & 1
        gate_ref[rows, :] = gate


def _peer_route(x2d, g, wq_bf16, keys_bf16, *, tm):
    t, d = x2d.shape
    col = lambda i: (0, i)
    slot_shape = jax.ShapeDtypeStruct((PEER_SLOTS, t), jnp.int32)
    return pl.pallas_call(
        _peer_route_kernel,
        grid=(t // tm,),
        in_specs=[pl.BlockSpec((tm, d), lambda i: (i, 0)),
                  _resident((1, d)),
                  _resident(wq_bf16.shape),
                  _resident(keys_bf16.shape)],
        out_specs=[pl.BlockSpec((tm, d), lambda i: (i, 0)),
                   pl.BlockSpec((PEER_SLOTS, tm), col),
                   pl.BlockSpec((PEER_SLOTS, tm), col),
                   pl.BlockSpec((PEER_SLOTS, tm), col)],
        out_shape=[jax.ShapeDtypeStruct((t, d), F32), slot_shape, slot_shape,
                   jax.ShapeDtypeStruct((PEER_SLOTS, t), F32)],
        compiler_params=_params(1),
        name="peer_route",
    )(x2d, g.reshape(1, d), wq_bf16, keys_bf16)


_HI_MASK = 0xFFFF0000


def _pack_expert_table(u):
    e, d = u.shape
    bits = lax.bitcast_convert_type(u.astype(BF16), jnp.uint16).astype(jnp.uint32)
    bits = bits.reshape(e // 2, 2, d)
    packed = (bits[:, 0] << 16) | bits[:, 1]
    return packed.reshape(e // 2, d // LANES, LANES)


_BIT_REVERSED = (0, 4, 2, 6, 1, 5, 3, 7)


def _bf16_pair_words(x):
    bits = pltpu.bitcast(x, jnp.uint32)
    top = (bits + jnp.uint32(0x7FFF) + ((bits >> 16) & jnp.uint32(1))) >> 16
    return (top << 16) | top


def _packed_add(a, b):
    return pltpu.bitcast(pltpu.bitcast(a, BF16) + pltpu.bitcast(b, BF16), jnp.uint32)


def _sublane_sums(words, sub):
    level = [words[i] for i in _BIT_REVERSED]
    for k in (4, 2, 1):
        low = (sub & k) == 0
        merged = []
        for a, b in zip(level[0::2], level[1::2]):
            if k == 4:
                merged.append(_packed_add(jnp.where(low, a, b),
                                          pltpu.roll(jnp.where(low, b, a), k, 0)))
            else:
                merged.append(jnp.where(low,
                                        _packed_add(a, pltpu.roll(a, SUBLANES - k, 0)),
                                        _packed_add(b, pltpu.roll(b, k, 0))))
        level = merged
    return level[0]


def _peer_up_kernel(pair_ref, x_ref, tab_ref, gate_ref, par_ref, spread_ref, wexp_ref, w2_ref):
    tm = x_ref.shape[0]
    sub = lax.broadcasted_iota(jnp.int32, (SUBLANES, LANES), 0)
    hi_mask = jnp.uint32(_HI_MASK)

    def token_sums(t, m):
        xx = pltpu.bitcast(_bf16_pair_words(x_ref[t]), BF16)
        even, odd = [], []
        for g in range(PEER_SLOTS // SUBLANES):
            words = []
            for i in range(SUBLANES):
                slab = pltpu.bitcast(tab_ref[pair_ref[t, g * SUBLANES + i]], BF16)
                words.append(pltpu.bitcast(slab * xx, jnp.uint32))
            q = _sublane_sums(words, sub)
            even.append(pltpu.bitcast(q & hi_mask, F32))
            odd.append(pltpu.bitcast(q << 16, F32))
        lane_parts = jnp.concatenate(even + odd, axis=0).astype(BF16)
        onehot = (sub == m).astype(BF16)
        return lax.dot_general(onehot, lane_parts, _NT, preferred_element_type=F32)

    def group(gi, _):
        base = pl.multiple_of(gi * SUBLANES, SUBLANES)
        h2 = token_sums(base, 0)
        for m in range(1, SUBLANES):
            h2 = h2 + token_sums(base + m, m)
        odd_slot = par_ref[pl.ds(base, SUBLANES), :] != 0
        h = jnp.where(odd_slot, h2[:, PEER_SLOTS:], h2[:, :PEER_SLOTS])
        act = 0.5 * h * (1.0 + lax.erf(h * (1.0 / math.sqrt(2.0))))
        w = gate_ref[pl.ds(base, SUBLANES), :] * act
        zero = jnp.zeros_like(w)
        w2_ref[pl.ds(base, SUBLANES), 0:PEER_SLOTS] = jnp.where(odd_slot, zero, w)
        w2_ref[pl.ds(base, SUBLANES), PEER_SLOTS:2 * PEER_SLOTS] = jnp.where(odd_slot, w, zero)
        return 0

    lax.fori_loop(0, tm // SUBLANES, group, 0)
    hi, lo = _split_bf16(w2_ref[...])
    wexp_ref[...] = (jnp.dot(hi, spread_ref[...], preferred_element_type=F32)
                     + jnp.dot(lo, spread_ref[...], preferred_element_type=F32))


PAIR_ROWS = 2 * SUBLANES
DOWN_K = PEER_SLOTS * PAIR_ROWS
DOWN_PAIRS_PER_STEP = 4


def _peer_down_kernel(pair_ref, wexp_ref, diag_ref, x_ref, tab_ref, o_ref):
    tm = x_ref.shape[0]

    def left_rows(t):
        row = wexp_ref[pl.ds(t, 1), :]
        hi, lo = _split_bf16(row * diag_ref[...])
        return [hi, lo]

    def token_pair(ta):
        tb = ta + 1
        rhs = jnp.concatenate(
            [jnp.concatenate([tab_ref[pair_ref[ta, j]], tab_ref[pair_ref[tb, j]]], axis=1)
             for j in range(PEER_SLOTS)], axis=0)
        lhs = jnp.concatenate(left_rows(ta) + left_rows(tb), axis=0)
        out = jnp.dot(lhs, rhs, preferred_element_type=F32)
        s = SUBLANES
        o_ref[ta] = x_ref[ta] + (out[0:s, :LANES] + out[s:2 * s, :LANES])
        o_ref[tb] = x_ref[tb] + (out[2 * s:3 * s, LANES:] + out[3 * s:4 * s, LANES:])

    def step(i, _):
        for k in range(DOWN_PAIRS_PER_STEP):
            token_pair(2 * (DOWN_PAIRS_PER_STEP * i + k))
        return 0

    lax.fori_loop(0, tm // (2 * DOWN_PAIRS_PER_STEP), step, 0)


def _smem_rows(tm):
    return pl.BlockSpec((tm, PEER_SLOTS), lambda i: (i, 0), memory_space=pltpu.SMEM)


def _peer_up(pair, parity, xn3, table, gate, spread, *, tm, n_tokens):
    rows = pl.BlockSpec((tm, PEER_SLOTS), lambda i: (i, 0))
    return pl.pallas_call(
        _peer_up_kernel,
        grid=(n_tokens // tm,),
        in_specs=[_smem_rows(tm),
                  pl.BlockSpec((tm, SUBLANES, LANES), lambda i: (i, 0, 0)),
                  _resident(table.shape), rows, rows, _resident(spread.shape)],
        out_specs=pl.BlockSpec((tm, DOWN_K), lambda i: (i, 0)),
        out_shape=jax.ShapeDtypeStruct((n_tokens, DOWN_K), F32),
        scratch_shapes=[pltpu.VMEM((tm, 2 * PEER_SLOTS), F32)],
        compiler_params=_params(1),
        name="peer_up",
    )(pair, xn3, table, gate, parity, spread)


def _peer_down(pair, wexp, diag, x3, table, *, tm, n_tokens):
    tile = pl.BlockSpec((tm, SUBLANES, LANES), lambda i: (i, 0, 0))
    return pl.pallas_call(
        _peer_down_kernel,
        grid=(n_tokens // tm,),
        in_specs=[_smem_rows(tm), pl.BlockSpec((tm, DOWN_K), lambda i: (i, 0)),
                  _resident(diag.shape), tile, _resident(table.shape)],
        out_specs=tile,
        out_shape=jax.ShapeDtypeStruct((n_tokens,) + x3.shape[1:], F32),
        compiler_params=_params(1),
        name="peer_down",
    )(pair, wexp, diag, x3, table)


SC_WORKERS = 32
SC_LANES = 16
SC_HALF = PEER_SLOTS // 2
SC_ROWS_PER_STEP = 4
SC_CHUNKS = 8


def _sc_word_table(v):
    e, d = v.shape
    bits = lax.bitcast_convert_type(v.astype(BF16), jnp.uint16).astype(jnp.uint32)
    blk = bits.reshape(e, d // (2 * SC_LANES), 2, SC_LANES)
    words = (blk[:, :, 0] << 16) | blk[:, :, 1]
    return lax.bitcast_convert_type(words.reshape(e, d // 2), jnp.int32)


def _sc_peer_up(expert, xn2d, table, *, first, n_tokens):
    t, d = xn2d.shape
    per_worker = n_tokens // SC_WORKERS
    words = d // 2
    mesh = plsc.VectorSubcoreMesh(core_axis_name="c", subcore_axis_name="s")

    @functools.partial(
        pl.kernel, mesh=mesh,
        out_type=jax.ShapeDtypeStruct((n_tokens, PEER_SLOTS), F32),
        scratch_types=[
            pltpu.VMEM((PEER_SLOTS,), jnp.int32),
            pltpu.VMEM((d,), F32),
            pltpu.VMEM((2, SC_HALF, words), jnp.int32),
            pltpu.VMEM((PEER_SLOTS * SC_LANES,), F32),
            pltpu.VMEM((PEER_SLOTS,), F32),
            pltpu.SemaphoreType.DMA,
            pltpu.SemaphoreType.DMA,
        ],
        compiler_params=pltpu.CompilerParams(needs_layout_passes=False),
        name="sc_peer_up",
    )
    def run(expert_hbm, x_hbm, table_hbm, h_hbm, idx_v, x_v, rows_v, part_v, h_v, sem0, sem1):
        worker = lax.axis_index("s") * 2 + lax.axis_index("c")
        local = worker * per_worker
        base = first + local
        lane = lax.iota(jnp.int32, SC_LANES)

        def partial_sums(h):
            @pl.loop(0, SC_HALF, step=SC_ROWS_PER_STEP)
            def _(j0):
                sums = [None] * SC_ROWS_PER_STEP
                for m in range(words // SC_LANES):
                    xa = x_v[pl.ds(2 * m * SC_LANES, SC_LANES)]
                    xb = x_v[pl.ds((2 * m + 1) * SC_LANES, SC_LANES)]
                    for r in range(SC_ROWS_PER_STEP):
                        word = rows_v[h, j0 + r, pl.ds(m * SC_LANES, SC_LANES)]
                        term = (xa * lax.bitcast_convert_type(word & jnp.int32(-65536), F32)
                                + xb * lax.bitcast_convert_type(word << 16, F32))
                        sums[r] = term if sums[r] is None else sums[r] + term
                for r in range(SC_ROWS_PER_STEP):
                    part_v[pl.ds((h * SC_HALF + j0 + r) * SC_LANES, SC_LANES)] = sums[r]

        @pl.loop(0, per_worker)
        def _(i):
            tok = base + i
            pltpu.sync_copy(expert_hbm.at[tok], idx_v)
            pltpu.sync_copy(x_hbm.at[tok], x_v)
            first_half = pltpu.async_copy(table_hbm.at[idx_v.at[pl.ds(0, SC_HALF)]],
                                          rows_v.at[0], sem0)
            second_half = pltpu.async_copy(table_hbm.at[idx_v.at[pl.ds(SC_HALF, SC_HALF)]],
                                           rows_v.at[1], sem1)
            first_half.wait()
            partial_sums(0)
            second_half.wait()
            partial_sums(1)
            for g in range(PEER_SLOTS // SC_LANES):
                total = None
                for k in range(SC_LANES):
                    column = plsc.load_gather(
                        part_v, [g * SC_LANES * SC_LANES + lane * SC_LANES + k])
                    total = column if total is None else total + column
                h_v[pl.ds(g * SC_LANES, SC_LANES)] = total
            pltpu.sync_copy(h_v, h_hbm.at[local + i])

    return run(expert, xn2d, table)


def _peer_act_kernel(h_ref, gate_ref, rep_ref, o_ref):
    h = h_ref[...]
    w = gate_ref[...] * (0.5 * h * (1.0 + lax.erf(h * (1.0 / math.sqrt(2.0)))))
    hi, lo = _split_bf16(w)
    o_ref[...] = (jnp.dot(hi, rep_ref[...], preferred_element_type=F32)
                  + jnp.dot(lo, rep_ref[...], preferred_element_type=F32))


def _peer_act(h, gate, rep, *, first, tm):
    n = h.shape[0]
    off = first // tm
    return pl.pallas_call(
        _peer_act_kernel,
        grid=(n // tm,),
        in_specs=[pl.BlockSpec((tm, PEER_SLOTS), lambda i: (i, 0)),
                  pl.BlockSpec((tm, PEER_SLOTS), lambda i: (i + off, 0)),
                  _resident(rep.shape)],
        out_specs=pl.BlockSpec((tm, PEER_SLOTS * SC_LANES), lambda i: (i, 0)),
        out_shape=jax.ShapeDtypeStruct((n, PEER_SLOTS * SC_LANES), F32),
        compiler_params=_params(1),
        name="peer_act",
    )(h, gate, rep)


def _sc_peer_down(expert, w_lanes, x2d, table, *, first, n_tokens):
    t, d = x2d.shape
    per_worker = n_tokens // SC_WORKERS
    words = d // 2
    mesh = plsc.VectorSubcoreMesh(core_axis_name="c", subcore_axis_name="s")

    @functools.partial(
        pl.kernel, mesh=mesh,
        out_type=jax.ShapeDtypeStruct((t, d), F32),
        scratch_types=[
            pltpu.VMEM((PEER_SLOTS,), jnp.int32),
            pltpu.VMEM((PEER_SLOTS * SC_LANES,), F32),
            pltpu.VMEM((d,), F32),
            pltpu.VMEM((2, SC_HALF, words), jnp.int32),
            pltpu.SemaphoreType.DMA,
            pltpu.SemaphoreType.DMA,
        ],
        compiler_params=pltpu.CompilerParams(needs_layout_passes=False),
        name="sc_peer_down",
    )
    def run(expert_hbm, w_hbm, x_hbm, table_hbm, out_hbm, idx_v, w_v, out_v, rows_v, sem0, sem1):
        worker = lax.axis_index("s") * 2 + lax.axis_index("c")
        local = worker * per_worker
        base = first + local

        def accumulate(h):
            @pl.loop(0, SC_HALF, step=SC_ROWS_PER_STEP)
            def _(j0):
                weights = [w_v[pl.ds((h * SC_HALF + j0 + r) * SC_LANES, SC_LANES)]
                           for r in range(SC_ROWS_PER_STEP)]
                for g in range(words // SC_LANES // SC_CHUNKS):
                    loaded = [[rows_v[h, j0 + r, pl.ds((g * SC_CHUNKS + q) * SC_LANES, SC_LANES)]
                               for q in range(SC_CHUNKS)] for r in range(SC_ROWS_PER_STEP)]
                    for q in range(SC_CHUNKS):
                        hi_sum = lo_sum = None
                        for r in range(SC_ROWS_PER_STEP):
                            word = loaded[r][q]
                            hi = weights[r] * lax.bitcast_convert_type(word & jnp.int32(-65536), F32)
                            lo = weights[r] * lax.bitcast_convert_type(word << 16, F32)
                            hi_sum = hi if hi_sum is None else hi_sum + hi
                            lo_sum = lo if lo_sum is None else lo_sum + lo
                        col = (g * SC_CHUNKS + q) * 2 * SC_LANES
                        plsc.addupdate(out_v.at[pl.ds(col, SC_LANES)], hi_sum)
                        plsc.addupdate(out_v.at[pl.ds(col + SC_LANES, SC_LANES)], lo_sum)

        @pl.loop(0, per_worker)
        def _(i):
            tok = base + i
            pltpu.sync_copy(expert_hbm.at[tok], idx_v)
            pltpu.sync_copy(w_hbm.at[local + i], w_v)
            pltpu.sync_copy(x_hbm.at[tok], out_v)
            first_half = pltpu.async_copy(table_hbm.at[idx_v.at[pl.ds(0, SC_HALF)]],
                                          rows_v.at[0], sem0)
            second_half = pltpu.async_copy(table_hbm.at[idx_v.at[pl.ds(SC_HALF, SC_HALF)]],
                                           rows_v.at[1], sem1)
            first_half.wait()
            accumulate(0)
            second_half.wait()
            accumulate(1)
            pltpu.sync_copy(out_v, out_hbm.at[tok])

    return run(expert, w_lanes, x2d, table)


def _peer_constants():
    col = jnp.arange(DOWN_K)
    src = ((col // SUBLANES) % 2) * PEER_SLOTS + col // PAIR_ROWS
    spread = (jnp.arange(2 * PEER_SLOTS)[:, None] == src[None, :]).astype(BF16)
    diag = (col[None, :] % SUBLANES == jnp.arange(SUBLANES)[:, None]).astype(F32)
    lane_repeat = jnp.repeat(jnp.eye(PEER_SLOTS, dtype=BF16), SC_LANES, axis=1)
    return spread, diag, lane_repeat


def _pair_slabs(v):
    e, d = v.shape
    return v.astype(BF16).reshape(e // 2, 2 * d // LANES, LANES)


def _peer(x2d, g, wq_bf16, keys_bf16, u_packed, u_words, v_slabs, v_words, consts, *, tm_route,
          tm_expert, n_sparsecore):
    t, d = x2d.shape
    spread, diag, lane_repeat = consts
    n_tc = t - n_sparsecore
    xn, pair, parity, gate = _peer_route(x2d, g, wq_bf16, keys_bf16, tm=tm_route)
    pair, parity, gate = pair.T, parity.T, gate.T
    xn3 = xn.reshape(t, d // LANES, LANES)
    wexp = _peer_up(pair, parity, xn3, u_packed, gate, spread, tm=tm_expert, n_tokens=n_tc)
    out3 = _peer_down(pair, wexp, diag, x2d.reshape(t, d // LANES, LANES), v_slabs, tm=tm_expert,
                      n_tokens=n_tc)
    if not n_sparsecore:
        return out3.reshape(t, d)
    expert = pair * 2 + parity
    h_sc = _sc_peer_up(expert, xn, u_words, first=n_tc, n_tokens=n_sparsecore)
    w_lanes = _peer_act(h_sc, gate, lane_repeat, first=n_tc, tm=tm_expert)
    out = _sc_peer_down(expert, w_lanes, x2d, v_words, first=n_tc, n_tokens=n_sparsecore)
    return lax.dynamic_update_slice(out, out3.reshape(n_tc, d), (0, 0))


def _final_norm_kernel(x_ref, g_ref, o_ref):
    o_ref[...] = _rms_scale(x_ref[...], g_ref[...])


def _final_norm(x2d, g, *, tm):
    t, d = x2d.shape
    return pl.pallas_call(
        _final_norm_kernel,
        grid=(t // tm,),
        in_specs=[pl.BlockSpec((tm, d), lambda i: (i, 0)), _resident((1, d))],
        out_specs=pl.BlockSpec((tm, d), lambda i: (i, 0)),
        out_shape=jax.ShapeDtypeStruct((t, d), F32),
        compiler_params=_params(1),
        name="final_norm",
    )(x2d, g.reshape(1, d))


def _tiles(seq):
    return dict(tm_proj=min(256, seq), tm_out=min(512, seq), tq=min(512, seq), tk=min(512, seq),
                tm_route=LANES, tm_expert=LANES, tm_norm=min(512, seq))


SC_SHARE = 0.3125


def _sparsecore_tokens(t, tm):
    unit = tm * SC_WORKERS // math.gcd(tm, SC_WORKERS)
    return int(t * SC_SHARE) // unit * unit


def _to_residue_major(a2d, batch, seq, dilation, c0, c1):
    cols = a2d[:, c0:c1].reshape(batch, seq // dilation, dilation, c1 - c0)
    return cols.transpose(0, 2, 1, 3)


def _from_residue_major(a4):
    b, d, length, c = a4.shape
    return a4.transpose(0, 2, 1, 3).reshape(b * d * length, c)


def kernel(x, mem, norm_mix, a_w_in, a_lambda, a_subln, b_w_in, norm_mem, w_mem_kv, w_out,
           norm_ffn, peer_wq, peer_keys, peer_u, peer_v, shared_norm, shared_w_kv, final_norm):
    batch, seq, d = x.shape
    mem_tokens = mem.shape[1]
    depth = norm_mix.shape[0]
    n_a = a_w_in.shape[0]
    t = batch * seq
    tiles = _tiles(seq)
    rope = _rope_tables(seq)
    scale = HEAD_DIM ** -0.5

    diff_qk = DIFF_HEADS * 2 * HEAD_DIM
    dil_qk = len(DIL_GROUPS) * DIL_GROUP_WIDTH
    peer_consts = _peer_constants()
    expand = jnp.repeat(jnp.eye(LANES, DIL_HEADS, dtype=BF16), HEAD_DIM, axis=1)

    x2d = x.reshape(t, d)
    mem2d = mem.reshape(batch * mem_tokens, d)
    shared = None

    for l in range(depth):
        memkv = _proj(mem2d, norm_mem[l], w_mem_kv[l].astype(BF16), tm=mem_tokens)
        w_o = w_out[l].astype(BF16)
        if l < n_a:
            col_scale = jnp.concatenate([jnp.full((diff_qk,), scale, F32),
                                         jnp.ones((a_w_in.shape[2] - diff_qk - MEM_Q_WIDTH,), F32),
                                         jnp.full((MEM_Q_WIDTH,), scale, F32)])
            w_in = (a_w_in[l] * col_scale).astype(BF16)
            proj = _proj(x2d, norm_mix[l], w_in, tm=tiles["tm_proj"], n_rope=2 * diff_qk,
                         rope=rope, seq=seq)
            lambda_init = 0.8 - 0.6 * math.exp(-0.3 * l)
            mix = _diff_attention(proj, a_lambda[l], a_subln[l], batch=batch, seq=seq,
                                  lambda_init=lambda_init, tq=tiles["tq"], tk=tiles["tk"])
            x2d = _outproj(x2d, proj, memkv, w_o, seq=seq, mem_tokens=mem_tokens,
                           tm=tiles["tm_out"], mix=mix)
        else:
            w_in = (b_w_in[l - n_a] * scale).astype(BF16)
            proj = _proj(x2d, norm_mix[l], w_in, tm=tiles["tm_proj"], n_rope=dil_qk,
                         rope=rope, seq=seq)
            outs, stats = [], []
            for gi, (window, dilation) in enumerate(DIL_GROUPS):
                c0 = gi * DIL_GROUP_WIDTH
                blk0 = c0 // LANES
                if dilation == 1:
                    q4, q_off = proj.reshape(batch, 1, seq, proj.shape[1]), blk0
                else:
                    q4, q_off = _to_residue_major(proj, batch, seq, dilation,
                                                  c0, c0 + DIL_GROUP_WIDTH), 0
                k4, k_off, v4, v_off = shared[gi]
                o4, st4 = _dilated_group(q4, k4, v4, q_off=q_off, k_off=k_off, v_off=v_off,
                                         n_steps=window // dilation)
                outs.append(_from_residue_major(o4))
                stats.append(_from_residue_major(st4))
            x2d = _outproj(x2d, proj, memkv, w_o, seq=seq, mem_tokens=mem_tokens,
                           tm=tiles["tm_out"], group_out=outs, group_stats=stats, expand=expand)

        x2d = _peer(x2d, norm_ffn[l], peer_wq[l].astype(BF16),
                    peer_keys[l].reshape(2 * PEER_HEADS, PEER_N_KEYS, PEER_HALF).astype(BF16),
                    _pack_expert_table(peer_u[l]), _sc_word_table(peer_u[l]),
                    _pair_slabs(peer_v[l]), _sc_word_table(peer_v[l]), peer_consts,
                    tm_route=tiles["tm_route"], tm_expert=tiles["tm_expert"],
                    n_sparsecore=_sparsecore_tokens(t, tiles["tm_expert"]))

        if l == n_a - 1:
            kv = _proj(x2d, shared_norm, shared_w_kv.astype(BF16), tm=tiles["tm_proj"],
                       n_rope=dil_qk, rope=rope, seq=seq)
            shared = []
            for gi, (window, dilation) in enumerate(DIL_GROUPS):
                c0 = gi * DIL_GROUP_WIDTH
                if dilation == 1:
                    kv4 = kv.reshape(batch, 1, seq, kv.shape[1])
                    shared.append((kv4, c0 // LANES, kv4, (dil_qk + c0) // LANES))
                else:
                    k4 = _to_residue_major(kv, batch, seq, dilation, c0, c0 + DIL_GROUP_WIDTH)
                    v4 = _to_residue_major(kv, batch, seq, dilation, dil_qk + c0,
                                           dil_qk + c0 + DIL_GROUP_WIDTH)
                    shared.append((k4, 0, v4, 0))

    return _final_norm(x2d, final_norm, tm=tiles["tm_norm"]).reshape(batch, seq, d)
```

```python
import functools
import math

import jax
import jax.numpy as jnp
from jax import lax
from jax.experimental import pallas as pl
from jax.experimental.pallas import tpu as pltpu
from jax.experimental.pallas import tpu_sc as plsc

F32 = jnp.float32
BF16 = jnp.bfloat16

HEAD_DIM = 64
ROPE_DIMS = HEAD_DIM // 4
ROPE_HALF = ROPE_DIMS // 2
ROPE_THETA = 500000.0
NORM_EPS = 1e-5
DIFF_HEADS = 6
MEM_HEADS = 4
MEM_Q_WIDTH = MEM_HEADS * HEAD_DIM
DIL_GROUPS = ((128, 1), (512, 4), (2048, 16))
DIL_HEADS = 12
DIL_BLOCK = 128
DIL_GROUP_WIDTH = DIL_HEADS * HEAD_DIM
PEER_HEADS = 8
PEER_N_KEYS = 128
PEER_TOPK = 16
PEER_HALF = 128
PEER_SLOTS = PEER_HEADS * PEER_TOPK

LANES = 128
SUBLANES = 8
VMEM_LIMIT_BYTES = 56 * 1024 * 1024

_NT = (((1,), (1,)), ((), ()))


def _params(n_axes):
    return pltpu.CompilerParams(dimension_semantics=("arbitrary",) * n_axes,
                                vmem_limit_bytes=VMEM_LIMIT_BYTES)


def _resident(shape):
    zeros = (0,) * len(shape)
    return pl.BlockSpec(shape, lambda *_: zeros, pipeline_mode=pl.Buffered(1))


def _rms_scale(x, g):
    ms = jnp.mean(x * x, axis=-1, keepdims=True)
    return x * lax.rsqrt(ms + NORM_EPS) * g


def _rope_tables(seq):
    inv = ROPE_THETA ** (-jnp.arange(0, ROPE_DIMS, 2, dtype=F32) / ROPE_DIMS)
    ang = jnp.arange(seq, dtype=F32)[:, None] * inv[None, :]
    cos, sin = jnp.cos(ang), jnp.sin(ang)
    ones = jnp.ones((seq, HEAD_DIM - ROPE_DIMS), F32)
    zeros = jnp.zeros((seq, HEAD_DIM - ROPE_DIMS), F32)
    zh = jnp.zeros((seq, ROPE_HALF), F32)
    c = jnp.concatenate([cos, cos, ones], axis=1)
    sa = jnp.concatenate([-sin, zh, zeros], axis=1)
    sb = jnp.concatenate([zh, sin, zeros], axis=1)
    rep = LANES // HEAD_DIM
    return tuple(jnp.tile(t, (1, rep)) for t in (c, sa, sb))


def _proj_kernel(*refs, n_rope, chunk):
    if n_rope:
        x_ref, g_ref, w_ref, c_ref, sa_ref, sb_ref, o_ref = refs
    else:
        x_ref, g_ref, w_ref, o_ref = refs
    y = _rms_scale(x_ref[...], g_ref[...]).astype(BF16)
    n = o_ref.shape[1]
    for c0 in range(0, n, chunk):
        acc = jnp.dot(y, w_ref[:, c0:c0 + chunk], preferred_element_type=F32)
        if c0 < n_rope:
            for k0 in range(0, chunk, LANES):
                a = acc[:, k0:k0 + LANES]
                a = (a * c_ref[...]
                     + pltpu.roll(a, LANES - ROPE_HALF, 1) * sa_ref[...]
                     + pltpu.roll(a, ROPE_HALF, 1) * sb_ref[...])
                o_ref[:, c0 + k0:c0 + k0 + LANES] = a.astype(o_ref.dtype)
        else:
            o_ref[:, c0:c0 + chunk] = acc.astype(o_ref.dtype)


def _proj(x2d, g, w_bf16, *, tm, n_rope=0, rope=None, seq=None):
    t, d = x2d.shape
    n = w_bf16.shape[1]
    chunk = 256
    assert t % tm == 0 and n % chunk == 0 and n_rope % chunk == 0
    in_specs = [pl.BlockSpec((tm, d), lambda i: (i, 0)),
                _resident((1, d)),
                _resident((d, n))]
    args = [x2d, g.reshape(1, d), w_bf16]
    if n_rope:
        nblk = seq // tm
        spec = pl.BlockSpec((tm, LANES), lambda i: (i % nblk, 0))
        in_specs += [spec, spec, spec]
        args += list(rope)
    return pl.pallas_call(
        functools.partial(_proj_kernel, n_rope=n_rope, chunk=chunk),
        grid=(t // tm,),
        in_specs=in_specs,
        out_specs=pl.BlockSpec((tm, n), lambda i: (i, 0)),
        out_shape=jax.ShapeDtypeStruct((t, n), BF16),
        compiler_params=_params(1),
        name="norm_proj",
    )(*args)


def _online_softmax_step(q, k, v, carry, mask):
    m, l, acc = carry
    s = lax.dot_general(q, k, _NT, preferred_element_type=F32)
    if mask is not None:
        s = jnp.where(mask, s, -jnp.inf)
    m_new = jnp.maximum(m, jnp.max(s, axis=-1, keepdims=True))
    alpha = jnp.exp(m - m_new)
    p = jnp.exp(s - m_new)
    l = alpha * l + jnp.sum(p, axis=-1, keepdims=True)
    acc = alpha * acc + jnp.dot(p.astype(BF16), v, preferred_element_type=F32)
    return m_new, l, acc


def _diff_attn_kernel(q_ref, k_ref, v_ref, lp_ref, g_ref, o_ref, *, tq, tk, lambda_init):
    qi = pl.program_id(2)
    q = q_ref[...]
    lane = lax.broadcasted_iota(jnp.int32, q.shape, 1)
    zero = jnp.zeros_like(q)
    qa = jnp.where(lane < HEAD_DIM, q, zero)
    qb = jnp.where(lane >= HEAD_DIM, q, zero)

    def init():
        return (jnp.full((tq, 1), -jnp.inf, F32), jnp.zeros((tq, 1), F32),
                jnp.zeros((tq, 2 * HEAD_DIM), F32))

    def body(j, carry):
        c1, c2 = carry
        k = k_ref[pl.ds(pl.multiple_of(j * tk, tk), tk), :]
        v = v_ref[pl.ds(pl.multiple_of(j * tk, tk), tk), :]
        return (_online_softmax_step(qa, k, v, c1, None),
                _online_softmax_step(qb, k, v, c2, None))

    n_full = (qi * tq) // tk
    c1, c2 = lax.fori_loop(0, n_full, body, (init(), init()))
    row = qi * tq + lax.broadcasted_iota(jnp.int32, (tq, tk), 0)
    col = n_full * tk + lax.broadcasted_iota(jnp.int32, (tq, tk), 1)
    causal = col <= row
    start = pl.multiple_of(n_full * tk, tk)
    k = k_ref[pl.ds(start, tk), :]
    v = v_ref[pl.ds(start, tk), :]
    m1, l1, a1 = _online_softmax_step(qa, k, v, c1, causal)
    m2, l2, a2 = _online_softmax_step(qb, k, v, c2, causal)

    lp = lp_ref[...]
    lam = (jnp.exp(jnp.sum(lp[0:1] * lp[1:2], axis=-1, keepdims=True))
           - jnp.exp(jnp.sum(lp[2:3] * lp[3:4], axis=-1, keepdims=True)) + lambda_init)
    o = a1 / l1 - lam * (a2 / l2)
    o = _rms_scale(o, g_ref[...]) * (1.0 - lambda_init)
    o_ref[...] = o.astype(o_ref.dtype)


def _diff_attention(proj, lp, subln, *, batch, seq, lambda_init, tq, tk):
    assert tk % tq == 0 and seq % tk == 0
    nq = seq // tq
    width = 2 * HEAD_DIM
    return pl.pallas_call(
        functools.partial(_diff_attn_kernel, tq=tq, tk=tk, lambda_init=lambda_init),
        grid=(batch, DIFF_HEADS, nq),
        in_specs=[
            pl.BlockSpec((tq, width), lambda b, h, i: (b * nq + i, h)),
            pl.BlockSpec((seq, width), lambda b, h, i: (b, DIFF_HEADS + h)),
            pl.BlockSpec((seq, width), lambda b, h, i: (b, 2 * DIFF_HEADS + h)),
            _resident((4, HEAD_DIM)),
            _resident((1, width)),
        ],
        out_specs=pl.BlockSpec((tq, width), lambda b, h, i: (b * nq + i, h)),
        out_shape=jax.ShapeDtypeStruct((batch * seq, DIFF_HEADS * width), BF16),
        compiler_params=_params(3),
        name="diff_attention",
    )(proj, proj, proj, lp, subln.reshape(1, width))


def _dilated_kernel(q_ref, kc_ref, kp_ref, vc_ref, vp_ref, o_ref, st_ref, *, tq, n_steps):
    n = pl.program_id(2)
    hp = pl.program_id(3)
    blk = DIL_BLOCK

    @pl.when(hp == 0)
    def _():
        st_ref[...] = jnp.zeros_like(st_ref)

    qrow = lax.broadcasted_iota(jnp.int32, (blk, 2 * blk), 0)
    kcol = lax.broadcasted_iota(jnp.int32, (blk, 2 * blk), 1)
    dist = qrow + blk - kcol
    band = (dist >= 0) & (dist <= n_steps)
    lane_q = lax.broadcasted_iota(jnp.int32, (blk, LANES), 1)
    lane_s = lax.broadcasted_iota(jnp.int32, (blk, LANES), 1)

    for i in range(tq // blk):
        rows = slice(i * blk, (i + 1) * blk)
        q = q_ref[rows, :]
        if i == 0:
            k_prev, v_prev = kp_ref[...], vp_ref[...]
            valid = band & ((n > 0) | (kcol >= blk))
        else:
            prev = slice((i - 1) * blk, i * blk)
            k_prev, v_prev = kc_ref[prev, :], vc_ref[prev, :]
            valid = band
        keys = jnp.concatenate([k_prev, kc_ref[rows, :]], axis=0)
        vals = jnp.concatenate([v_prev, vc_ref[rows, :]], axis=0)
        out = jnp.zeros((blk, LANES), F32)
        stats = st_ref[rows, :]
        for e in range(LANES // HEAD_DIM):
            head_lanes = (lane_q // HEAD_DIM) == e
            qe = jnp.where(head_lanes, q, jnp.zeros_like(q))
            s = lax.dot_general(qe, keys, _NT, preferred_element_type=F32)
            s = jnp.where(valid, s, -jnp.inf)
            m = jnp.max(s, axis=-1, keepdims=True)
            p = jnp.exp(s - m)
            den = jnp.sum(p, axis=-1, keepdims=True)
            oe = jnp.dot((p / den).astype(BF16), vals, preferred_element_type=F32)
            out = jnp.where(head_lanes, oe, out)
            lse = m + jnp.log(den)
            stats = jnp.where(lane_s == hp * (LANES // HEAD_DIM) + e, lse, stats)
        o_ref[rows, :] = out.astype(o_ref.dtype)
        st_ref[rows, :] = stats


def _dilated_group(q4, k4, v4, *, q_off, k_off, v_off, n_steps):
    b, d, length, _ = q4.shape
    tq = min(512, length)
    assert length % tq == 0 and tq % DIL_BLOCK == 0
    sub = tq // DIL_BLOCK
    pairs = DIL_GROUP_WIDTH // LANES

    def cur(off):
        return pl.BlockSpec((None, None, tq, LANES), lambda bi, r, n, h: (bi, r, n, off + h))

    def prev(off):
        return pl.BlockSpec((None, None, DIL_BLOCK, LANES),
                            lambda bi, r, n, h: (bi, r, jnp.maximum(n * sub - 1, 0), off + h))

    return pl.pallas_call(
        functools.partial(_dilated_kernel, tq=tq, n_steps=n_steps),
        grid=(b, d, length // tq, pairs),
        in_specs=[cur(q_off), cur(k_off), prev(k_off), cur(v_off), prev(v_off)],
        out_specs=[pl.BlockSpec((None, None, tq, LANES), lambda bi, r, n, h: (bi, r, n, h)),
                   pl.BlockSpec((None, None, tq, LANES), lambda bi, r, n, h: (bi, r, n, 0))],
        out_shape=[jax.ShapeDtypeStruct((b, d, length, DIL_GROUP_WIDTH), BF16),
                   jax.ShapeDtypeStruct((b, d, length, LANES), F32)],
        compiler_params=_params(4),
        name="dilated_attention",
    )(q4, k4, k4, v4, v4)


def _memory_attention(q, k, v):
    lane = lax.broadcasted_iota(jnp.int32, q.shape, 1)
    out = jnp.zeros(q.shape, F32)
    for h in range(MEM_HEADS):
        head_lanes = (lane // HEAD_DIM) == h
        qh = jnp.where(head_lanes, q, jnp.zeros_like(q))
        s = lax.dot_general(qh, k, _NT, preferred_element_type=F32)
        m = jnp.max(s, axis=-1, keepdims=True)
        p = jnp.exp(s - m)
        den = jnp.sum(p, axis=-1, keepdims=True)
        oh = jnp.dot((p / den).astype(BF16), v, preferred_element_type=F32)
        out = jnp.where(head_lanes, oh, out)
    return out


def _split_bf16(x):
    hi = x.astype(BF16)
    lo = (x - hi.astype(F32)).astype(BF16)
    return hi, lo


def _outproj_kernel(*refs, n_groups):
    if n_groups:
        x_ref, qm_ref, mk_ref, mv_ref, w_ref = refs[:5]
        o_refs = refs[5:5 + n_groups]
        st_refs = refs[5 + n_groups:5 + 2 * n_groups]
        ex_ref, out_ref = refs[5 + 2 * n_groups:]
        lses = [r[...] for r in st_refs]
        top = functools.reduce(jnp.maximum, lses)
        es = [jnp.exp(l - top) for l in lses]
        den = functools.reduce(lambda a, b: a + b, es)
        mix = None
        for e, o_ref in zip(es, o_refs):
            hi, lo = _split_bf16(e / den)
            wide = (jnp.dot(hi, ex_ref[...], preferred_element_type=F32)
                    + jnp.dot(lo, ex_ref[...], preferred_element_type=F32))
            term = wide * o_ref[...].astype(F32)
            mix = term if mix is None else mix + term
        mix = mix.astype(BF16)
    else:
        x_ref, qm_ref, mk_ref, mv_ref, w_ref, mix_ref, out_ref = refs
        mix = mix_ref[...]
    mo = _memory_attention(qm_ref[...], mk_ref[...], mv_ref[...]).astype(BF16)
    k_mix = mix.shape[1]
    acc = jnp.dot(mix, w_ref[:k_mix, :], preferred_element_type=F32)
    acc += jnp.dot(mo, w_ref[k_mix:, :], preferred_element_type=F32)
    out_ref[...] = x_ref[...] + acc


def _outproj(x2d, proj, memkv, w_bf16, *, seq, mem_tokens, tm, mix=None, group_out=None,
             group_stats=None, expand=None):
    t, d = x2d.shape
    per_batch = seq // tm
    qm_block = (proj.shape[1] - MEM_Q_WIDTH) // MEM_Q_WIDTH
    row = lambda i: (i, 0)
    in_specs = [pl.BlockSpec((tm, d), row),
                pl.BlockSpec((tm, MEM_Q_WIDTH), lambda i: (i, qm_block)),
                pl.BlockSpec((mem_tokens, MEM_Q_WIDTH), lambda i: (i // per_batch, 0)),
                pl.BlockSpec((mem_tokens, MEM_Q_WIDTH), lambda i: (i // per_batch, 1)),
                _resident(w_bf16.shape)]
    args = [x2d, proj, memkv, memkv, w_bf16]
    if mix is not None:
        n_groups = 0
        in_specs.append(pl.BlockSpec((tm, mix.shape[1]), row))
        args.append(mix)
    else:
        n_groups = len(group_out)
        in_specs += [pl.BlockSpec((tm, DIL_GROUP_WIDTH), row)] * n_groups
        in_specs += [pl.BlockSpec((tm, LANES), row)] * n_groups
        in_specs.append(_resident(expand.shape))
        args += list(group_out) + list(group_stats) + [expand]
    return pl.pallas_call(
        functools.partial(_outproj_kernel, n_groups=n_groups),
        grid=(t // tm,),
        in_specs=in_specs,
        out_specs=pl.BlockSpec((tm, d), row),
        out_shape=jax.ShapeDtypeStruct((t, d), F32),
        compiler_params=_params(1),
        name="mix_outproj",
    )(*args)


def _topk_rows(s, k):
    n = s.shape[0]
    row = lax.broadcasted_iota(jnp.int32, s.shape, 0)
    vals, idxs = [], []
    for _ in range(k):
        m = jnp.max(s, axis=0, keepdims=True)
        i = jnp.min(jnp.where(s == m, row, n), axis=0, keepdims=True)
        vals.append(m)
        idxs.append(i)
        s = jnp.where(row == i, -jnp.inf, s)
    return jnp.concatenate(vals, axis=0), jnp.concatenate(idxs, axis=0)


def _select_rows(table, idx):
    row = lax.broadcasted_iota(jnp.int32, table.shape, 0)
    return jnp.sum(jnp.where(row == idx, table, jnp.zeros_like(table)), axis=0, keepdims=True)


def _peer_route_kernel(x_ref, g_ref, wq_ref, keys_ref, xn_ref, pair_ref, parity_ref, gate_ref):
    xn = _rms_scale(x_ref[...], g_ref[...])
    xn_ref[...] = xn
    q = jnp.dot(xn.astype(BF16), wq_ref[...], preferred_element_type=F32).astype(BF16)
    kk = PEER_TOPK
    for h in range(PEER_HEADS):
        tops = []
        for half in range(2):
            c0 = (2 * h + half) * PEER_HALF
            s = lax.dot_general(keys_ref[2 * h + half], q[:, c0:c0 + PEER_HALF], _NT,
                                preferred_element_type=F32)
            tops.append(_topk_rows(s, kk))
        (s1, i1), (s2, i2) = tops
        half = kk // 2
        sub = lax.broadcasted_iota(jnp.int32, (half, s1.shape[1]), 0)
        blocks = [s1[0:1] + s2]
        experts = [i1[0:1] * PEER_N_KEYS + i2]
        for a in range(1, half):
            blocks.append(jnp.where(sub < kk // (a + 1), s1[a:a + 1] + s2[:half], -jnp.inf))
            experts.append(i1[a:a + 1] * PEER_N_KEYS + i2[:half])
        blocks.append(s1[half:] + s2[0:1])
        experts.append(i1[half:] * PEER_N_KEYS + i2[0:1])
        cand = jnp.concatenate(blocks, axis=0)
        cand_expert = jnp.concatenate(experts, axis=0)
        top_s, top_j = _topk_rows(cand, kk)
        expert = jnp.concatenate([_select_rows(cand_expert, top_j[r:r + 1]) for r in range(kk)],
                                 axis=0)
        e = jnp.exp(top_s - top_s[0:1])
        gate = e / jnp.sum(e, axis=0, keepdims=True)
        rows = slice(h * kk, (h + 1) * kk)
        pair_ref[rows, :] = expert >> 1
        parity_ref[rows, :] = expert & 1
        gate_ref[rows, :] = gate


def _peer_route(x2d, g, wq_bf16, keys_bf16, *, tm):
    t, d = x2d.shape
    col = lambda i: (0, i)
    slot_shape = jax.ShapeDtypeStruct((PEER_SLOTS, t), jnp.int32)
    return pl.pallas_call(
        _peer_route_kernel,
        grid=(t // tm,),
        in_specs=[pl.BlockSpec((tm, d), lambda i: (i, 0)),
                  _resident((1, d)),
                  _resident(wq_bf16.shape),
                  _resident(keys_bf16.shape)],
        out_specs=[pl.BlockSpec((tm, d), lambda i: (i, 0)),
                   pl.BlockSpec((PEER_SLOTS, tm), col),
                   pl.BlockSpec((PEER_SLOTS, tm), col),
                   pl.BlockSpec((PEER_SLOTS, tm), col)],
        out_shape=[jax.ShapeDtypeStruct((t, d), F32), slot_shape, slot_shape,
                   jax.ShapeDtypeStruct((PEER_SLOTS, t), F32)],
        compiler_params=_params(1),
        name="peer_route",
    )(x2d, g.reshape(1, d), wq_bf16, keys_bf16)


_HI_MASK = 0xFFFF0000


def _pack_expert_table(u):
    e, d = u.shape
    bits = lax.bitcast_convert_type(u.astype(BF16), jnp.uint16).astype(jnp.uint32)
    bits = bits.reshape(e // 2, 2, d)
    packed = (bits[:, 0] << 16) | bits[:, 1]
    return packed.reshape(e // 2, d // LANES, LANES)


_BIT_REVERSED = (0, 4, 2, 6, 1, 5, 3, 7)


def _bf16_pair_words(x):
    bits = pltpu.bitcast(x, jnp.uint32)
    top = (bits + jnp.uint32(0x7FFF) + ((bits >> 16) & jnp.uint32(1))) >> 16
    return (top << 16) | top


def _packed_add(a, b):
    return pltpu.bitcast(pltpu.bitcast(a, BF16) + pltpu.bitcast(b, BF16), jnp.uint32)


def _sublane_sums(words, sub):
    level = [words[i] for i in _BIT_REVERSED]
    for k in (4, 2, 1):
        low = (sub & k) == 0
        merged = []
        for a, b in zip(level[0::2], level[1::2]):
            if k == 4:
                merged.append(_packed_add(jnp.where(low, a, b),
                                          pltpu.roll(jnp.where(low, b, a), k, 0)))
            else:
                merged.append(jnp.where(low,
                                        _packed_add(a, pltpu.roll(a, SUBLANES - k, 0)),
                                        _packed_add(b, pltpu.roll(b, k, 0))))
        level = merged
    return level[0]


def _peer_up_kernel(pair_ref, x_ref, tab_ref, gate_ref, par_ref, spread_ref, wexp_ref, w2_ref):
    tm = x_ref.shape[0]
    sub = lax.broadcasted_iota(jnp.int32, (SUBLANES, LANES), 0)
    hi_mask = jnp.uint32(_HI_MASK)

    def token_sums(t, m):
        xx = pltpu.bitcast(_bf16_pair_words(x_ref[t]), BF16)
        even, odd = [], []
        for g in range(PEER_SLOTS // SUBLANES):
            words = []
            for i in range(SUBLANES):
                slab = pltpu.bitcast(tab_ref[pair_ref[t, g * SUBLANES + i]], BF16)
                words.append(pltpu.bitcast(slab * xx, jnp.uint32))
            q = _sublane_sums(words, sub)
            even.append(pltpu.bitcast(q & hi_mask, F32))
            odd.append(pltpu.bitcast(q << 16, F32))
        lane_parts = jnp.concatenate(even + odd, axis=0).astype(BF16)
        onehot = (sub == m).astype(BF16)
        return lax.dot_general(onehot, lane_parts, _NT, preferred_element_type=F32)

    def group(gi, _):
        base = pl.multiple_of(gi * SUBLANES, SUBLANES)
        h2 = token_sums(base, 0)
        for m in range(1, SUBLANES):
            h2 = h2 + token_sums(base + m, m)
        odd_slot = par_ref[pl.ds(base, SUBLANES), :] != 0
        h = jnp.where(odd_slot, h2[:, PEER_SLOTS:], h2[:, :PEER_SLOTS])
        act = 0.5 * h * (1.0 + lax.erf(h * (1.0 / math.sqrt(2.0))))
        w = gate_ref[pl.ds(base, SUBLANES), :] * act
        zero = jnp.zeros_like(w)
        w2_ref[pl.ds(base, SUBLANES), 0:PEER_SLOTS] = jnp.where(odd_slot, zero, w)
        w2_ref[pl.ds(base, SUBLANES), PEER_SLOTS:2 * PEER_SLOTS] = jnp.where(odd_slot, w, zero)
        return 0

    lax.fori_loop(0, tm // SUBLANES, group, 0)
    hi, lo = _split_bf16(w2_ref[...])
    wexp_ref[...] = (jnp.dot(hi, spread_ref[...], preferred_element_type=F32)
                     + jnp.dot(lo, spread_ref[...], preferred_element_type=F32))


PAIR_ROWS = 2 * SUBLANES
DOWN_K = PEER_SLOTS * PAIR_ROWS
DOWN_PAIRS_PER_STEP = 4


def _peer_down_kernel(pair_ref, wexp_ref, diag_ref, x_ref, tab_ref, o_ref):
    tm = x_ref.shape[0]

    def left_rows(t):
        row = wexp_ref[pl.ds(t, 1), :]
        hi, lo = _split_bf16(row * diag_ref[...])
        return [hi, lo]

    def token_pair(ta):
        tb = ta + 1
        rhs = jnp.concatenate(
            [jnp.concatenate([tab_ref[pair_ref[ta, j]], tab_ref[pair_ref[tb, j]]], axis=1)
             for j in range(PEER_SLOTS)], axis=0)
        lhs = jnp.concatenate(left_rows(ta) + left_rows(tb), axis=0)
        out = jnp.dot(lhs, rhs, preferred_element_type=F32)
        s = SUBLANES
        o_ref[ta] = x_ref[ta] + (out[0:s, :LANES] + out[s:2 * s, :LANES])
        o_ref[tb] = x_ref[tb] + (out[2 * s:3 * s, LANES:] + out[3 * s:4 * s, LANES:])

    def step(i, _):
        for k in range(DOWN_PAIRS_PER_STEP):
            token_pair(2 * (DOWN_PAIRS_PER_STEP * i + k))
        return 0

    lax.fori_loop(0, tm // (2 * DOWN_PAIRS_PER_STEP), step, 0)


def _smem_rows(tm):
    return pl.BlockSpec((tm, PEER_SLOTS), lambda i: (i, 0), memory_space=pltpu.SMEM)


def _peer_up(pair, parity, xn3, table, gate, spread, *, tm, n_tokens):
    rows = pl.BlockSpec((tm, PEER_SLOTS), lambda i: (i, 0))
    return pl.pallas_call(
        _peer_up_kernel,
        grid=(n_tokens // tm,),
        in_specs=[_smem_rows(tm),
                  pl.BlockSpec((tm, SUBLANES, LANES), lambda i: (i, 0, 0)),
                  _resident(table.shape), rows, rows, _resident(spread.shape)],
        out_specs=pl.BlockSpec((tm, DOWN_K), lambda i: (i, 0)),
        out_shape=jax.ShapeDtypeStruct((n_tokens, DOWN_K), F32),
        scratch_shapes=[pltpu.VMEM((tm, 2 * PEER_SLOTS), F32)],
        compiler_params=_params(1),
        name="peer_up",
    )(pair, xn3, table, gate, parity, spread)


def _peer_down(pair, wexp, diag, x3, table, *, tm, n_tokens):
    tile = pl.BlockSpec((tm, SUBLANES, LANES), lambda i: (i, 0, 0))
    return pl.pallas_call(
        _peer_down_kernel,
        grid=(n_tokens // tm,),
        in_specs=[_smem_rows(tm), pl.BlockSpec((tm, DOWN_K), lambda i: (i, 0)),
                  _resident(diag.shape), tile, _resident(table.shape)],
        out_specs=tile,
        out_shape=jax.ShapeDtypeStruct((n_tokens,) + x3.shape[1:], F32),
        compiler_params=_params(1),
        name="peer_down",
    )(pair, wexp, diag, x3, table)


SC_WORKERS = 32
SC_LANES = 16
SC_HALF = PEER_SLOTS // 2
SC_ROWS_PER_STEP = 4
SC_CHUNKS = 8


def _sc_word_table(v):
    e, d = v.shape
    bits = lax.bitcast_convert_type(v.astype(BF16), jnp.uint16).astype(jnp.uint32)
    blk = bits.reshape(e, d // (2 * SC_LANES), 2, SC_LANES)
    words = (blk[:, :, 0] << 16) | blk[:, :, 1]
    return lax.bitcast_convert_type(words.reshape(e, d // 2), jnp.int32)


def _sc_peer_up(expert, xn2d, table, *, first, n_tokens):
    t, d = xn2d.shape
    per_worker = n_tokens // SC_WORKERS
    words = d // 2
    mesh = plsc.VectorSubcoreMesh(core_axis_name="c", subcore_axis_name="s")

    @functools.partial(
        pl.kernel, mesh=mesh,
        out_type=jax.ShapeDtypeStruct((n_tokens, PEER_SLOTS), F32),
        scratch_types=[
            pltpu.VMEM((PEER_SLOTS,), jnp.int32),
            pltpu.VMEM((d,), F32),
            pltpu.VMEM((2, SC_HALF, words), jnp.int32),
            pltpu.VMEM((PEER_SLOTS * SC_LANES,), F32),
            pltpu.VMEM((PEER_SLOTS,), F32),
            pltpu.SemaphoreType.DMA,
            pltpu.SemaphoreType.DMA,
        ],
        compiler_params=pltpu.CompilerParams(needs_layout_passes=False),
        name="sc_peer_up",
    )
    def run(expert_hbm, x_hbm, table_hbm, h_hbm, idx_v, x_v, rows_v, part_v, h_v, sem0, sem1):
        worker = lax.axis_index("s") * 2 + lax.axis_index("c")
        local = worker * per_worker
        base = first + local
        lane = lax.iota(jnp.int32, SC_LANES)

        def partial_sums(h):
            @pl.loop(0, SC_HALF, step=SC_ROWS_PER_STEP)
            def _(j0):
                sums = [None] * SC_ROWS_PER_STEP
                for m in range(words // SC_LANES):
                    xa = x_v[pl.ds(2 * m * SC_LANES, SC_LANES)]
                    xb = x_v[pl.ds((2 * m + 1) * SC_LANES, SC_LANES)]
                    for r in range(SC_ROWS_PER_STEP):
                        word = rows_v[h, j0 + r, pl.ds(m * SC_LANES, SC_LANES)]
                        term = (xa * lax.bitcast_convert_type(word & jnp.int32(-65536), F32)
                                + xb * lax.bitcast_convert_type(word << 16, F32))
                        sums[r] = term if sums[r] is None else sums[r] + term
                for r in range(SC_ROWS_PER_STEP):
                    part_v[pl.ds((h * SC_HALF + j0 + r) * SC_LANES, SC_LANES)] = sums[r]

        @pl.loop(0, per_worker)
        def _(i):
            tok = base + i
            pltpu.sync_copy(expert_hbm.at[tok], idx_v)
            pltpu.sync_copy(x_hbm.at[tok], x_v)
            first_half = pltpu.async_copy(table_hbm.at[idx_v.at[pl.ds(0, SC_HALF)]],
                                          rows_v.at[0], sem0)
            second_half = pltpu.async_copy(table_hbm.at[idx_v.at[pl.ds(SC_HALF, SC_HALF)]],
                                           rows_v.at[1], sem1)
            first_half.wait()
            partial_sums(0)
            second_half.wait()
            partial_sums(1)
            for g in range(PEER_SLOTS // SC_LANES):
                total = None
                for k in range(SC_LANES):
                    column = plsc.load_gather(
                        part_v, [g * SC_LANES * SC_LANES + lane * SC_LANES + k])
                    total = column if total is None else total + column
                h_v[pl.ds(g * SC_LANES, SC_LANES)] = total
            pltpu.sync_copy(h_v, h_hbm.at[local + i])

    return run(expert, xn2d, table)


def _peer_act_kernel(h_ref, gate_ref, rep_ref, after_ref, o_ref):
    del after_ref
    h = h_ref[...]
    w = gate_ref[...] * (0.5 * h * (1.0 + lax.erf(h * (1.0 / math.sqrt(2.0)))))
    hi, lo = _split_bf16(w)
    o_ref[...] = (jnp.dot(hi, rep_ref[...], preferred_element_type=F32)
                  + jnp.dot(lo, rep_ref[...], preferred_element_type=F32))


def _peer_act(h, gate, rep, after, *, first, tm):
    n = h.shape[0]
    off = first // tm
    return pl.pallas_call(
        _peer_act_kernel,
        grid=(n // tm,),
        in_specs=[pl.BlockSpec((tm, PEER_SLOTS), lambda i: (i, 0)),
                  pl.BlockSpec((tm, PEER_SLOTS), lambda i: (i + off, 0)),
                  _resident(rep.shape),
                  pl.BlockSpec(memory_space=pl.ANY)],
        out_specs=pl.BlockSpec((tm, PEER_SLOTS * SC_LANES), lambda i: (i, 0)),
        out_shape=jax.ShapeDtypeStruct((n, PEER_SLOTS * SC_LANES), F32),
        compiler_params=_params(1),
        name="peer_act",
    )(h, gate, rep, after)


def _sc_peer_down(expert, w_lanes, x2d, table, *, first, n_tokens):
    t, d = x2d.shape
    per_worker = n_tokens // SC_WORKERS
    words = d // 2
    mesh = plsc.VectorSubcoreMesh(core_axis_name="c", subcore_axis_name="s")

    @functools.partial(
        pl.kernel, mesh=mesh,
        out_type=jax.ShapeDtypeStruct((t, d), F32),
        scratch_types=[
            pltpu.VMEM((PEER_SLOTS,), jnp.int32),
            pltpu.VMEM((PEER_SLOTS * SC_LANES,), F32),
            pltpu.VMEM((d,), F32),
            pltpu.VMEM((2, SC_HALF, words), jnp.int32),
            pltpu.SemaphoreType.DMA,
            pltpu.SemaphoreType.DMA,
        ],
        compiler_params=pltpu.CompilerParams(needs_layout_passes=False),
        name="sc_peer_down",
    )
    def run(expert_hbm, w_hbm, x_hbm, table_hbm, out_hbm, idx_v, w_v, out_v, rows_v, sem0, sem1):
        worker = lax.axis_index("s") * 2 + lax.axis_index("c")
        local = worker * per_worker
        base = first + local

        def accumulate(h):
            @pl.loop(0, SC_HALF, step=SC_ROWS_PER_STEP)
            def _(j0):
                weights = [w_v[pl.ds((h * SC_HALF + j0 + r) * SC_LANES, SC_LANES)]
                           for r in range(SC_ROWS_PER_STEP)]
                for g in range(words // SC_LANES // SC_CHUNKS):
                    loaded = [[rows_v[h, j0 + r, pl.ds((g * SC_CHUNKS + q) * SC_LANES, SC_LANES)]
                               for q in range(SC_CHUNKS)] for r in range(SC_ROWS_PER_STEP)]
                    for q in range(SC_CHUNKS):
                        hi_sum = lo_sum = None
                        for r in range(SC_ROWS_PER_STEP):
                            word = loaded[r][q]
                            hi = weights[r] * lax.bitcast_convert_type(word & jnp.int32(-65536), F32)
                            lo = weights[r] * lax.bitcast_convert_type(word << 16, F32)
                            hi_sum = hi if hi_sum is None else hi_sum + hi
                            lo_sum = lo if lo_sum is None else lo_sum + lo
                        col = (g * SC_CHUNKS + q) * 2 * SC_LANES
                        plsc.addupdate(out_v.at[pl.ds(col, SC_LANES)], hi_sum)
                        plsc.addupdate(out_v.at[pl.ds(col + SC_LANES, SC_LANES)], lo_sum)

        @pl.loop(0, per_worker)
        def _(i):
            tok = base + i
            pltpu.sync_copy(expert_hbm.at[tok], idx_v)
            pltpu.sync_copy(w_hbm.at[local + i], w_v)
            pltpu.sync_copy(x_hbm.at[tok], out_v)
            first_half = pltpu.async_copy(table_hbm.at[idx_v.at[pl.ds(0, SC_HALF)]],
                                          rows_v.at[0], sem0)
            second_half = pltpu.async_copy(table_hbm.at[idx_v.at[pl.ds(SC_HALF, SC_HALF)]],
                                           rows_v.at[1], sem1)
            first_half.wait()
            accumulate(0)
            second_half.wait()
            accumulate(1)
            pltpu.sync_copy(out_v, out_hbm.at[tok])

    return run(expert, w_lanes, x2d, table)


def _peer_constants():
    col = jnp.arange(DOWN_K)
    src = ((col // SUBLANES) % 2) * PEER_SLOTS + col // PAIR_ROWS
    spread = (jnp.arange(2 * PEER_SLOTS)[:, None] == src[None, :]).astype(BF16)
    diag = (col[None, :] % SUBLANES == jnp.arange(SUBLANES)[:, None]).astype(F32)
    lane_repeat = jnp.repeat(jnp.eye(PEER_SLOTS, dtype=BF16), SC_LANES, axis=1)
    return spread, diag, lane_repeat


def _pair_slabs(v):
    e, d = v.shape
    return v.astype(BF16).reshape(e // 2, 2 * d // LANES, LANES)


def _peer(x2d, g, wq_bf16, keys_bf16, u_packed, u_words, v_slabs, v_words, consts, *, tm_route,
          tm_expert, n_sparsecore):
    t, d = x2d.shape
    spread, diag, lane_repeat = consts
    n_tc = t - n_sparsecore
    xn, pair, parity, gate = _peer_route(x2d, g, wq_bf16, keys_bf16, tm=tm_route)
    pair, parity, gate = pair.T, parity.T, gate.T
    xn3 = xn.reshape(t, d // LANES, LANES)
    expert = pair * 2 + parity
    h_sc = _sc_peer_up(expert, xn, u_words, first=n_tc, n_tokens=n_sparsecore)
    wexp = _peer_up(pair, parity, xn3, u_packed, gate, spread, tm=tm_expert, n_tokens=n_tc)
    w_lanes = _peer_act(h_sc, gate, lane_repeat, wexp, first=n_tc, tm=tm_expert)
    out = _sc_peer_down(expert, w_lanes, x2d, v_words, first=n_tc, n_tokens=n_sparsecore)
    out3 = _peer_down(pair, wexp, diag, x2d.reshape(t, d // LANES, LANES), v_slabs, tm=tm_expert,
                      n_tokens=n_tc)
    return lax.dynamic_update_slice(out, out3.reshape(n_tc, d), (0, 0))


def _final_norm_kernel(x_ref, g_ref, o_ref):
    o_ref[...] = _rms_scale(x_ref[...], g_ref[...])


def _final_norm(x2d, g, *, tm):
    t, d = x2d.shape
    return pl.pallas_call(
        _final_norm_kernel,
        grid=(t // tm,),
        in_specs=[pl.BlockSpec((tm, d), lambda i: (i, 0)), _resident((1, d))],
        out_specs=pl.BlockSpec((tm, d), lambda i: (i, 0)),
        out_shape=jax.ShapeDtypeStruct((t, d), F32),
        compiler_params=_params(1),
        name="final_norm",
    )(x2d, g.reshape(1, d))


def _tiles(seq):
    return dict(tm_proj=min(256, seq), tm_out=min(512, seq), tq=min(512, seq), tk=min(512, seq),
                tm_route=LANES, tm_expert=LANES, tm_norm=min(512, seq))


SC_SHARE = 0.3125


def _sparsecore_tokens(t, tm):
    unit = tm * SC_WORKERS // math.gcd(tm, SC_WORKERS)
    return int(t * SC_SHARE) // unit * unit


def _to_residue_major(a2d, batch, seq, dilation, c0, c1):
    cols = a2d[:, c0:c1].reshape(batch, seq // dilation, dilation, c1 - c0)
    return cols.transpose(0, 2, 1, 3)


def _from_residue_major(a4):
    b, d, length, c = a4.shape
    return a4.transpose(0, 2, 1, 3).reshape(b * d * length, c)


def kernel(x, mem, norm_mix, a_w_in, a_lambda, a_subln, b_w_in, norm_mem, w_mem_kv, w_out,
           norm_ffn, peer_wq, peer_keys, peer_u, peer_v, shared_norm, shared_w_kv, final_norm):
    batch, seq, d = x.shape
    mem_tokens = mem.shape[1]
    depth = norm_mix.shape[0]
    n_a = a_w_in.shape[0]
    t = batch * seq
    tiles = _tiles(seq)
    rope = _rope_tables(seq)
    scale = HEAD_DIM ** -0.5

    diff_qk = DIFF_HEADS * 2 * HEAD_DIM
    dil_qk = len(DIL_GROUPS) * DIL_GROUP_WIDTH
    peer_consts = _peer_constants()
    expand = jnp.repeat(jnp.eye(LANES, DIL_HEADS, dtype=BF16), HEAD_DIM, axis=1)

    x2d = x.reshape(t, d)
    mem2d = mem.reshape(batch * mem_tokens, d)
    shared = None

    for l in range(depth):
        memkv = _proj(mem2d, norm_mem[l], w_mem_kv[l].astype(BF16), tm=mem_tokens)
        w_o = w_out[l].astype(BF16)
        if l < n_a:
            col_scale = jnp.concatenate([jnp.full((diff_qk,), scale, F32),
                                         jnp.ones((a_w_in.shape[2] - diff_qk - MEM_Q_WIDTH,), F32),
                                         jnp.full((MEM_Q_WIDTH,), scale, F32)])
            w_in = (a_w_in[l] * col_scale).astype(BF16)
            proj = _proj(x2d, norm_mix[l], w_in, tm=tiles["tm_proj"], n_rope=2 * diff_qk,
                         rope=rope, seq=seq)
            lambda_init = 0.8 - 0.6 * math.exp(-0.3 * l)
            mix = _diff_attention(proj, a_lambda[l], a_subln[l], batch=batch, seq=seq,
                                  lambda_init=lambda_init, tq=tiles["tq"], tk=tiles["tk"])
            x2d = _outproj(x2d, proj, memkv, w_o, seq=seq, mem_tokens=mem_tokens,
                           tm=tiles["tm_out"], mix=mix)
        else:
            w_in = (b_w_in[l - n_a] * scale).astype(BF16)
            proj = _proj(x2d, norm_mix[l], w_in, tm=tiles["tm_proj"], n_rope=dil_qk,
                         rope=rope, seq=seq)
            outs, stats = [], []
            for gi, (window, dilation) in enumerate(DIL_GROUPS):
                c0 = gi * DIL_GROUP_WIDTH
                blk0 = c0 // LANES
                if dilation == 1:
                    q4, q_off = proj.reshape(batch, 1, seq, proj.shape[1]), blk0
                else:
                    q4, q_off = _to_residue_major(proj, batch, seq, dilation,
                                                  c0, c0 + DIL_GROUP_WIDTH), 0
                k4, k_off, v4, v_off = shared[gi]
                o4, st4 = _dilated_group(q4, k4, v4, q_off=q_off, k_off=k_off, v_off=v_off,
                                         n_steps=window // dilation)
                outs.append(_from_residue_major(o4))
                stats.append(_from_residue_major(st4))
            x2d = _outproj(x2d, proj, memkv, w_o, seq=seq, mem_tokens=mem_tokens,
                           tm=tiles["tm_out"], group_out=outs, group_stats=stats, expand=expand)

        x2d = _peer(x2d, norm_ffn[l], peer_wq[l].astype(BF16),
                    peer_keys[l].reshape(2 * PEER_HEADS, PEER_N_KEYS, PEER_HALF).astype(BF16),
                    _pack_expert_table(peer_u[l]), _sc_word_table(peer_u[l]),
                    _pair_slabs(peer_v[l]), _sc_word_table(peer_v[l]), peer_consts,
                    tm_route=tiles["tm_route"], tm_expert=tiles["tm_expert"],
                    n_sparsecore=_sparsecore_tokens(t, tiles["tm_expert"]))

        if l == n_a - 1:
            kv = _proj(x2d, shared_norm, shared_w_kv.astype(BF16), tm=tiles["tm_proj"],
                       n_rope=dil_qk, rope=rope, seq=seq)
            shared = []
            for gi, (window, dilation) in enumerate(DIL_GROUPS):
                c0 = gi * DIL_GROUP_WIDTH
                if dilation == 1:
                    kv4 = kv.reshape(batch, 1, seq, kv.shape[1])
                    shared.append((kv4, c0 // LANES, kv4, (dil_qk + c0) // LANES))
                else:
                    k4 = _to_residue_major(kv, batch, seq, dilation, c0, c0 + DIL_GROUP_WIDTH)
                    v4 = _to_residue_major(kv, batch, seq, dilation, dil_qk + c0,
                                           dil_qk + c0 + DIL_GROUP_WIDTH)
                    shared.append((k4, 0, v4, 0))

    return _final_norm(x2d, final_norm, tm=tiles["tm_norm"]).reshape(batch, seq, d)
```

```python
import functools
import math

import jax
import jax.numpy as jnp
from jax import lax
from jax.experimental import pallas as pl
from jax.experimental.pallas import tpu as pltpu
from jax.experimental.pallas import tpu_sc as plsc

F32 = jnp.float32
BF16 = jnp.bfloat16

HEAD_DIM = 64
ROPE_DIMS = HEAD_DIM // 4
ROPE_HALF = ROPE_DIMS // 2
ROPE_THETA = 500000.0
NORM_EPS = 1e-5
DIFF_HEADS = 6
MEM_HEADS = 4
MEM_Q_WIDTH = MEM_HEADS * HEAD_DIM
DIL_GROUPS = ((128, 1), (512, 4), (2048, 16))
DIL_HEADS = 12
DIL_BLOCK = 128
DIL_GROUP_WIDTH = DIL_HEADS * HEAD_DIM
PEER_HEADS = 8
PEER_N_KEYS = 128
PEER_TOPK = 16
PEER_HALF = 128
PEER_SLOTS = PEER_HEADS * PEER_TOPK

LANES = 128
SUBLANES = 8
VMEM_LIMIT_BYTES = 56 * 1024 * 1024

_NT = (((1,), (1,)), ((), ()))


def _params(n_axes):
    return pltpu.CompilerParams(dimension_semantics=("arbitrary",) * n_axes,
                                vmem_limit_bytes=VMEM_LIMIT_BYTES)


def _resident(shape):
    zeros = (0,) * len(shape)
    return pl.BlockSpec(shape, lambda *_: zeros, pipeline_mode=pl.Buffered(1))


def _rms_scale(x, g):
    ms = jnp.mean(x * x, axis=-1, keepdims=True)
    return x * lax.rsqrt(ms + NORM_EPS) * g


def _rope_tables(seq):
    inv = ROPE_THETA ** (-jnp.arange(0, ROPE_DIMS, 2, dtype=F32) / ROPE_DIMS)
    ang = jnp.arange(seq, dtype=F32)[:, None] * inv[None, :]
    cos, sin = jnp.cos(ang), jnp.sin(ang)
    ones = jnp.ones((seq, HEAD_DIM - ROPE_DIMS), F32)
    zeros = jnp.zeros((seq, HEAD_DIM - ROPE_DIMS), F32)
    zh = jnp.zeros((seq, ROPE_HALF), F32)
    c = jnp.concatenate([cos, cos, ones], axis=1)
    sa = jnp.concatenate([-sin, zh, zeros], axis=1)
    sb = jnp.concatenate([zh, sin, zeros], axis=1)
    rep = LANES // HEAD_DIM
    return tuple(jnp.tile(t, (1, rep)) for t in (c, sa, sb))


def _proj_kernel(*refs, n_rope, chunk):
    if n_rope:
        x_ref, g_ref, w_ref, c_ref, sa_ref, sb_ref, o_ref = refs
    else:
        x_ref, g_ref, w_ref, o_ref = refs
    y = _rms_scale(x_ref[...], g_ref[...]).astype(BF16)
    n = o_ref.shape[1]
    for c0 in range(0, n, chunk):
        acc = jnp.dot(y, w_ref[:, c0:c0 + chunk], preferred_element_type=F32)
        if c0 < n_rope:
            for k0 in range(0, chunk, LANES):
                a = acc[:, k0:k0 + LANES]
                a = (a * c_ref[...]
                     + pltpu.roll(a, LANES - ROPE_HALF, 1) * sa_ref[...]
                     + pltpu.roll(a, ROPE_HALF, 1) * sb_ref[...])
                o_ref[:, c0 + k0:c0 + k0 + LANES] = a.astype(o_ref.dtype)
        else:
            o_ref[:, c0:c0 + chunk] = acc.astype(o_ref.dtype)


def _proj(x2d, g, w_bf16, *, tm, n_rope=0, rope=None, seq=None):
    t, d = x2d.shape
    n = w_bf16.shape[1]
    chunk = 256
    assert t % tm == 0 and n % chunk == 0 and n_rope % chunk == 0
    in_specs = [pl.BlockSpec((tm, d), lambda i: (i, 0)),
                _resident((1, d)),
                _resident((d, n))]
    args = [x2d, g.reshape(1, d), w_bf16]
    if n_rope:
        nblk = seq // tm
        spec = pl.BlockSpec((tm, LANES), lambda i: (i % nblk, 0))
        in_specs += [spec, spec, spec]
        args += list(rope)
    return pl.pallas_call(
        functools.partial(_proj_kernel, n_rope=n_rope, chunk=chunk),
        grid=(t // tm,),
        in_specs=in_specs,
        out_specs=pl.BlockSpec((tm, n), lambda i: (i, 0)),
        out_shape=jax.ShapeDtypeStruct((t, n), BF16),
        compiler_params=_params(1),
        name="norm_proj",
    )(*args)


def _online_softmax_step(q, k, v, carry, mask):
    m, l, acc = carry
    s = lax.dot_general(q, k, _NT, preferred_element_type=F32)
    if mask is not None:
        s = jnp.where(mask, s, -jnp.inf)
    m_new = jnp.maximum(m, jnp.max(s, axis=-1, keepdims=True))
    alpha = jnp.exp(m - m_new)
    p = jnp.exp(s - m_new)
    l = alpha * l + jnp.sum(p, axis=-1, keepdims=True)
    acc = alpha * acc + jnp.dot(p.astype(BF16), v, preferred_element_type=F32)
    return m_new, l, acc


def _diff_attn_kernel(q_ref, k_ref, v_ref, lp_ref, g_ref, o_ref, *, tq, tk, lambda_init):
    qi = pl.program_id(2)
    q = q_ref[...]
    lane = lax.broadcasted_iota(jnp.int32, q.shape, 1)
    zero = jnp.zeros_like(q)
    qa = jnp.where(lane < HEAD_DIM, q, zero)
    qb = jnp.where(lane >= HEAD_DIM, q, zero)

    def init():
        return (jnp.full((tq, 1), -jnp.inf, F32), jnp.zeros((tq, 1), F32),
                jnp.zeros((tq, 2 * HEAD_DIM), F32))

    def body(j, carry):
        c1, c2 = carry
        k = k_ref[pl.ds(pl.multiple_of(j * tk, tk), tk), :]
        v = v_ref[pl.ds(pl.multiple_of(j * tk, tk), tk), :]
        return (_online_softmax_step(qa, k, v, c1, None),
                _online_softmax_step(qb, k, v, c2, None))

    n_full = (qi * tq) // tk
    c1, c2 = lax.fori_loop(0, n_full, body, (init(), init()))
    row = qi * tq + lax.broadcasted_iota(jnp.int32, (tq, tk), 0)
    col = n_full * tk + lax.broadcasted_iota(jnp.int32, (tq, tk), 1)
    causal = col <= row
    start = pl.multiple_of(n_full * tk, tk)
    k = k_ref[pl.ds(start, tk), :]
    v = v_ref[pl.ds(start, tk), :]
    m1, l1, a1 = _online_softmax_step(qa, k, v, c1, causal)
    m2, l2, a2 = _online_softmax_step(qb, k, v, c2, causal)

    lp = lp_ref[...]
    lam = (jnp.exp(jnp.sum(lp[0:1] * lp[1:2], axis=-1, keepdims=True))
           - jnp.exp(jnp.sum(lp[2:3] * lp[3:4], axis=-1, keepdims=True)) + lambda_init)
    o = a1 / l1 - lam * (a2 / l2)
    o = _rms_scale(o, g_ref[...]) * (1.0 - lambda_init)
    o_ref[...] = o.astype(o_ref.dtype)


def _diff_attention(proj, lp, subln, *, batch, seq, lambda_init, tq, tk):
    assert tk % tq == 0 and seq % tk == 0
    nq = seq // tq
    width = 2 * HEAD_DIM
    return pl.pallas_call(
        functools.partial(_diff_attn_kernel, tq=tq, tk=tk, lambda_init=lambda_init),
        grid=(batch, DIFF_HEADS, nq),
        in_specs=[
            pl.BlockSpec((tq, width), lambda b, h, i: (b * nq + i, h)),
            pl.BlockSpec((seq, width), lambda b, h, i: (b, DIFF_HEADS + h)),
            pl.BlockSpec((seq, width), lambda b, h, i: (b, 2 * DIFF_HEADS + h)),
            _resident((4, HEAD_DIM)),
            _resident((1, width)),
        ],
        out_specs=pl.BlockSpec((tq, width), lambda b, h, i: (b * nq + i, h)),
        out_shape=jax.ShapeDtypeStruct((batch * seq, DIFF_HEADS * width), BF16),
        compiler_params=_params(3),
        name="diff_attention",
    )(proj, proj, proj, lp, subln.reshape(1, width))


def _dilated_kernel(q_ref, kc_ref, kp_ref, vc_ref, vp_ref, o_ref, st_ref, *, tq, n_steps):
    n = pl.program_id(2)
    hp = pl.program_id(3)
    blk = DIL_BLOCK

    @pl.when(hp == 0)
    def _():
        st_ref[...] = jnp.zeros_like(st_ref)

    qrow = lax.broadcasted_iota(jnp.int32, (blk, 2 * blk), 0)
    kcol = lax.broadcasted_iota(jnp.int32, (blk, 2 * blk), 1)
    dist = qrow + blk - kcol
    band = (dist >= 0) & (dist <= n_steps)
    lane_q = lax.broadcasted_iota(jnp.int32, (blk, LANES), 1)
    lane_s = lax.broadcasted_iota(jnp.int32, (blk, LANES), 1)

    for i in range(tq // blk):
        rows = slice(i * blk, (i + 1) * blk)
        q = q_ref[rows, :]
        if i == 0:
            k_prev, v_prev = kp_ref[...], vp_ref[...]
            valid = band & ((n > 0) | (kcol >= blk))
        else:
            prev = slice((i - 1) * blk, i * blk)
            k_prev, v_prev = kc_ref[prev, :], vc_ref[prev, :]
            valid = band
        keys = jnp.concatenate([k_prev, kc_ref[rows, :]], axis=0)
        vals = jnp.concatenate([v_prev, vc_ref[rows, :]], axis=0)
        out = jnp.zeros((blk, LANES), F32)
        stats = st_ref[rows, :]
        for e in range(LANES // HEAD_DIM):
            head_lanes = (lane_q // HEAD_DIM) == e
            qe = jnp.where(head_lanes, q, jnp.zeros_like(q))
            s = lax.dot_general(qe, keys, _NT, preferred_element_type=F32)
            s = jnp.where(valid, s, -jnp.inf)
            m = jnp.max(s, axis=-1, keepdims=True)
            p = jnp.exp(s - m)
            den = jnp.sum(p, axis=-1, keepdims=True)
            oe = jnp.dot((p / den).astype(BF16), vals, preferred_element_type=F32)
            out = jnp.where(head_lanes, oe, out)
            lse = m + jnp.log(den)
            stats = jnp.where(lane_s == hp * (LANES // HEAD_DIM) + e, lse, stats)
        o_ref[rows, :] = out.astype(o_ref.dtype)
        st_ref[rows, :] = stats


def _dilated_group(q4, k4, v4, *, q_off, k_off, v_off, n_steps):
    b, d, length, _ = q4.shape
    tq = min(512, length)
    assert length % tq == 0 and tq % DIL_BLOCK == 0
    sub = tq // DIL_BLOCK
    pairs = DIL_GROUP_WIDTH // LANES

    def cur(off):
        return pl.BlockSpec((None, None, tq, LANES), lambda bi, r, n, h: (bi, r, n, off + h))

    def prev(off):
        return pl.BlockSpec((None, None, DIL_BLOCK, LANES),
                            lambda bi, r, n, h: (bi, r, jnp.maximum(n * sub - 1, 0), off + h))

    return pl.pallas_call(
        functools.partial(_dilated_kernel, tq=tq, n_steps=n_steps),
        grid=(b, d, length // tq, pairs),
        in_specs=[cur(q_off), cur(k_off), prev(k_off), cur(v_off), prev(v_off)],
        out_specs=[pl.BlockSpec((None, None, tq, LANES), lambda bi, r, n, h: (bi, r, n, h)),
                   pl.BlockSpec((None, None, tq, LANES), lambda bi, r, n, h: (bi, r, n, 0))],
        out_shape=[jax.ShapeDtypeStruct((b, d, length, DIL_GROUP_WIDTH), BF16),
                   jax.ShapeDtypeStruct((b, d, length, LANES), F32)],
        compiler_params=_params(4),
        name="dilated_attention",
    )(q4, k4, k4, v4, v4)


def _memory_attention(q, k, v):
    lane = lax.broadcasted_iota(jnp.int32, q.shape, 1)
    out = jnp.zeros(q.shape, F32)
    for h in range(MEM_HEADS):
        head_lanes = (lane // HEAD_DIM) == h
        qh = jnp.where(head_lanes, q, jnp.zeros_like(q))
        s = lax.dot_general(qh, k, _NT, preferred_element_type=F32)
        m = jnp.max(s, axis=-1, keepdims=True)
        p = jnp.exp(s - m)
        den = jnp.sum(p, axis=-1, keepdims=True)
        oh = jnp.dot((p / den).astype(BF16), v, preferred_element_type=F32)
        out = jnp.where(head_lanes, oh, out)
    return out


def _split_bf16(x):
    hi = x.astype(BF16)
    lo = (x - hi.astype(F32)).astype(BF16)
    return hi, lo


def _outproj_kernel(*refs, n_groups):
    if n_groups:
        x_ref, qm_ref, mk_ref, mv_ref, w_ref = refs[:5]
        o_refs = refs[5:5 + n_groups]
        st_refs = refs[5 + n_groups:5 + 2 * n_groups]
        ex_ref, out_ref = refs[5 + 2 * n_groups:]
        lses = [r[...] for r in st_refs]
        top = functools.reduce(jnp.maximum, lses)
        es = [jnp.exp(l - top) for l in lses]
        den = functools.reduce(lambda a, b: a + b, es)
        mix = None
        for e, o_ref in zip(es, o_refs):
            hi, lo = _split_bf16(e / den)
            wide = (jnp.dot(hi, ex_ref[...], preferred_element_type=F32)
                    + jnp.dot(lo, ex_ref[...], preferred_element_type=F32))
            term = wide * o_ref[...].astype(F32)
            mix = term if mix is None else mix + term
        mix = mix.astype(BF16)
    else:
        x_ref, qm_ref, mk_ref, mv_ref, w_ref, mix_ref, out_ref = refs
        mix = mix_ref[...]
    mo = _memory_attention(qm_ref[...], mk_ref[...], mv_ref[...]).astype(BF16)
    k_mix = mix.shape[1]
    acc = jnp.dot(mix, w_ref[:k_mix, :], preferred_element_type=F32)
    acc += jnp.dot(mo, w_ref[k_mix:, :], preferred_element_type=F32)
    out_ref[...] = x_ref[...] + acc


def _outproj(x2d, proj, memkv, w_bf16, *, seq, mem_tokens, tm, mix=None, group_out=None,
             group_stats=None, expand=None):
    t, d = x2d.shape
    per_batch = seq // tm
    qm_block = (proj.shape[1] - MEM_Q_WIDTH) // MEM_Q_WIDTH
    row = lambda i: (i, 0)
    in_specs = [pl.BlockSpec((tm, d), row),
                pl.BlockSpec((tm, MEM_Q_WIDTH), lambda i: (i, qm_block)),
                pl.BlockSpec((mem_tokens, MEM_Q_WIDTH), lambda i: (i // per_batch, 0)),
                pl.BlockSpec((mem_tokens, MEM_Q_WIDTH), lambda i: (i // per_batch, 1)),
                _resident(w_bf16.shape)]
    args = [x2d, proj, memkv, memkv, w_bf16]
    if mix is not None:
        n_groups = 0
        in_specs.append(pl.BlockSpec((tm, mix.shape[1]), row))
        args.append(mix)
    else:
        n_groups = len(group_out)
        in_specs += [pl.BlockSpec((tm, DIL_GROUP_WIDTH), row)] * n_groups
        in_specs += [pl.BlockSpec((tm, LANES), row)] * n_groups
        in_specs.append(_resident(expand.shape))
        args += list(group_out) + list(group_stats) + [expand]
    return pl.pallas_call(
        functools.partial(_outproj_kernel, n_groups=n_groups),
        grid=(t // tm,),
        in_specs=in_specs,
        out_specs=pl.BlockSpec((tm, d), row),
        out_shape=jax.ShapeDtypeStruct((t, d), F32),
        compiler_params=_params(1),
        name="mix_outproj",
    )(*args)


def _topk_rows(s, k):
    n = s.shape[0]
    row = lax.broadcasted_iota(jnp.int32, s.shape, 0)
    vals, idxs = [], []
    for _ in range(k):
        m = jnp.max(s, axis=0, keepdims=True)
        i = jnp.min(jnp.where(s == m, row, n), axis=0, keepdims=True)
        vals.append(m)
        idxs.append(i)
        s = jnp.where(row == i, -jnp.inf, s)
    return jnp.concatenate(vals, axis=0), jnp.concatenate(idxs, axis=0)


def _select_rows(table, idx):
    row = lax.broadcasted_iota(jnp.int32, table.shape, 0)
    return jnp.sum(jnp.where(row == idx, table, jnp.zeros_like(table)), axis=0, keepdims=True)


def _peer_route_kernel(x_ref, g_ref, wq_ref, keys_ref, xn_ref, expert_ref, pair_ref, parity_ref,
                       gate_ref):
    xn = _rms_scale(x_ref[...], g_ref[...])
    xn_ref[...] = jnp.stack([xn[:, c:c + LANES] for c in range(0, xn.shape[1], LANES)], axis=1)
    q = jnp.dot(xn.astype(BF16), wq_ref[...], preferred_element_type=F32).astype(BF16)
    kk = PEER_TOPK
    experts_by_head, gates_by_head = [], []
    for h in range(PEER_HEADS):
        tops = []
        for half in range(2):
            c0 = (2 * h + half) * PEER_HALF
            s = lax.dot_general(keys_ref[2 * h + half], q[:, c0:c0 + PEER_HALF], _NT,
                                preferred_element_type=F32)
            tops.append(_topk_rows(s, kk))
        (s1, i1), (s2, i2) = tops
        half = kk // 2
        sub = lax.broadcasted_iota(jnp.int32, (half, s1.shape[1]), 0)
        blocks = [s1[0:1] + s2]
        experts = [i1[0:1] * PEER_N_KEYS + i2]
        for a in range(1, half):
            blocks.append(jnp.where(sub < kk // (a + 1), s1[a:a + 1] + s2[:half], -jnp.inf))
            experts.append(i1[a:a + 1] * PEER_N_KEYS + i2[:half])
        blocks.append(s1[half:] + s2[0:1])
        experts.append(i1[half:] * PEER_N_KEYS + i2[0:1])
        cand = jnp.concatenate(blocks, axis=0)
        cand_expert = jnp.concatenate(experts, axis=0)
        top_s, top_j = _topk_rows(cand, kk)
        expert = jnp.concatenate([_select_rows(cand_expert, top_j[r:r + 1]) for r in range(kk)],
                                 axis=0)
        e = jnp.exp(top_s - top_s[0:1])
        experts_by_head.append(expert)
        gates_by_head.append(e / jnp.sum(e, axis=0, keepdims=True))
    expert = jnp.concatenate(experts_by_head, axis=0).T
    expert_ref[...] = expert
    pair_ref[...] = expert >> 1
    parity_ref[...] = expert & 1
    gate_ref[...] = jnp.concatenate(gates_by_head, axis=0).T


def _peer_route(x2d, g, wq_bf16, keys_bf16, *, tm):
    t, d = x2d.shape
    assert tm == PEER_SLOTS
    rows = pl.BlockSpec((tm, PEER_SLOTS), lambda i: (i, 0))
    slot_shape = jax.ShapeDtypeStruct((t, PEER_SLOTS), jnp.int32)
    return pl.pallas_call(
        _peer_route_kernel,
        grid=(t // tm,),
        in_specs=[pl.BlockSpec((tm, d), lambda i: (i, 0)),
                  _resident((1, d)),
                  _resident(wq_bf16.shape),
                  _resident(keys_bf16.shape)],
        out_specs=[pl.BlockSpec((tm, d // LANES, LANES), lambda i: (i, 0, 0)),
                   rows, rows, rows, rows],
        out_shape=[jax.ShapeDtypeStruct((t, d // LANES, LANES), F32),
                   slot_shape, slot_shape, slot_shape,
                   jax.ShapeDtypeStruct((t, PEER_SLOTS), F32)],
        compiler_params=_params(1),
        name="peer_route",
    )(x2d, g.reshape(1, d), wq_bf16, keys_bf16)


_HI_MASK = 0xFFFF0000


def _pack_expert_table(u):
    e, d = u.shape
    bits = lax.bitcast_convert_type(u.astype(BF16), jnp.uint16).astype(jnp.uint32)
    bits = bits.reshape(e // 2, 2, d)
    packed = (bits[:, 0] << 16) | bits[:, 1]
    return packed.reshape(e // 2, d // LANES, LANES)


_BIT_REVERSED = (0, 4, 2, 6, 1, 5, 3, 7)


def _bf16_pair_words(x):
    bits = pltpu.bitcast(x, jnp.uint32)
    top = (bits + jnp.uint32(0x7FFF) + ((bits >> 16) & jnp.uint32(1))) >> 16
    return (top << 16) | top


def _packed_add(a, b):
    return pltpu.bitcast(pltpu.bitcast(a, BF16) + pltpu.bitcast(b, BF16), jnp.uint32)


def _sublane_sums(words, sub):
    level = [words[i] for i in _BIT_REVERSED]
    for k in (4, 2, 1):
        low = (sub & k) == 0
        merged = []
        for a, b in zip(level[0::2], level[1::2]):
            if k == 4:
                merged.append(_packed_add(jnp.where(low, a, b),
                                          pltpu.roll(jnp.where(low, b, a), k, 0)))
            else:
                merged.append(jnp.where(low,
                                        _packed_add(a, pltpu.roll(a, SUBLANES - k, 0)),
                                        _packed_add(b, pltpu.roll(b, k, 0))))
        level = merged
    return level[0]


def _peer_up_kernel(pair_ref, x_ref, tab_ref, gate_ref, par_ref, spread_ref, wexp_ref, w2_ref):
    tm = x_ref.shape[0]
    sub = lax.broadcasted_iota(jnp.int32, (SUBLANES, LANES), 0)
    hi_mask = jnp.uint32(_HI_MASK)

    def token_sums(t, m):
        xx = pltpu.bitcast(_bf16_pair_words(x_ref[t]), BF16)
        even, odd = [], []
        for g in range(PEER_SLOTS // SUBLANES):
            words = []
            for i in range(SUBLANES):
                slab = pltpu.bitcast(tab_ref[pair_ref[t, g * SUBLANES + i]], BF16)
                words.append(pltpu.bitcast(slab * xx, jnp.uint32))
            q = _sublane_sums(words, sub)
            even.append(pltpu.bitcast(q & hi_mask, F32))
            odd.append(pltpu.bitcast(q << 16, F32))
        lane_parts = jnp.concatenate(even + odd, axis=0).astype(BF16)
        onehot = (sub == m).astype(BF16)
        return lax.dot_general(onehot, lane_parts, _NT, preferred_element_type=F32)

    def group(gi, _):
        base = pl.multiple_of(gi * SUBLANES, SUBLANES)
        h2 = token_sums(base, 0)
        for m in range(1, SUBLANES):
            h2 = h2 + token_sums(base + m, m)
        odd_slot = par_ref[pl.ds(base, SUBLANES), :] != 0
        h = jnp.where(odd_slot, h2[:, PEER_SLOTS:], h2[:, :PEER_SLOTS])
        act = 0.5 * h * (1.0 + lax.erf(h * (1.0 / math.sqrt(2.0))))
        w = gate_ref[pl.ds(base, SUBLANES), :] * act
        zero = jnp.zeros_like(w)
        w2_ref[pl.ds(base, SUBLANES), 0:PEER_SLOTS] = jnp.where(odd_slot, zero, w)
        w2_ref[pl.ds(base, SUBLANES), PEER_SLOTS:2 * PEER_SLOTS] = jnp.where(odd_slot, w, zero)
        return 0

    lax.fori_loop(0, tm // SUBLANES, group, 0)
    hi, lo = _split_bf16(w2_ref[...])
    wexp_ref[...] = (jnp.dot(hi, spread_ref[...], preferred_element_type=F32)
                     + jnp.dot(lo, spread_ref[...], preferred_element_type=F32))


PAIR_ROWS = 2 * SUBLANES
DOWN_K = PEER_SLOTS * PAIR_ROWS
DOWN_PAIRS_PER_STEP = 4


def _peer_down_kernel(pair_ref, wexp_ref, diag_ref, x_ref, tab_ref, o_ref):
    tm = x_ref.shape[0]

    def left_rows(t):
        row = wexp_ref[pl.ds(t, 1), :]
        hi, lo = _split_bf16(row * diag_ref[...])
        return [hi, lo]

    def token_pair(ta):
        tb = ta + 1
        rhs = jnp.concatenate(
            [jnp.concatenate([tab_ref[pair_ref[ta, j]], tab_ref[pair_ref[tb, j]]], axis=1)
             for j in range(PEER_SLOTS)], axis=0)
        lhs = jnp.concatenate(left_rows(ta) + left_rows(tb), axis=0)
        out = jnp.dot(lhs, rhs, preferred_element_type=F32)
        s = SUBLANES
        o_ref[ta] = x_ref[ta] + (out[0:s, :LANES] + out[s:2 * s, :LANES])
        o_ref[tb] = x_ref[tb] + (out[2 * s:3 * s, LANES:] + out[3 * s:4 * s, LANES:])

    def step(i, _):
        for k in range(DOWN_PAIRS_PER_STEP):
            token_pair(2 * (DOWN_PAIRS_PER_STEP * i + k))
        return 0

    lax.fori_loop(0, tm // (2 * DOWN_PAIRS_PER_STEP), step, 0)


def _smem_rows(tm):
    return pl.BlockSpec((tm, PEER_SLOTS), lambda i: (i, 0), memory_space=pltpu.SMEM)


def _peer_up(pair, parity, xn3, table, gate, spread, *, tm, n_tokens):
    rows = pl.BlockSpec((tm, PEER_SLOTS), lambda i: (i, 0))
    return pl.pallas_call(
        _peer_up_kernel,
        grid=(n_tokens // tm,),
        in_specs=[_smem_rows(tm),
                  pl.BlockSpec((tm, SUBLANES, LANES), lambda i: (i, 0, 0)),
                  _resident(table.shape), rows, rows, _resident(spread.shape)],
        out_specs=pl.BlockSpec((tm, DOWN_K), lambda i: (i, 0)),
        out_shape=jax.ShapeDtypeStruct((n_tokens, DOWN_K), F32),
        scratch_shapes=[pltpu.VMEM((tm, 2 * PEER_SLOTS), F32)],
        compiler_params=_params(1),
        name="peer_up",
    )(pair, xn3, table, gate, parity, spread)


def _peer_down(pair, wexp, diag, x3, table, *, tm, n_tokens):
    tile = pl.BlockSpec((tm, SUBLANES, LANES), lambda i: (i, 0, 0))
    return pl.pallas_call(
        _peer_down_kernel,
        grid=(n_tokens // tm,),
        in_specs=[_smem_rows(tm), pl.BlockSpec((tm, DOWN_K), lambda i: (i, 0)),
                  _resident(diag.shape), tile, _resident(table.shape)],
        out_specs=tile,
        out_shape=jax.ShapeDtypeStruct((n_tokens,) + x3.shape[1:], F32),
        compiler_params=_params(1),
        name="peer_down",
    )(pair, wexp, diag, x3, table)


SC_WORKERS = 32
SC_LANES = 16
SC_HALF = PEER_SLOTS // 2
SC_ROWS_PER_STEP = 4
SC_CHUNKS = 8


def _sc_word_table(v):
    e, d = v.shape
    bits = lax.bitcast_convert_type(v.astype(BF16), jnp.uint16).astype(jnp.uint32)
    blk = bits.reshape(e, d // (2 * SC_LANES), 2, SC_LANES)
    words = (blk[:, :, 0] << 16) | blk[:, :, 1]
    return lax.bitcast_convert_type(words.reshape(e, d // 2), jnp.int32)


def _sc_peer_up(expert, xn3, table, *, first, n_tokens):
    d = xn3.shape[1] * xn3.shape[2]
    per_worker = n_tokens // SC_WORKERS
    words = d // 2
    per_row = LANES // SC_LANES
    mesh = plsc.VectorSubcoreMesh(core_axis_name="c", subcore_axis_name="s")

    @functools.partial(
        pl.kernel, mesh=mesh,
        out_type=jax.ShapeDtypeStruct((n_tokens, PEER_SLOTS), F32),
        scratch_types=[
            pltpu.VMEM((PEER_SLOTS,), jnp.int32),
            pltpu.VMEM(xn3.shape[1:], F32),
            pltpu.VMEM((2, SC_HALF, words), jnp.int32),
            pltpu.VMEM((PEER_SLOTS * SC_LANES,), F32),
            pltpu.VMEM((PEER_SLOTS,), F32),
            pltpu.SemaphoreType.DMA,
            pltpu.SemaphoreType.DMA,
        ],
        compiler_params=pltpu.CompilerParams(needs_layout_passes=False),
        name="sc_peer_up",
    )
    def run(expert_hbm, x_hbm, table_hbm, h_hbm, idx_v, x_v, rows_v, part_v, h_v, sem0, sem1):
        worker = lax.axis_index("s") * 2 + lax.axis_index("c")
        local = worker * per_worker
        base = first + local
        lane = lax.iota(jnp.int32, SC_LANES)

        def partial_sums(h):
            @pl.loop(0, SC_HALF, step=SC_ROWS_PER_STEP)
            def _(j0):
                sums = [None] * SC_ROWS_PER_STEP
                for m in range(words // SC_LANES):
                    xa = x_v[(2 * m) // per_row, pl.ds((2 * m) % per_row * SC_LANES, SC_LANES)]
                    xb = x_v[(2 * m + 1) // per_row,
                             pl.ds((2 * m + 1) % per_row * SC_LANES, SC_LANES)]
                    for r in range(SC_ROWS_PER_STEP):
                        word = rows_v[h, j0 + r, pl.ds(m * SC_LANES, SC_LANES)]
                        term = (xa * lax.bitcast_convert_type(word & jnp.int32(-65536), F32)
                                + xb * lax.bitcast_convert_type(word << 16, F32))
                        sums[r] = term if sums[r] is None else sums[r] + term
                for r in range(SC_ROWS_PER_STEP):
                    part_v[pl.ds((h * SC_HALF + j0 + r) * SC_LANES, SC_LANES)] = sums[r]

        @pl.loop(0, per_worker)
        def _(i):
            tok = base + i
            pltpu.sync_copy(expert_hbm.at[tok], idx_v)
            first_half = pltpu.async_copy(table_hbm.at[idx_v.at[pl.ds(0, SC_HALF)]],
                                          rows_v.at[0], sem0)
            second_half = pltpu.async_copy(table_hbm.at[idx_v.at[pl.ds(SC_HALF, SC_HALF)]],
                                           rows_v.at[1], sem1)
            pltpu.sync_copy(x_hbm.at[tok], x_v)
            first_half.wait()
            partial_sums(0)
            second_half.wait()
            partial_sums(1)
            for g in range(PEER_SLOTS // SC_LANES):
                total = None
                for k in range(SC_LANES):
                    column = plsc.load_gather(
                        part_v, [g * SC_LANES * SC_LANES + lane * SC_LANES + k])
                    total = column if total is None else total + column
                h_v[pl.ds(g * SC_LANES, SC_LANES)] = total
            pltpu.sync_copy(h_v, h_hbm.at[local + i])

    return run(expert, xn3, table)


def _peer_act_kernel(h_ref, gate_ref, rep_ref, after_ref, o_ref):
    del after_ref
    h = h_ref[...]
    w = gate_ref[...] * (0.5 * h * (1.0 + lax.erf(h * (1.0 / math.sqrt(2.0)))))
    hi, lo = _split_bf16(w)
    o_ref[...] = (jnp.dot(hi, rep_ref[...], preferred_element_type=F32)
                  + jnp.dot(lo, rep_ref[...], preferred_element_type=F32))


def _peer_act(h, gate, rep, after, *, first, tm):
    n = h.shape[0]
    off = first // tm
    return pl.pallas_call(
        _peer_act_kernel,
        grid=(n // tm,),
        in_specs=[pl.BlockSpec((tm, PEER_SLOTS), lambda i: (i, 0)),
                  pl.BlockSpec((tm, PEER_SLOTS), lambda i: (i + off, 0)),
                  _resident(rep.shape),
                  pl.BlockSpec(memory_space=pl.ANY)],
        out_specs=pl.BlockSpec((tm, PEER_SLOTS * SC_LANES), lambda i: (i, 0)),
        out_shape=jax.ShapeDtypeStruct((n, PEER_SLOTS * SC_LANES), F32),
        compiler_params=_params(1),
        name="peer_act",
    )(h, gate, rep, after)


def _sc_peer_down(expert, w_lanes, x2d, table, *, first, n_tokens):
    t, d = x2d.shape
    per_worker = n_tokens // SC_WORKERS
    words = d // 2
    mesh = plsc.VectorSubcoreMesh(core_axis_name="c", subcore_axis_name="s")

    @functools.partial(
        pl.kernel, mesh=mesh,
        out_type=jax.ShapeDtypeStruct((t, d), F32),
        scratch_types=[
            pltpu.VMEM((PEER_SLOTS,), jnp.int32),
            pltpu.VMEM((PEER_SLOTS * SC_LANES,), F32),
            pltpu.VMEM((d,), F32),
            pltpu.VMEM((2, SC_HALF, words), jnp.int32),
            pltpu.SemaphoreType.DMA,
            pltpu.SemaphoreType.DMA,
        ],
        compiler_params=pltpu.CompilerParams(needs_layout_passes=False),
        name="sc_peer_down",
    )
    def run(expert_hbm, w_hbm, x_hbm, table_hbm, out_hbm, idx_v, w_v, out_v, rows_v, sem0, sem1):
        worker = lax.axis_index("s") * 2 + lax.axis_index("c")
        local = worker * per_worker
        base = first + local

        def accumulate(h):
            @pl.loop(0, SC_HALF, step=SC_ROWS_PER_STEP)
            def _(j0):
                weights = [w_v[pl.ds((h * SC_HALF + j0 + r) * SC_LANES, SC_LANES)]
                           for r in range(SC_ROWS_PER_STEP)]
                for g in range(words // SC_LANES // SC_CHUNKS):
                    loaded = [[rows_v[h, j0 + r, pl.ds((g * SC_CHUNKS + q) * SC_LANES, SC_LANES)]
                               for q in range(SC_CHUNKS)] for r in range(SC_ROWS_PER_STEP)]
                    for q in range(SC_CHUNKS):
                        hi_sum = lo_sum = None
                        for r in range(SC_ROWS_PER_STEP):
                            word = loaded[r][q]
                            hi = weights[r] * lax.bitcast_convert_type(word & jnp.int32(-65536), F32)
                            lo = weights[r] * lax.bitcast_convert_type(word << 16, F32)
                            hi_sum = hi if hi_sum is None else hi_sum + hi
                            lo_sum = lo if lo_sum is None else lo_sum + lo
                        col = (g * SC_CHUNKS + q) * 2 * SC_LANES
                        plsc.addupdate(out_v.at[pl.ds(col, SC_LANES)], hi_sum)
                        plsc.addupdate(out_v.at[pl.ds(col + SC_LANES, SC_LANES)], lo_sum)

        @pl.loop(0, per_worker)
        def _(i):
            tok = base + i
            pltpu.sync_copy(expert_hbm.at[tok], idx_v)
            first_half = pltpu.async_copy(table_hbm.at[idx_v.at[pl.ds(0, SC_HALF)]],
                                          rows_v.at[0], sem0)
            second_half = pltpu.async_copy(table_hbm.at[idx_v.at[pl.ds(SC_HALF, SC_HALF)]],
                                           rows_v.at[1], sem1)
            pltpu.sync_copy(w_hbm.at[local + i], w_v)
            pltpu.sync_copy(x_hbm.at[tok], out_v)
            first_half.wait()
            accumulate(0)
            second_half.wait()
            accumulate(1)
            pltpu.sync_copy(out_v, out_hbm.at[tok])

    return run(expert, w_lanes, x2d, table)


def _peer_constants():
    col = jnp.arange(DOWN_K)
    src = ((col // SUBLANES) % 2) * PEER_SLOTS + col // PAIR_ROWS
    spread = (jnp.arange(2 * PEER_SLOTS)[:, None] == src[None, :]).astype(BF16)
    diag = (col[None, :] % SUBLANES == jnp.arange(SUBLANES)[:, None]).astype(F32)
    lane_repeat = jnp.repeat(jnp.eye(PEER_SLOTS, dtype=BF16), SC_LANES, axis=1)
    return spread, diag, lane_repeat


def _pair_slabs(v):
    e, d = v.shape
    return v.astype(BF16).reshape(e // 2, 2 * d // LANES, LANES)


def _peer(x2d, g, wq_bf16, keys_bf16, u_packed, u_words, v_slabs, v_words, consts, *, tm_route,
          tm_expert, n_sparsecore):
    t, d = x2d.shape
    spread, diag, lane_repeat = consts
    n_tc = t - n_sparsecore
    xn3, expert, pair, parity, gate = _peer_route(x2d, g, wq_bf16, keys_bf16, tm=tm_route)
    h_sc = _sc_peer_up(expert, xn3, u_words, first=n_tc, n_tokens=n_sparsecore)
    wexp = _peer_up(pair, parity, xn3, u_packed, gate, spread, tm=tm_expert, n_tokens=n_tc)
    w_lanes = _peer_act(h_sc, gate, lane_repeat, wexp, first=n_tc, tm=tm_expert)
    out = _sc_peer_down(expert, w_lanes, x2d, v_words, first=n_tc, n_tokens=n_sparsecore)
    out3 = _peer_down(pair, wexp, diag, x2d.reshape(t, d // LANES, LANES), v_slabs, tm=tm_expert,
                      n_tokens=n_tc)
    return lax.dynamic_update_slice(out, out3.reshape(n_tc, d), (0, 0))


def _final_norm_kernel(x_ref, g_ref, o_ref):
    o_ref[...] = _rms_scale(x_ref[...], g_ref[...])


def _final_norm(x2d, g, *, tm):
    t, d = x2d.shape
    return pl.pallas_call(
        _final_norm_kernel,
        grid=(t // tm,),
        in_specs=[pl.BlockSpec((tm, d), lambda i: (i, 0)), _resident((1, d))],
        out_specs=pl.BlockSpec((tm, d), lambda i: (i, 0)),
        out_shape=jax.ShapeDtypeStruct((t, d), F32),
        compiler_params=_params(1),
        name="final_norm",
    )(x2d, g.reshape(1, d))


def _tiles(seq):
    return dict(tm_proj=min(256, seq), tm_out=min(512, seq), tq=min(512, seq), tk=min(512, seq),
                tm_route=LANES, tm_expert=LANES, tm_norm=min(512, seq))


SC_SHARE = 0.3125


def _sparsecore_tokens(t, tm):
    unit = tm * SC_WORKERS // math.gcd(tm, SC_WORKERS)
    return int(t * SC_SHARE) // unit * unit


def _to_residue_major(a2d, batch, seq, dilation, c0, c1):
    cols = a2d[:, c0:c1].reshape(batch, seq // dilation, dilation, c1 - c0)
    return cols.transpose(0, 2, 1, 3)


def _from_residue_major(a4):
    b, d, length, c = a4.shape
    return a4.transpose(0, 2, 1, 3).reshape(b * d * length, c)


def kernel(x, mem, norm_mix, a_w_in, a_lambda, a_subln, b_w_in, norm_mem, w_mem_kv, w_out,
           norm_ffn, peer_wq, peer_keys, peer_u, peer_v, shared_norm, shared_w_kv, final_norm):
    batch, seq, d = x.shape
    mem_tokens = mem.shape[1]
    depth = norm_mix.shape[0]
    n_a = a_w_in.shape[0]
    t = batch * seq
    tiles = _tiles(seq)
    rope = _rope_tables(seq)
    scale = HEAD_DIM ** -0.5

    diff_qk = DIFF_HEADS * 2 * HEAD_DIM
    dil_qk = len(DIL_GROUPS) * DIL_GROUP_WIDTH
    peer_consts = _peer_constants()
    expand = jnp.repeat(jnp.eye(LANES, DIL_HEADS, dtype=BF16), HEAD_DIM, axis=1)

    x2d = x.reshape(t, d)
    mem2d = mem.reshape(batch * mem_tokens, d)
    shared = None

    for l in range(depth):
        memkv = _proj(mem2d, norm_mem[l], w_mem_kv[l].astype(BF16), tm=mem_tokens)
        w_o = w_out[l].astype(BF16)
        if l < n_a:
            col_scale = jnp.concatenate([jnp.full((diff_qk,), scale, F32),
                                         jnp.ones((a_w_in.shape[2] - diff_qk - MEM_Q_WIDTH,), F32),
                                         jnp.full((MEM_Q_WIDTH,), scale, F32)])
            w_in = (a_w_in[l] * col_scale).astype(BF16)
            proj = _proj(x2d, norm_mix[l], w_in, tm=tiles["tm_proj"], n_rope=2 * diff_qk,
                         rope=rope, seq=seq)
            lambda_init = 0.8 - 0.6 * math.exp(-0.3 * l)
            mix = _diff_attention(proj, a_lambda[l], a_subln[l], batch=batch, seq=seq,
                                  lambda_init=lambda_init, tq=tiles["tq"], tk=tiles["tk"])
            x2d = _outproj(x2d, proj, memkv, w_o, seq=seq, mem_tokens=mem_tokens,
                           tm=tiles["tm_out"], mix=mix)
        else:
            w_in = (b_w_in[l - n_a] * scale).astype(BF16)
            proj = _proj(x2d, norm_mix[l], w_in, tm=tiles["tm_proj"], n_rope=dil_qk,
                         rope=rope, seq=seq)
            outs, stats = [], []
            for gi, (window, dilation) in enumerate(DIL_GROUPS):
                c0 = gi * DIL_GROUP_WIDTH
                blk0 = c0 // LANES
                if dilation == 1:
                    q4, q_off = proj.reshape(batch, 1, seq, proj.shape[1]), blk0
                else:
                    q4, q_off = _to_residue_major(proj, batch, seq, dilation,
                                                  c0, c0 + DIL_GROUP_WIDTH), 0
                k4, k_off, v4, v_off = shared[gi]
                o4, st4 = _dilated_group(q4, k4, v4, q_off=q_off, k_off=k_off, v_off=v_off,
                                         n_steps=window // dilation)
                outs.append(_from_residue_major(o4))
                stats.append(_from_residue_major(st4))
            x2d = _outproj(x2d, proj, memkv, w_o, seq=seq, mem_tokens=mem_tokens,
                           tm=tiles["tm_out"], group_out=outs, group_stats=stats, expand=expand)

        x2d = _peer(x2d, norm_ffn[l], peer_wq[l].astype(BF16),
                    peer_keys[l].reshape(2 * PEER_HEADS, PEER_N_KEYS, PEER_HALF).astype(BF16),
                    _pack_expert_table(peer_u[l]), _sc_word_table(peer_u[l]),
                    _pair_slabs(peer_v[l]), _sc_word_table(peer_v[l]), peer_consts,
                    tm_route=tiles["tm_route"], tm_expert=tiles["tm_expert"],
                    n_sparsecore=_sparsecore_tokens(t, tiles["tm_expert"]))

        if l == n_a - 1:
            kv = _proj(x2d, shared_norm, shared_w_kv.astype(BF16), tm=tiles["tm_proj"],
                       n_rope=dil_qk, rope=rope, seq=seq)
            shared = []
            for gi, (window, dilation) in enumerate(DIL_GROUPS):
                c0 = gi * DIL_GROUP_WIDTH
                if dilation == 1:
                    kv4 = kv.reshape(batch, 1, seq, kv.shape[1])
                    shared.append((kv4, c0 // LANES, kv4, (dil_qk + c0) // LANES))
                else:
                    k4 = _to_residue_major(kv, batch, seq, dilation, c0, c0 + DIL_GROUP_WIDTH)
                    v4 = _to_residue_major(kv, batch, seq, dilation, dil_qk + c0,
                                           dil_qk + c0 + DIL_GROUP_WIDTH)
                    shared.append((k4, 0, v4, 0))

    return _final_norm(x2d, final_norm, tm=tiles["tm_norm"]).reshape(batch, seq, d)
```

```python
import functools
import math

import jax
import jax.numpy as jnp
from jax import lax
from jax.experimental import pallas as pl
from jax.experimental.pallas import tpu as pltpu
from jax.experimental.pallas import tpu_sc as plsc

F32 = jnp.float32
BF16 = jnp.bfloat16

HEAD_DIM = 64
ROPE_DIMS = HEAD_DIM // 4
ROPE_HALF = ROPE_DIMS // 2
ROPE_THETA = 500000.0
NORM_EPS = 1e-5
DIFF_HEADS = 6
MEM_HEADS = 4
MEM_Q_WIDTH = MEM_HEADS * HEAD_DIM
DIL_GROUPS = ((128, 1), (512, 4), (2048, 16))
DIL_HEADS = 12
DIL_BLOCK = 128
DIL_GROUP_WIDTH = DIL_HEADS * HEAD_DIM
PEER_HEADS = 8
PEER_N_KEYS = 128
PEER_TOPK = 16
PEER_HALF = 128
PEER_SLOTS = PEER_HEADS * PEER_TOPK

LANES = 128
SUBLANES = 8
VMEM_LIMIT_BYTES = 56 * 1024 * 1024

_NT = (((1,), (1,)), ((), ()))


def _params(n_axes):
    return pltpu.CompilerParams(dimension_semantics=("arbitrary",) * n_axes,
                                vmem_limit_bytes=VMEM_LIMIT_BYTES)


def _resident(shape):
    zeros = (0,) * len(shape)
    return pl.BlockSpec(shape, lambda *_: zeros, pipeline_mode=pl.Buffered(1))


def _rms_scale(x, g):
    ms = jnp.mean(x * x, axis=-1, keepdims=True)
    return x * lax.rsqrt(ms + NORM_EPS) * g


def _rope_tables(seq):
    inv = ROPE_THETA ** (-jnp.arange(0, ROPE_DIMS, 2, dtype=F32) / ROPE_DIMS)
    ang = jnp.arange(seq, dtype=F32)[:, None] * inv[None, :]
    cos, sin = jnp.cos(ang), jnp.sin(ang)
    ones = jnp.ones((seq, HEAD_DIM - ROPE_DIMS), F32)
    zeros = jnp.zeros((seq, HEAD_DIM - ROPE_DIMS), F32)
    zh = jnp.zeros((seq, ROPE_HALF), F32)
    c = jnp.concatenate([cos, cos, ones], axis=1)
    sa = jnp.concatenate([-sin, zh, zeros], axis=1)
    sb = jnp.concatenate([zh, sin, zeros], axis=1)
    rep = LANES // HEAD_DIM
    return tuple(jnp.tile(t, (1, rep)) for t in (c, sa, sb))


def _proj_kernel(*refs, n_rope, chunk):
    if n_rope:
        x_ref, g_ref, w_ref, c_ref, sa_ref, sb_ref, o_ref = refs
    else:
        x_ref, g_ref, w_ref, o_ref = refs
    y = _rms_scale(x_ref[...], g_ref[...]).astype(BF16)
    n = o_ref.shape[1]
    for c0 in range(0, n, chunk):
        acc = jnp.dot(y, w_ref[:, c0:c0 + chunk], preferred_element_type=F32)
        if c0 < n_rope:
            for k0 in range(0, chunk, LANES):
                a = acc[:, k0:k0 + LANES]
                a = (a * c_ref[...]
                     + pltpu.roll(a, LANES - ROPE_HALF, 1) * sa_ref[...]
                     + pltpu.roll(a, ROPE_HALF, 1) * sb_ref[...])
                o_ref[:, c0 + k0:c0 + k0 + LANES] = a.astype(o_ref.dtype)
        else:
            o_ref[:, c0:c0 + chunk] = acc.astype(o_ref.dtype)


def _proj(x2d, g, w_bf16, *, tm, n_rope=0, rope=None, seq=None):
    t, d = x2d.shape
    n = w_bf16.shape[1]
    chunk = 256
    assert t % tm == 0 and n % chunk == 0 and n_rope % chunk == 0
    in_specs = [pl.BlockSpec((tm, d), lambda i: (i, 0)),
                _resident((1, d)),
                _resident((d, n))]
    args = [x2d, g.reshape(1, d), w_bf16]
    if n_rope:
        nblk = seq // tm
        spec = pl.BlockSpec((tm, LANES), lambda i: (i % nblk, 0))
        in_specs += [spec, spec, spec]
        args += list(rope)
    return pl.pallas_call(
        functools.partial(_proj_kernel, n_rope=n_rope, chunk=chunk),
        grid=(t // tm,),
        in_specs=in_specs,
        out_specs=pl.BlockSpec((tm, n), lambda i: (i, 0)),
        out_shape=jax.ShapeDtypeStruct((t, n), BF16),
        compiler_params=_params(1),
        name="norm_proj",
    )(*args)


def _online_softmax_step(q, k, v, carry, mask):
    m, l, acc = carry
    s = lax.dot_general(q, k, _NT, preferred_element_type=F32)
    if mask is not None:
        s = jnp.where(mask, s, -jnp.inf)
    m_new = jnp.maximum(m, jnp.max(s, axis=-1, keepdims=True))
    alpha = jnp.exp(m - m_new)
    p = jnp.exp(s - m_new)
    l = alpha * l + jnp.sum(p, axis=-1, keepdims=True)
    acc = alpha * acc + jnp.dot(p.astype(BF16), v, preferred_element_type=F32)
    return m_new, l, acc


def _diff_attn_kernel(q_ref, k_ref, v_ref, lp_ref, g_ref, o_ref, *, tq, tk, lambda_init):
    qi = pl.program_id(2)
    q = q_ref[...]
    lane = lax.broadcasted_iota(jnp.int32, q.shape, 1)
    zero = jnp.zeros_like(q)
    qa = jnp.where(lane < HEAD_DIM, q, zero)
    qb = jnp.where(lane >= HEAD_DIM, q, zero)

    def init():
        return (jnp.full((tq, 1), -jnp.inf, F32), jnp.zeros((tq, 1), F32),
                jnp.zeros((tq, 2 * HEAD_DIM), F32))

    def body(j, carry):
        c1, c2 = carry
        k = k_ref[pl.ds(pl.multiple_of(j * tk, tk), tk), :]
        v = v_ref[pl.ds(pl.multiple_of(j * tk, tk), tk), :]
        return (_online_softmax_step(qa, k, v, c1, None),
                _online_softmax_step(qb, k, v, c2, None))

    n_full = (qi * tq) // tk
    c1, c2 = lax.fori_loop(0, n_full, body, (init(), init()))
    row = qi * tq + lax.broadcasted_iota(jnp.int32, (tq, tk), 0)
    col = n_full * tk + lax.broadcasted_iota(jnp.int32, (tq, tk), 1)
    causal = col <= row
    start = pl.multiple_of(n_full * tk, tk)
    k = k_ref[pl.ds(start, tk), :]
    v = v_ref[pl.ds(start, tk), :]
    m1, l1, a1 = _online_softmax_step(qa, k, v, c1, causal)
    m2, l2, a2 = _online_softmax_step(qb, k, v, c2, causal)

    lp = lp_ref[...]
    lam = (jnp.exp(jnp.sum(lp[0:1] * lp[1:2], axis=-1, keepdims=True))
           - jnp.exp(jnp.sum(lp[2:3] * lp[3:4], axis=-1, keepdims=True)) + lambda_init)
    o = a1 / l1 - lam * (a2 / l2)
    o = _rms_scale(o, g_ref[...]) * (1.0 - lambda_init)
    o_ref[...] = o.astype(o_ref.dtype)


def _diff_attention(proj, lp, subln, *, batch, seq, lambda_init, tq, tk):
    assert tk % tq == 0 and seq % tk == 0
    nq = seq // tq
    width = 2 * HEAD_DIM
    return pl.pallas_call(
        functools.partial(_diff_attn_kernel, tq=tq, tk=tk, lambda_init=lambda_init),
        grid=(batch, DIFF_HEADS, nq),
        in_specs=[
            pl.BlockSpec((tq, width), lambda b, h, i: (b * nq + i, h)),
            pl.BlockSpec((seq, width), lambda b, h, i: (b, DIFF_HEADS + h)),
            pl.BlockSpec((seq, width), lambda b, h, i: (b, 2 * DIFF_HEADS + h)),
            _resident((4, HEAD_DIM)),
            _resident((1, width)),
        ],
        out_specs=pl.BlockSpec((tq, width), lambda b, h, i: (b * nq + i, h)),
        out_shape=jax.ShapeDtypeStruct((batch * seq, DIFF_HEADS * width), BF16),
        compiler_params=_params(3),
        name="diff_attention",
    )(proj, proj, proj, lp, subln.reshape(1, width))


def _dilated_kernel(q_ref, kc_ref, kp_ref, vc_ref, vp_ref, o_ref, st_ref, *, tq, n_steps):
    n = pl.program_id(2)
    hp = pl.program_id(3)
    blk = DIL_BLOCK

    @pl.when(hp == 0)
    def _():
        st_ref[...] = jnp.zeros_like(st_ref)

    qrow = lax.broadcasted_iota(jnp.int32, (blk, 2 * blk), 0)
    kcol = lax.broadcasted_iota(jnp.int32, (blk, 2 * blk), 1)
    dist = qrow + blk - kcol
    band = (dist >= 0) & (dist <= n_steps)
    lane_q = lax.broadcasted_iota(jnp.int32, (blk, LANES), 1)
    lane_s = lax.broadcasted_iota(jnp.int32, (blk, LANES), 1)

    for r, i in [(r, i) for r in range(q_ref.shape[0]) for i in range(tq // blk)]:
        rows = slice(i * blk, (i + 1) * blk)
        q = q_ref[r, rows, :]
        if i == 0:
            k_prev, v_prev = kp_ref[r], vp_ref[r]
            valid = band & ((n > 0) | (kcol >= blk))
        else:
            prev = slice((i - 1) * blk, i * blk)
            k_prev, v_prev = kc_ref[r, prev, :], vc_ref[r, prev, :]
            valid = band
        keys = jnp.concatenate([k_prev, kc_ref[r, rows, :]], axis=0)
        vals = jnp.concatenate([v_prev, vc_ref[r, rows, :]], axis=0)
        out = jnp.zeros((blk, LANES), F32)
        stats = st_ref[r, rows, :]
        for e in range(LANES // HEAD_DIM):
            head_lanes = (lane_q // HEAD_DIM) == e
            qe = jnp.where(head_lanes, q, jnp.zeros_like(q))
            s = lax.dot_general(qe, keys, _NT, preferred_element_type=F32)
            s = jnp.where(valid, s, -jnp.inf)
            m = jnp.max(s, axis=-1, keepdims=True)
            p = jnp.exp(s - m)
            den = jnp.sum(p, axis=-1, keepdims=True)
            oe = jnp.dot((p / den).astype(BF16), vals, preferred_element_type=F32)
            out = jnp.where(head_lanes, oe, out)
            lse = m + jnp.log(den)
            stats = jnp.where(lane_s == hp * (LANES // HEAD_DIM) + e, lse, stats)
        o_ref[r, rows, :] = out.astype(o_ref.dtype)
        st_ref[r, rows, :] = stats


def _dilated_group(q4, k4, v4, *, q_off, k_off, v_off, n_steps):
    b, d, length, _ = q4.shape
    rows_per_step = 1024
    tq = min(rows_per_step, length)
    nres = min(d, rows_per_step // tq)
    assert length % tq == 0 and tq % DIL_BLOCK == 0 and d % nres == 0
    sub = tq // DIL_BLOCK
    pairs = DIL_GROUP_WIDTH // LANES

    def cur(off):
        return pl.BlockSpec((None, nres, tq, LANES), lambda bi, r, n, h: (bi, r, n, off + h))

    def prev(off):
        return pl.BlockSpec((None, nres, DIL_BLOCK, LANES),
                            lambda bi, r, n, h: (bi, r, jnp.maximum(n * sub - 1, 0), off + h))

    return pl.pallas_call(
        functools.partial(_dilated_kernel, tq=tq, n_steps=n_steps),
        grid=(b, d // nres, length // tq, pairs),
        in_specs=[cur(q_off), cur(k_off), prev(k_off), cur(v_off), prev(v_off)],
        out_specs=[pl.BlockSpec((None, nres, tq, LANES), lambda bi, r, n, h: (bi, r, n, h)),
                   pl.BlockSpec((None, nres, tq, LANES), lambda bi, r, n, h: (bi, r, n, 0))],
        out_shape=[jax.ShapeDtypeStruct((b, d, length, DIL_GROUP_WIDTH), BF16),
                   jax.ShapeDtypeStruct((b, d, length, LANES), F32)],
        compiler_params=_params(4),
        name="dilated_attention",
    )(q4, k4, k4, v4, v4)


def _memory_attention(q, k, v):
    lane = lax.broadcasted_iota(jnp.int32, q.shape, 1)
    out = jnp.zeros(q.shape, F32)
    for h in range(MEM_HEADS):
        head_lanes = (lane // HEAD_DIM) == h
        qh = jnp.where(head_lanes, q, jnp.zeros_like(q))
        s = lax.dot_general(qh, k, _NT, preferred_element_type=F32)
        m = jnp.max(s, axis=-1, keepdims=True)
        p = jnp.exp(s - m)
        den = jnp.sum(p, axis=-1, keepdims=True)
        oh = jnp.dot((p / den).astype(BF16), v, preferred_element_type=F32)
        out = jnp.where(head_lanes, oh, out)
    return out


def _split_bf16(x):
    hi = x.astype(BF16)
    lo = (x - hi.astype(F32)).astype(BF16)
    return hi, lo


def _outproj_kernel(*refs, n_groups):
    if n_groups:
        x_ref, qm_ref, mk_ref, mv_ref, w_ref = refs[:5]
        o_refs = refs[5:5 + n_groups]
        st_refs = refs[5 + n_groups:5 + 2 * n_groups]
        ex_ref, out_ref = refs[5 + 2 * n_groups:]
        lses = [r[...] for r in st_refs]
        top = functools.reduce(jnp.maximum, lses)
        es = [jnp.exp(l - top) for l in lses]
        den = functools.reduce(lambda a, b: a + b, es)
        mix = None
        for e, o_ref in zip(es, o_refs):
            hi, lo = _split_bf16(e / den)
            wide = (jnp.dot(hi, ex_ref[...], preferred_element_type=F32)
                    + jnp.dot(lo, ex_ref[...], preferred_element_type=F32))
            term = wide * o_ref[...].astype(F32)
            mix = term if mix is None else mix + term
        mix = mix.astype(BF16)
    else:
        x_ref, qm_ref, mk_ref, mv_ref, w_ref, mix_ref, out_ref = refs
        mix = mix_ref[...]
    mo = _memory_attention(qm_ref[...], mk_ref[...], mv_ref[...]).astype(BF16)
    k_mix = mix.shape[1]
    acc = jnp.dot(mix, w_ref[:k_mix, :], preferred_element_type=F32)
    acc += jnp.dot(mo, w_ref[k_mix:, :], preferred_element_type=F32)
    out_ref[...] = x_ref[...] + acc


def _outproj(x2d, proj, memkv, w_bf16, *, seq, mem_tokens, tm, mix=None, group_out=None,
             group_stats=None, expand=None):
    t, d = x2d.shape
    per_batch = seq // tm
    qm_block = (proj.shape[1] - MEM_Q_WIDTH) // MEM_Q_WIDTH
    row = lambda i: (i, 0)
    in_specs = [pl.BlockSpec((tm, d), row),
                pl.BlockSpec((tm, MEM_Q_WIDTH), lambda i: (i, qm_block)),
                pl.BlockSpec((mem_tokens, MEM_Q_WIDTH), lambda i: (i // per_batch, 0)),
                pl.BlockSpec((mem_tokens, MEM_Q_WIDTH), lambda i: (i // per_batch, 1)),
                _resident(w_bf16.shape)]
    args = [x2d, proj, memkv, memkv, w_bf16]
    if mix is not None:
        n_groups = 0
        in_specs.append(pl.BlockSpec((tm, mix.shape[1]), row))
        args.append(mix)
    else:
        n_groups = len(group_out)
        in_specs += [pl.BlockSpec((tm, DIL_GROUP_WIDTH), row)] * n_groups
        in_specs += [pl.BlockSpec((tm, LANES), row)] * n_groups
        in_specs.append(_resident(expand.shape))
        args += list(group_out) + list(group_stats) + [expand]
    return pl.pallas_call(
        functools.partial(_outproj_kernel, n_groups=n_groups),
        grid=(t // tm,),
        in_specs=in_specs,
        out_specs=pl.BlockSpec((tm, d), row),
        out_shape=jax.ShapeDtypeStruct((t, d), F32),
        compiler_params=_params(1),
        name="mix_outproj",
    )(*args)


def _topk_rows(s, k, payload=None):
    n = s.shape[0]
    row = lax.broadcasted_iota(jnp.int32, s.shape, 0)
    vals, picked = [], []
    for _ in range(k):
        m = jnp.max(s, axis=0, keepdims=True)
        i = jnp.min(jnp.where(s == m, row, n), axis=0, keepdims=True)
        hit = row == i
        vals.append(m)
        if payload is None:
            picked.append(i)
        else:
            picked.append(jnp.sum(jnp.where(hit, payload, 0), axis=0, keepdims=True))
        s = jnp.where(hit, -jnp.inf, s)
    return jnp.concatenate(vals, axis=0), jnp.concatenate(picked, axis=0)


def _peer_route_kernel(x_ref, g_ref, wq_ref, keys_ref, xn_ref, pair_ref, parity_ref, gate_ref):
    xn = _rms_scale(x_ref[...], g_ref[...])
    xn_ref[...] = xn
    q = jnp.dot(xn.astype(BF16), wq_ref[...], preferred_element_type=F32).astype(BF16)
    kk = PEER_TOPK
    for h in range(PEER_HEADS):
        tops = []
        for half in range(2):
            c0 = (2 * h + half) * PEER_HALF
            s = lax.dot_general(keys_ref[2 * h + half], q[:, c0:c0 + PEER_HALF], _NT,
                                preferred_element_type=F32)
            tops.append(_topk_rows(s, kk))
        (s1, i1), (s2, i2) = tops
        half = kk // 2
        sub = lax.broadcasted_iota(jnp.int32, (half, s1.shape[1]), 0)
        blocks = [s1[0:1] + s2]
        experts = [i1[0:1] * PEER_N_KEYS + i2]
        for a in range(1, half):
            blocks.append(jnp.where(sub < kk // (a + 1), s1[a:a + 1] + s2[:half], -jnp.inf))
            experts.append(i1[a:a + 1] * PEER_N_KEYS + i2[:half])
        blocks.append(s1[half:] + s2[0:1])
        experts.append(i1[half:] * PEER_N_KEYS + i2[0:1])
        cand = jnp.concatenate(blocks, axis=0)
        cand_expert = jnp.concatenate(experts, axis=0)
        top_s, expert = _topk_rows(cand, kk, payload=cand_expert)
        e = jnp.exp(top_s - top_s[0:1])
        gate = e / jnp.sum(e, axis=0, keepdims=True)
        rows = slice(h * kk, (h + 1) * kk)
        pair_ref[rows, :] = expert >> 1
        parity_ref[rows, :] = expert & 1
        gate_ref[rows, :] = gate


def _peer_route(x2d, g, wq_bf16, keys_bf16, *, tm):
    t, d = x2d.shape
    col = lambda i: (0, i)
    slot_shape = jax.ShapeDtypeStruct((PEER_SLOTS, t), jnp.int32)
    return pl.pallas_call(
        _peer_route_kernel,
        grid=(t // tm,),
        in_specs=[pl.BlockSpec((tm, d), lambda i: (i, 0)),
                  _resident((1, d)),
                  _resident(wq_bf16.shape),
                  _resident(keys_bf16.shape)],
        out_specs=[pl.BlockSpec((tm, d), lambda i: (i, 0)),
                   pl.BlockSpec((PEER_SLOTS, tm), col),
                   pl.BlockSpec((PEER_SLOTS, tm), col),
                   pl.BlockSpec((PEER_SLOTS, tm), col)],
        out_shape=[jax.ShapeDtypeStruct((t, d), F32), slot_shape, slot_shape,
                   jax.ShapeDtypeStruct((PEER_SLOTS, t), F32)],
        compiler_params=_params(1),
        name="peer_route",
    )(x2d, g.reshape(1, d), wq_bf16, keys_bf16)


_HI_MASK = 0xFFFF0000


def _pack_expert_table(u):
    e, d = u.shape
    bits = lax.bitcast_convert_type(u.astype(BF16), jnp.uint16).astype(jnp.uint32)
    bits = bits.reshape(e // 2, 2, d)
    packed = (bits[:, 0] << 16) | bits[:, 1]
    return packed.reshape(e // 2, d // LANES, LANES)


_BIT_REVERSED = (0, 4, 2, 6, 1, 5, 3, 7)


def _bf16_pair_words(x):
    bits = pltpu.bitcast(x, jnp.uint32)
    top = (bits + jnp.uint32(0x7FFF) + ((bits >> 16) & jnp.uint32(1))) >> 16
    return (top << 16) | top


def _packed_add(a, b):
    return pltpu.bitcast(pltpu.bitcast(a, BF16) + pltpu.bitcast(b, BF16), jnp.uint32)


def _sublane_sums(words, sub):
    level = [words[i] for i in _BIT_REVERSED]
    for k in (4, 2, 1):
        low = (sub & k) == 0
        merged = []
        for a, b in zip(level[0::2], level[1::2]):
            if k == 4:
                merged.append(_packed_add(jnp.where(low, a, b),
                                          pltpu.roll(jnp.where(low, b, a), k, 0)))
            else:
                merged.append(jnp.where(low,
                                        _packed_add(a, pltpu.roll(a, SUBLANES - k, 0)),
                                        _packed_add(b, pltpu.roll(b, k, 0))))
        level = merged
    return level[0]


def _peer_up_kernel(pair_ref, x_ref, tab_ref, gate_ref, par_ref, spread_ref, wexp_ref, w2_ref):
    tm = x_ref.shape[0]
    sub = lax.broadcasted_iota(jnp.int32, (SUBLANES, LANES), 0)
    hi_mask = jnp.uint32(_HI_MASK)

    def token_sums(t, m):
        xx = pltpu.bitcast(_bf16_pair_words(x_ref[t]), BF16)
        even, odd = [], []
        for g in range(PEER_SLOTS // SUBLANES):
            words = []
            for i in range(SUBLANES):
                slab = pltpu.bitcast(tab_ref[pair_ref[t, g * SUBLANES + i]], BF16)
                words.append(pltpu.bitcast(slab * xx, jnp.uint32))
            q = _sublane_sums(words, sub)
            even.append(pltpu.bitcast(q & hi_mask, F32))
            odd.append(pltpu.bitcast(q << 16, F32))
        lane_parts = jnp.concatenate(even + odd, axis=0).astype(BF16)
        onehot = (sub == m).astype(BF16)
        return lax.dot_general(onehot, lane_parts, _NT, preferred_element_type=F32)

    def group(gi, _):
        base = pl.multiple_of(gi * SUBLANES, SUBLANES)
        h2 = token_sums(base, 0)
        for m in range(1, SUBLANES):
            h2 = h2 + token_sums(base + m, m)
        odd_slot = par_ref[pl.ds(base, SUBLANES), :] != 0
        h = jnp.where(odd_slot, h2[:, PEER_SLOTS:], h2[:, :PEER_SLOTS])
        act = 0.5 * h * (1.0 + lax.erf(h * (1.0 / math.sqrt(2.0))))
        w = gate_ref[pl.ds(base, SUBLANES), :] * act
        zero = jnp.zeros_like(w)
        w2_ref[pl.ds(base, SUBLANES), 0:PEER_SLOTS] = jnp.where(odd_slot, zero, w)
        w2_ref[pl.ds(base, SUBLANES), PEER_SLOTS:2 * PEER_SLOTS] = jnp.where(odd_slot, w, zero)
        return 0

    lax.fori_loop(0, tm // SUBLANES, group, 0)
    hi, lo = _split_bf16(w2_ref[...])
    wexp_ref[...] = (jnp.dot(hi, spread_ref[...], preferred_element_type=F32)
                     + jnp.dot(lo, spread_ref[...], preferred_element_type=F32))


PAIR_ROWS = 2 * SUBLANES
DOWN_K = PEER_SLOTS * PAIR_ROWS
DOWN_PAIRS_PER_STEP = 4


def _peer_down_kernel(pair_ref, wexp_ref, diag_ref, x_ref, tab_ref, o_ref):
    tm = x_ref.shape[0]

    def left_rows(t):
        row = wexp_ref[pl.ds(t, 1), :]
        hi, lo = _split_bf16(row * diag_ref[...])
        return [hi, lo]

    def token_pair(ta):
        tb = ta + 1
        rhs = jnp.concatenate(
            [jnp.concatenate([tab_ref[pair_ref[ta, j]], tab_ref[pair_ref[tb, j]]], axis=1)
             for j in range(PEER_SLOTS)], axis=0)
        lhs = jnp.concatenate(left_rows(ta) + left_rows(tb), axis=0)
        out = jnp.dot(lhs, rhs, preferred_element_type=F32)
        s = SUBLANES
        o_ref[ta] = x_ref[ta] + (out[0:s, :LANES] + out[s:2 * s, :LANES])
        o_ref[tb] = x_ref[tb] + (out[2 * s:3 * s, LANES:] + out[3 * s:4 * s, LANES:])

    def step(i, _):
        for k in range(DOWN_PAIRS_PER_STEP):
            token_pair(2 * (DOWN_PAIRS_PER_STEP * i + k))
        return 0

    lax.fori_loop(0, tm // (2 * DOWN_PAIRS_PER_STEP), step, 0)


def _smem_rows(tm):
    return pl.BlockSpec((tm, PEER_SLOTS), lambda i: (i, 0), memory_space=pltpu.SMEM)


def _peer_up(pair, parity, xn3, table, gate, spread, *, tm, n_tokens):
    rows = pl.BlockSpec((tm, PEER_SLOTS), lambda i: (i, 0))
    return pl.pallas_call(
        _peer_up_kernel,
        grid=(n_tokens // tm,),
        in_specs=[_smem_rows(tm),
                  pl.BlockSpec((tm, SUBLANES, LANES), lambda i: (i, 0, 0)),
                  _resident(table.shape), rows, rows, _resident(spread.shape)],
        out_specs=pl.BlockSpec((tm, DOWN_K), lambda i: (i, 0)),
        out_shape=jax.ShapeDtypeStruct((n_tokens, DOWN_K), F32),
        scratch_shapes=[pltpu.VMEM((tm, 2 * PEER_SLOTS), F32)],
        compiler_params=_params(1),
        name="peer_up",
    )(pair, xn3, table, gate, parity, spread)


def _peer_down(pair, wexp, diag, x3, table, *, tm, n_tokens):
    tile = pl.BlockSpec((tm, SUBLANES, LANES), lambda i: (i, 0, 0))
    return pl.pallas_call(
        _peer_down_kernel,
        grid=(n_tokens // tm,),
        in_specs=[_smem_rows(tm), pl.BlockSpec((tm, DOWN_K), lambda i: (i, 0)),
                  _resident(diag.shape), tile, _resident(table.shape)],
        out_specs=tile,
        out_shape=jax.ShapeDtypeStruct((n_tokens,) + x3.shape[1:], F32),
        compiler_params=_params(1),
        name="peer_down",
    )(pair, wexp, diag, x3, table)


SC_WORKERS = 32
SC_LANES = 16
SC_HALF = PEER_SLOTS // 2
SC_ROWS_PER_STEP = 4
SC_CHUNKS = 8


def _sc_word_table(v):
    e, d = v.shape
    bits = lax.bitcast_convert_type(v.astype(BF16), jnp.uint16).astype(jnp.uint32)
    blk = bits.reshape(e, d // (2 * SC_LANES), 2, SC_LANES)
    words = (blk[:, :, 0] << 16) | blk[:, :, 1]
    return lax.bitcast_convert_type(words.reshape(e, d // 2), jnp.int32)


def _sc_peer_up(expert, xn2d, table, *, first, n_tokens):
    t, d = xn2d.shape
    per_worker = n_tokens // SC_WORKERS
    words = d // 2
    mesh = plsc.VectorSubcoreMesh(core_axis_name="c", subcore_axis_name="s")

    @functools.partial(
        pl.kernel, mesh=mesh,
        out_type=jax.ShapeDtypeStruct((n_tokens, PEER_SLOTS), F32),
        scratch_types=[
            pltpu.VMEM((PEER_SLOTS,), jnp.int32),
            pltpu.VMEM((d,), F32),
            pltpu.VMEM((2, SC_HALF, words), jnp.int32),
            pltpu.VMEM((PEER_SLOTS * SC_LANES,), F32),
            pltpu.VMEM((PEER_SLOTS,), F32),
            pltpu.SemaphoreType.DMA,
            pltpu.SemaphoreType.DMA,
        ],
        compiler_params=pltpu.CompilerParams(needs_layout_passes=False),
        name="sc_peer_up",
    )
    def run(expert_hbm, x_hbm, table_hbm, h_hbm, idx_v, x_v, rows_v, part_v, h_v, sem0, sem1):
        worker = lax.axis_index("s") * 2 + lax.axis_index("c")
        local = worker * per_worker
        base = first + local
        lane = lax.iota(jnp.int32, SC_LANES)

        def partial_sums(h):
            @pl.loop(0, SC_HALF, step=SC_ROWS_PER_STEP)
            def _(j0):
                sums = [None] * SC_ROWS_PER_STEP
                for m in range(words // SC_LANES):
                    xa = x_v[pl.ds(2 * m * SC_LANES, SC_LANES)]
                    xb = x_v[pl.ds((2 * m + 1) * SC_LANES, SC_LANES)]
                    for r in range(SC_ROWS_PER_STEP):
                        word = rows_v[h, j0 + r, pl.ds(m * SC_LANES, SC_LANES)]
                        term = (xa * lax.bitcast_convert_type(word & jnp.int32(-65536), F32)
                                + xb * lax.bitcast_convert_type(word << 16, F32))
                        sums[r] = term if sums[r] is None else sums[r] + term
                for r in range(SC_ROWS_PER_STEP):
                    part_v[pl.ds((h * SC_HALF + j0 + r) * SC_LANES, SC_LANES)] = sums[r]

        @pl.loop(0, per_worker)
        def _(i):
            tok = base + i
            pltpu.sync_copy(expert_hbm.at[tok], idx_v)
            first_half = pltpu.async_copy(table_hbm.at[idx_v.at[pl.ds(0, SC_HALF)]],
                                          rows_v.at[0], sem0)
            second_half = pltpu.async_copy(table_hbm.at[idx_v.at[pl.ds(SC_HALF, SC_HALF)]],
                                           rows_v.at[1], sem1)
            pltpu.sync_copy(x_hbm.at[tok], x_v)
            first_half.wait()
            partial_sums(0)
            second_half.wait()
            partial_sums(1)
            for g in range(PEER_SLOTS // SC_LANES):
                total = None
                for k in range(SC_LANES):
                    column = plsc.load_gather(
                        part_v, [g * SC_LANES * SC_LANES + lane * SC_LANES + k])
                    total = column if total is None else total + column
                h_v[pl.ds(g * SC_LANES, SC_LANES)] = total
            pltpu.sync_copy(h_v, h_hbm.at[local + i])

    return run(expert, xn2d, table)


def _peer_act_kernel(h_ref, gate_ref, rep_ref, after_ref, o_ref):
    del after_ref
    h = h_ref[...]
    w = gate_ref[...] * (0.5 * h * (1.0 + lax.erf(h * (1.0 / math.sqrt(2.0)))))
    hi, lo = _split_bf16(w)
    o_ref[...] = (jnp.dot(hi, rep_ref[...], preferred_element_type=F32)
                  + jnp.dot(lo, rep_ref[...], preferred_element_type=F32))


def _peer_act(h, gate, rep, after, *, first, tm):
    n = h.shape[0]
    off = first // tm
    return pl.pallas_call(
        _peer_act_kernel,
        grid=(n // tm,),
        in_specs=[pl.BlockSpec((tm, PEER_SLOTS), lambda i: (i, 0)),
                  pl.BlockSpec((tm, PEER_SLOTS), lambda i: (i + off, 0)),
                  _resident(rep.shape),
                  pl.BlockSpec(memory_space=pl.ANY)],
        out_specs=pl.BlockSpec((tm, PEER_SLOTS * SC_LANES), lambda i: (i, 0)),
        out_shape=jax.ShapeDtypeStruct((n, PEER_SLOTS * SC_LANES), F32),
        compiler_params=_params(1),
        name="peer_act",
    )(h, gate, rep, after)


def _sc_peer_down(expert, w_lanes, x2d, table, *, first, n_tokens):
    t, d = x2d.shape
    per_worker = n_tokens // SC_WORKERS
    words = d // 2
    mesh = plsc.VectorSubcoreMesh(core_axis_name="c", subcore_axis_name="s")

    @functools.partial(
        pl.kernel, mesh=mesh,
        out_type=jax.ShapeDtypeStruct((t, d), F32),
        scratch_types=[
            pltpu.VMEM((PEER_SLOTS,), jnp.int32),
            pltpu.VMEM((PEER_SLOTS * SC_LANES,), F32),
            pltpu.VMEM((d,), F32),
            pltpu.VMEM((2, SC_HALF, words), jnp.int32),
            pltpu.SemaphoreType.DMA,
            pltpu.SemaphoreType.DMA,
        ],
        compiler_params=pltpu.CompilerParams(needs_layout_passes=False),
        name="sc_peer_down",
    )
    def run(expert_hbm, w_hbm, x_hbm, table_hbm, out_hbm, idx_v, w_v, out_v, rows_v, sem0, sem1):
        worker = lax.axis_index("s") * 2 + lax.axis_index("c")
        local = worker * per_worker
        base = first + local

        def accumulate(h):
            @pl.loop(0, SC_HALF, step=SC_ROWS_PER_STEP)
            def _(j0):
                weights = [w_v[pl.ds((h * SC_HALF + j0 + r) * SC_LANES, SC_LANES)]
                           for r in range(SC_ROWS_PER_STEP)]
                for g in range(words // SC_LANES // SC_CHUNKS):
                    loaded = [[rows_v[h, j0 + r, pl.ds((g * SC_CHUNKS + q) * SC_LANES, SC_LANES)]
                               for q in range(SC_CHUNKS)] for r in range(SC_ROWS_PER_STEP)]
                    for q in range(SC_CHUNKS):
                        hi_sum = lo_sum = None
                        for r in range(SC_ROWS_PER_STEP):
                            word = loaded[r][q]
                            hi = weights[r] * lax.bitcast_convert_type(word & jnp.int32(-65536), F32)
                            lo = weights[r] * lax.bitcast_convert_type(word << 16, F32)
                            hi_sum = hi if hi_sum is None else hi_sum + hi
                            lo_sum = lo if lo_sum is None else lo_sum + lo
                        col = (g * SC_CHUNKS + q) * 2 * SC_LANES
                        plsc.addupdate(out_v.at[pl.ds(col, SC_LANES)], hi_sum)
                        plsc.addupdate(out_v.at[pl.ds(col + SC_LANES, SC_LANES)], lo_sum)

        @pl.loop(0, per_worker)
        def _(i):
            tok = base + i
            pltpu.sync_copy(expert_hbm.at[tok], idx_v)
            first_half = pltpu.async_copy(table_hbm.at[idx_v.at[pl.ds(0, SC_HALF)]],
                                          rows_v.at[0], sem0)
            second_half = pltpu.async_copy(table_hbm.at[idx_v.at[pl.ds(SC_HALF, SC_HALF)]],
                                           rows_v.at[1], sem1)
            pltpu.sync_copy(w_hbm.at[local + i], w_v)
            pltpu.sync_copy(x_hbm.at[tok], out_v)
            first_half.wait()
            accumulate(0)
            second_half.wait()
            accumulate(1)
            pltpu.sync_copy(out_v, out_hbm.at[tok])

    return run(expert, w_lanes, x2d, table)


def _peer_constants():
    col = jnp.arange(DOWN_K)
    src = ((col // SUBLANES) % 2) * PEER_SLOTS + col // PAIR_ROWS
    spread = (jnp.arange(2 * PEER_SLOTS)[:, None] == src[None, :]).astype(BF16)
    diag = (col[None, :] % SUBLANES == jnp.arange(SUBLANES)[:, None]).astype(F32)
    lane_repeat = jnp.repeat(jnp.eye(PEER_SLOTS, dtype=BF16), SC_LANES, axis=1)
    return spread, diag, lane_repeat


def _pair_slabs(v):
    e, d = v.shape
    return v.astype(BF16).reshape(e // 2, 2 * d // LANES, LANES)


def _peer(x2d, g, wq_bf16, keys_bf16, u_packed, u_words, v_slabs, v_words, consts, *, tm_route,
          tm_expert, n_sparsecore):
    t, d = x2d.shape
    spread, diag, lane_repeat = consts
    n_tc = t - n_sparsecore
    xn, pair, parity, gate = _peer_route(x2d, g, wq_bf16, keys_bf16, tm=tm_route)
    pair, parity, gate = pair.T, parity.T, gate.T
    xn3 = xn.reshape(t, d // LANES, LANES)
    expert = pair * 2 + parity
    h_sc = _sc_peer_up(expert, xn, u_words, first=n_tc, n_tokens=n_sparsecore)
    wexp = _peer_up(pair, parity, xn3, u_packed, gate, spread, tm=tm_expert, n_tokens=n_tc)
    w_lanes = _peer_act(h_sc, gate, lane_repeat, wexp, first=n_tc, tm=tm_expert)
    out = _sc_peer_down(expert, w_lanes, x2d, v_words, first=n_tc, n_tokens=n_sparsecore)
    out3 = _peer_down(pair, wexp, diag, x2d.reshape(t, d // LANES, LANES), v_slabs, tm=tm_expert,
                      n_tokens=n_tc)
    return lax.dynamic_update_slice(out, out3.reshape(n_tc, d), (0, 0))


def _final_norm_kernel(x_ref, g_ref, o_ref):
    o_ref[...] = _rms_scale(x_ref[...], g_ref[...])


def _final_norm(x2d, g, *, tm):
    t, d = x2d.shape
    return pl.pallas_call(
        _final_norm_kernel,
        grid=(t // tm,),
        in_specs=[pl.BlockSpec((tm, d), lambda i: (i, 0)), _resident((1, d))],
        out_specs=pl.BlockSpec((tm, d), lambda i: (i, 0)),
        out_shape=jax.ShapeDtypeStruct((t, d), F32),
        compiler_params=_params(1),
        name="final_norm",
    )(x2d, g.reshape(1, d))


def _tiles(seq):
    return dict(tm_proj=min(256, seq), tm_out=min(512, seq), tq=min(512, seq), tk=min(1024, seq),
                tm_route=LANES, tm_expert=LANES, tm_norm=min(512, seq))


SC_SHARE = 0.328125


def _sparsecore_tokens(t, tm):
    unit = tm * SC_WORKERS // math.gcd(tm, SC_WORKERS)
    return int(t * SC_SHARE) // unit * unit


def _to_residue_major(a2d, batch, seq, dilation, c0, c1):
    cols = a2d[:, c0:c1].reshape(batch, seq // dilation, dilation, c1 - c0)
    return cols.transpose(0, 2, 1, 3)


def _from_residue_major(a4):
    b, d, length, c = a4.shape
    return a4.transpose(0, 2, 1, 3).reshape(b * d * length, c)


def kernel(x, mem, norm_mix, a_w_in, a_lambda, a_subln, b_w_in, norm_mem, w_mem_kv, w_out,
           norm_ffn, peer_wq, peer_keys, peer_u, peer_v, shared_norm, shared_w_kv, final_norm):
    batch, seq, d = x.shape
    mem_tokens = mem.shape[1]
    depth = norm_mix.shape[0]
    n_a = a_w_in.shape[0]
    t = batch * seq
    tiles = _tiles(seq)
    rope = _rope_tables(seq)
    scale = HEAD_DIM ** -0.5

    diff_qk = DIFF_HEADS * 2 * HEAD_DIM
    dil_qk = len(DIL_GROUPS) * DIL_GROUP_WIDTH
    peer_consts = _peer_constants()
    expand = jnp.repeat(jnp.eye(LANES, DIL_HEADS, dtype=BF16), HEAD_DIM, axis=1)

    x2d = x.reshape(t, d)
    mem2d = mem.reshape(batch * mem_tokens, d)
    shared = None

    for l in range(depth):
        memkv = _proj(mem2d, norm_mem[l], w_mem_kv[l].astype(BF16), tm=mem_tokens)
        w_o = w_out[l].astype(BF16)
        if l < n_a:
            col_scale = jnp.concatenate([jnp.full((diff_qk,), scale, F32),
                                         jnp.ones((a_w_in.shape[2] - diff_qk - MEM_Q_WIDTH,), F32),
                                         jnp.full((MEM_Q_WIDTH,), scale, F32)])
            w_in = (a_w_in[l] * col_scale).astype(BF16)
            proj = _proj(x2d, norm_mix[l], w_in, tm=tiles["tm_proj"], n_rope=2 * diff_qk,
                         rope=rope, seq=seq)
            lambda_init = 0.8 - 0.6 * math.exp(-0.3 * l)
            mix = _diff_attention(proj, a_lambda[l], a_subln[l], batch=batch, seq=seq,
                                  lambda_init=lambda_init, tq=tiles["tq"], tk=tiles["tk"])
            x2d = _outproj(x2d, proj, memkv, w_o, seq=seq, mem_tokens=mem_tokens,
                           tm=tiles["tm_out"], mix=mix)
        else:
            w_in = (b_w_in[l - n_a] * scale).astype(BF16)
            proj = _proj(x2d, norm_mix[l], w_in, tm=tiles["tm_proj"], n_rope=dil_qk,
                         rope=rope, seq=seq)
            outs, stats = [], []
            for gi, (window, dilation) in enumerate(DIL_GROUPS):
                c0 = gi * DIL_GROUP_WIDTH
                blk0 = c0 // LANES
                if dilation == 1:
                    q4, q_off = proj.reshape(batch, 1, seq, proj.shape[1]), blk0
                else:
                    q4, q_off = _to_residue_major(proj, batch, seq, dilation,
                                                  c0, c0 + DIL_GROUP_WIDTH), 0
                k4, k_off, v4, v_off = shared[gi]
                o4, st4 = _dilated_group(q4, k4, v4, q_off=q_off, k_off=k_off, v_off=v_off,
                                         n_steps=window // dilation)
                outs.append(_from_residue_major(o4))
                stats.append(_from_residue_major(st4))
            x2d = _outproj(x2d, proj, memkv, w_o, seq=seq, mem_tokens=mem_tokens,
                           tm=tiles["tm_out"], group_out=outs, group_stats=stats, expand=expand)

        x2d = _peer(x2d, norm_ffn[l], peer_wq[l].astype(BF16),
                    peer_keys[l].reshape(2 * PEER_HEADS, PEER_N_KEYS, PEER_HALF).astype(BF16),
                    _pack_expert_table(peer_u[l]), _sc_word_table(peer_u[l]),
                    _pair_slabs(peer_v[l]), _sc_word_table(peer_v[l]), peer_consts,
                    tm_route=tiles["tm_route"], tm_expert=tiles["tm_expert"],
                    n_sparsecore=_sparsecore_tokens(t, tiles["tm_expert"]))

        if l == n_a - 1:
            kv = _proj(x2d, shared_norm, shared_w_kv.astype(BF16), tm=tiles["tm_proj"],
                       n_rope=dil_qk, rope=rope, seq=seq)
            shared = []
            for gi, (window, dilation) in enumerate(DIL_GROUPS):
                c0 = gi * DIL_GROUP_WIDTH
                if dilation == 1:
                    kv4 = kv.reshape(batch, 1, seq, kv.shape[1])
                    shared.append((kv4, c0 // LANES, kv4, (dil_qk + c0) // LANES))
                else:
                    k4 = _to_residue_major(kv, batch, seq, dilation, c0, c0 + DIL_GROUP_WIDTH)
                    v4 = _to_residue_major(kv, batch, seq, dilation, dil_qk + c0,
                                           dil_qk + c0 + DIL_GROUP_WIDTH)
                    shared.append((k4, 0, v4, 0))

    return _final_norm(x2d, final_norm, tm=tiles["tm_norm"]).reshape(batch, seq, d)
```

```python
import functools
import math

import jax
import jax.numpy as jnp
from jax import lax
from jax.experimental import pallas as pl
from jax.experimental.pallas import tpu as pltpu
from jax.experimental.pallas import tpu_sc as plsc

F32 = jnp.float32
BF16 = jnp.bfloat16

HEAD_DIM = 64
ROPE_DIMS = HEAD_DIM // 4
ROPE_HALF = ROPE_DIMS // 2
ROPE_THETA = 500000.0
NORM_EPS = 1e-5
DIFF_HEADS = 6
MEM_HEADS = 4
MEM_Q_WIDTH = MEM_HEADS * HEAD_DIM
DIL_GROUPS = ((128, 1), (512, 4), (2048, 16))
DIL_HEADS = 12
DIL_BLOCK = 128
DIL_GROUP_WIDTH = DIL_HEADS * HEAD_DIM
PEER_HEADS = 8
PEER_N_KEYS = 128
PEER_TOPK = 16
PEER_HALF = 128
PEER_SLOTS = PEER_HEADS * PEER_TOPK

LANES = 128
SUBLANES = 8
VMEM_LIMIT_BYTES = 56 * 1024 * 1024

_NT = (((1,), (1,)), ((), ()))


def _params(n_axes):
    return pltpu.CompilerParams(dimension_semantics=("arbitrary",) * n_axes,
                                vmem_limit_bytes=VMEM_LIMIT_BYTES)


def _resident(shape):
    zeros = (0,) * len(shape)
    return pl.BlockSpec(shape, lambda *_: zeros, pipeline_mode=pl.Buffered(1))


def _rms_scale(x, g):
    ms = jnp.mean(x * x, axis=-1, keepdims=True)
    return x * lax.rsqrt(ms + NORM_EPS) * g


def _rope_tables(seq):
    inv = ROPE_THETA ** (-jnp.arange(0, ROPE_DIMS, 2, dtype=F32) / ROPE_DIMS)
    ang = jnp.arange(seq, dtype=F32)[:, None] * inv[None, :]
    cos, sin = jnp.cos(ang), jnp.sin(ang)
    ones = jnp.ones((seq, HEAD_DIM - ROPE_DIMS), F32)
    zeros = jnp.zeros((seq, HEAD_DIM - ROPE_DIMS), F32)
    zh = jnp.zeros((seq, ROPE_HALF), F32)
    c = jnp.concatenate([cos, cos, ones], axis=1)
    sa = jnp.concatenate([-sin, zh, zeros], axis=1)
    sb = jnp.concatenate([zh, sin, zeros], axis=1)
    rep = LANES // HEAD_DIM
    return tuple(jnp.tile(t, (1, rep)) for t in (c, sa, sb))


def _proj_kernel(*refs, n_rope, chunk):
    if n_rope:
        x_ref, g_ref, w_ref, c_ref, sa_ref, sb_ref, o_ref = refs
    else:
        x_ref, g_ref, w_ref, o_ref = refs
    y = _rms_scale(x_ref[...], g_ref[...]).astype(BF16)
    n = o_ref.shape[1]
    for c0 in range(0, n, chunk):
        acc = jnp.dot(y, w_ref[:, c0:c0 + chunk], preferred_element_type=F32)
        if c0 < n_rope:
            for k0 in range(0, chunk, LANES):
                a = acc[:, k0:k0 + LANES]
                a = (a * c_ref[...]
                     + pltpu.roll(a, LANES - ROPE_HALF, 1) * sa_ref[...]
                     + pltpu.roll(a, ROPE_HALF, 1) * sb_ref[...])
                o_ref[:, c0 + k0:c0 + k0 + LANES] = a.astype(o_ref.dtype)
        else:
            o_ref[:, c0:c0 + chunk] = acc.astype(o_ref.dtype)


def _proj(x2d, g, w_bf16, *, tm, n_rope=0, rope=None, seq=None):
    t, d = x2d.shape
    n = w_bf16.shape[1]
    chunk = 256
    assert t % tm == 0 and n % chunk == 0 and n_rope % chunk == 0
    in_specs = [pl.BlockSpec((tm, d), lambda i: (i, 0)),
                _resident((1, d)),
                _resident((d, n))]
    args = [x2d, g.reshape(1, d), w_bf16]
    if n_rope:
        nblk = seq // tm
        spec = pl.BlockSpec((tm, LANES), lambda i: (i % nblk, 0))
        in_specs += [spec, spec, spec]
        args += list(rope)
    return pl.pallas_call(
        functools.partial(_proj_kernel, n_rope=n_rope, chunk=chunk),
        grid=(t // tm,),
        in_specs=in_specs,
        out_specs=pl.BlockSpec((tm, n), lambda i: (i, 0)),
        out_shape=jax.ShapeDtypeStruct((t, n), BF16),
        compiler_params=_params(1),
        name="norm_proj",
    )(*args)


def _online_softmax_step(q, k, v, carry, mask):
    m, l, acc = carry
    s = lax.dot_general(q, k, _NT, preferred_element_type=F32)
    if mask is not None:
        s = jnp.where(mask, s, -jnp.inf)
    m_new = jnp.maximum(m, jnp.max(s, axis=-1, keepdims=True))
    alpha = jnp.exp(m - m_new)
    p = jnp.exp(s - m_new)
    l = alpha * l + jnp.sum(p, axis=-1, keepdims=True)
    acc = alpha * acc + jnp.dot(p.astype(BF16), v, preferred_element_type=F32)
    return m_new, l, acc


def _diff_attn_kernel(q_ref, k_ref, v_ref, lp_ref, g_ref, o_ref, *, tq, tk, lambda_init):
    qi = pl.program_id(2)
    q = q_ref[...]
    lane = lax.broadcasted_iota(jnp.int32, q.shape, 1)
    zero = jnp.zeros_like(q)
    qa = jnp.where(lane < HEAD_DIM, q, zero)
    qb = jnp.where(lane >= HEAD_DIM, q, zero)

    def init():
        return (jnp.full((tq, 1), -jnp.inf, F32), jnp.zeros((tq, 1), F32),
                jnp.zeros((tq, 2 * HEAD_DIM), F32))

    def body(j, carry):
        c1, c2 = carry
        k = k_ref[pl.ds(pl.multiple_of(j * tk, tk), tk), :]
        v = v_ref[pl.ds(pl.multiple_of(j * tk, tk), tk), :]
        return (_online_softmax_step(qa, k, v, c1, None),
                _online_softmax_step(qb, k, v, c2, None))

    n_full = (qi * tq) // tk
    c1, c2 = lax.fori_loop(0, n_full, body, (init(), init()))
    row = qi * tq + lax.broadcasted_iota(jnp.int32, (tq, tk), 0)
    col = n_full * tk + lax.broadcasted_iota(jnp.int32, (tq, tk), 1)
    causal = col <= row
    start = pl.multiple_of(n_full * tk, tk)
    k = k_ref[pl.ds(start, tk), :]
    v = v_ref[pl.ds(start, tk), :]
    m1, l1, a1 = _online_softmax_step(qa, k, v, c1, causal)
    m2, l2, a2 = _online_softmax_step(qb, k, v, c2, causal)

    lp = lp_ref[...]
    lam = (jnp.exp(jnp.sum(lp[0:1] * lp[1:2], axis=-1, keepdims=True))
           - jnp.exp(jnp.sum(lp[2:3] * lp[3:4], axis=-1, keepdims=True)) + lambda_init)
    o = a1 / l1 - lam * (a2 / l2)
    o = _rms_scale(o, g_ref[...]) * (1.0 - lambda_init)
    o_ref[...] = o.astype(o_ref.dtype)


def _diff_attention(proj, lp, subln, *, batch, seq, lambda_init, tq, tk):
    assert tk % tq == 0 and seq % tk == 0
    nq = seq // tq
    width = 2 * HEAD_DIM
    return pl.pallas_call(
        functools.partial(_diff_attn_kernel, tq=tq, tk=tk, lambda_init=lambda_init),
        grid=(batch, DIFF_HEADS, nq),
        in_specs=[
            pl.BlockSpec((tq, width), lambda b, h, i: (b * nq + i, h)),
            pl.BlockSpec((seq, width), lambda b, h, i: (b, DIFF_HEADS + h)),
            pl.BlockSpec((seq, width), lambda b, h, i: (b, 2 * DIFF_HEADS + h)),
            _resident((4, HEAD_DIM)),
            _resident((1, width)),
        ],
        out_specs=pl.BlockSpec((tq, width), lambda b, h, i: (b * nq + i, h)),
        out_shape=jax.ShapeDtypeStruct((batch * seq, DIFF_HEADS * width), BF16),
        compiler_params=_params(3),
        name="diff_attention",
    )(proj, proj, proj, lp, subln.reshape(1, width))


def _dilated_kernel(q_ref, kc_ref, kp_ref, vc_ref, vp_ref, o_ref, st_ref, *, tq, n_steps):
    n = pl.program_id(2)
    hp = pl.program_id(3)
    blk = DIL_BLOCK

    @pl.when(hp == 0)
    def _():
        st_ref[...] = jnp.zeros_like(st_ref)

    qrow = lax.broadcasted_iota(jnp.int32, (blk, 2 * blk), 0)
    kcol = lax.broadcasted_iota(jnp.int32, (blk, 2 * blk), 1)
    dist = qrow + blk - kcol
    band = (dist >= 0) & (dist <= n_steps)
    lane_q = lax.broadcasted_iota(jnp.int32, (blk, LANES), 1)
    lane_s = lax.broadcasted_iota(jnp.int32, (blk, LANES), 1)

    for r, i in [(r, i) for r in range(q_ref.shape[0]) for i in range(tq // blk)]:
        rows = slice(i * blk, (i + 1) * blk)
        q = q_ref[r, rows, :]
        if i == 0:
            k_prev, v_prev = kp_ref[r], vp_ref[r]
            valid = band & ((n > 0) | (kcol >= blk))
        else:
            prev = slice((i - 1) * blk, i * blk)
            k_prev, v_prev = kc_ref[r, prev, :], vc_ref[r, prev, :]
            valid = band
        keys = jnp.concatenate([k_prev, kc_ref[r, rows, :]], axis=0)
        vals = jnp.concatenate([v_prev, vc_ref[r, rows, :]], axis=0)
        out = jnp.zeros((blk, LANES), F32)
        stats = st_ref[r, rows, :]
        for e in range(LANES // HEAD_DIM):
            head_lanes = (lane_q // HEAD_DIM) == e
            qe = jnp.where(head_lanes, q, jnp.zeros_like(q))
            s = lax.dot_general(qe, keys, _NT, preferred_element_type=F32)
            s = jnp.where(valid, s, -jnp.inf)
            m = jnp.max(s, axis=-1, keepdims=True)
            p = jnp.exp(s - m)
            den = jnp.sum(p, axis=-1, keepdims=True)
            oe = jnp.dot((p / den).astype(BF16), vals, preferred_element_type=F32)
            out = jnp.where(head_lanes, oe, out)
            lse = m + jnp.log(den)
            stats = jnp.where(lane_s == hp * (LANES // HEAD_DIM) + e, lse, stats)
        o_ref[r, rows, :] = out.astype(o_ref.dtype)
        st_ref[r, rows, :] = stats


def _dilated_group(q4, k4, v4, *, q_off, k_off, v_off, n_steps):
    b, d, length, _ = q4.shape
    rows_per_step = 1024
    tq = min(rows_per_step, length)
    nres = min(d, rows_per_step // tq)
    assert length % tq == 0 and tq % DIL_BLOCK == 0 and d % nres == 0
    sub = tq // DIL_BLOCK
    pairs = DIL_GROUP_WIDTH // LANES

    def cur(off):
        return pl.BlockSpec((None, nres, tq, LANES), lambda bi, r, n, h: (bi, r, n, off + h))

    def prev(off):
        return pl.BlockSpec((None, nres, DIL_BLOCK, LANES),
                            lambda bi, r, n, h: (bi, r, jnp.maximum(n * sub - 1, 0), off + h))

    return pl.pallas_call(
        functools.partial(_dilated_kernel, tq=tq, n_steps=n_steps),
        grid=(b, d // nres, length // tq, pairs),
        in_specs=[cur(q_off), cur(k_off), prev(k_off), cur(v_off), prev(v_off)],
        out_specs=[pl.BlockSpec((None, nres, tq, LANES), lambda bi, r, n, h: (bi, r, n, h)),
                   pl.BlockSpec((None, nres, tq, LANES), lambda bi, r, n, h: (bi, r, n, 0))],
        out_shape=[jax.ShapeDtypeStruct((b, d, length, DIL_GROUP_WIDTH), BF16),
                   jax.ShapeDtypeStruct((b, d, length, LANES), F32)],
        compiler_params=_params(4),
        name="dilated_attention",
    )(q4, k4, k4, v4, v4)


def _memory_attention(q, k, v):
    lane = lax.broadcasted_iota(jnp.int32, q.shape, 1)
    out = jnp.zeros(q.shape, F32)
    for h in range(MEM_HEADS):
        head_lanes = (lane // HEAD_DIM) == h
        qh = jnp.where(head_lanes, q, jnp.zeros_like(q))
        s = lax.dot_general(qh, k, _NT, preferred_element_type=F32)
        m = jnp.max(s, axis=-1, keepdims=True)
        p = jnp.exp(s - m)
        den = jnp.sum(p, axis=-1, keepdims=True)
        oh = jnp.dot((p / den).astype(BF16), v, preferred_element_type=F32)
        out = jnp.where(head_lanes, oh, out)
    return out


def _split_bf16(x):
    hi = x.astype(BF16)
    lo = (x - hi.astype(F32)).astype(BF16)
    return hi, lo


def _outproj_kernel(*refs, n_groups):
    if n_groups:
        x_ref, qm_ref, mk_ref, mv_ref, w_ref = refs[:5]
        o_refs = refs[5:5 + n_groups]
        st_refs = refs[5 + n_groups:5 + 2 * n_groups]
        ex_ref, out_ref = refs[5 + 2 * n_groups:]
        lses = [r[...] for r in st_refs]
        top = functools.reduce(jnp.maximum, lses)
        es = [jnp.exp(l - top) for l in lses]
        den = functools.reduce(lambda a, b: a + b, es)
        mix = None
        for e, o_ref in zip(es, o_refs):
            hi, lo = _split_bf16(e / den)
            wide = (jnp.dot(hi, ex_ref[...], preferred_element_type=F32)
                    + jnp.dot(lo, ex_ref[...], preferred_element_type=F32))
            term = wide * o_ref[...].astype(F32)
            mix = term if mix is None else mix + term
        mix = mix.astype(BF16)
    else:
        x_ref, qm_ref, mk_ref, mv_ref, w_ref, mix_ref, out_ref = refs
        mix = mix_ref[...]
    mo = _memory_attention(qm_ref[...], mk_ref[...], mv_ref[...]).astype(BF16)
    k_mix = mix.shape[1]
    acc = jnp.dot(mix, w_ref[:k_mix, :], preferred_element_type=F32)
    acc += jnp.dot(mo, w_ref[k_mix:, :], preferred_element_type=F32)
    out_ref[...] = x_ref[...] + acc


def _outproj(x2d, proj, memkv, w_bf16, *, seq, mem_tokens, tm, mix=None, group_out=None,
             group_stats=None, expand=None):
    t, d = x2d.shape
    per_batch = seq // tm
    qm_block = (proj.shape[1] - MEM_Q_WIDTH) // MEM_Q_WIDTH
    row = lambda i: (i, 0)
    in_specs = [pl.BlockSpec((tm, d), row),
                pl.BlockSpec((tm, MEM_Q_WIDTH), lambda i: (i, qm_block)),
                pl.BlockSpec((mem_tokens, MEM_Q_WIDTH), lambda i: (i // per_batch, 0)),
                pl.BlockSpec((mem_tokens, MEM_Q_WIDTH), lambda i: (i // per_batch, 1)),
                _resident(w_bf16.shape)]
    args = [x2d, proj, memkv, memkv, w_bf16]
    if mix is not None:
        n_groups = 0
        in_specs.append(pl.BlockSpec((tm, mix.shape[1]), row))
        args.append(mix)
    else:
        n_groups = len(group_out)
        in_specs += [pl.BlockSpec((tm, DIL_GROUP_WIDTH), row)] * n_groups
        in_specs += [pl.BlockSpec((tm, LANES), row)] * n_groups
        in_specs.append(_resident(expand.shape))
        args += list(group_out) + list(group_stats) + [expand]
    return pl.pallas_call(
        functools.partial(_outproj_kernel, n_groups=n_groups),
        grid=(t // tm,),
        in_specs=in_specs,
        out_specs=pl.BlockSpec((tm, d), row),
        out_shape=jax.ShapeDtypeStruct((t, d), F32),
        compiler_params=_params(1),
        name="mix_outproj",
    )(*args)


def _topk_rows(s, k, payload=None):
    n = s.shape[0]
    row = lax.broadcasted_iota(jnp.int32, s.shape, 0)
    vals, picked = [], []
    for _ in range(k):
        m = jnp.max(s, axis=0, keepdims=True)
        i = jnp.min(jnp.where(s == m, row, n), axis=0, keepdims=True)
        hit = row == i
        vals.append(m)
        if payload is None:
            picked.append(i)
        else:
            picked.append(jnp.sum(jnp.where(hit, payload, 0), axis=0, keepdims=True))
        s = jnp.where(hit, -jnp.inf, s)
    return jnp.concatenate(vals, axis=0), jnp.concatenate(picked, axis=0)


def _peer_route_kernel(x_ref, g_ref, wq_ref, keys_ref, xn_ref, pair_ref, parity_ref, gate_ref):
    xn = _rms_scale(x_ref[...], g_ref[...])
    xn_ref[...] = xn
    q = jnp.dot(xn.astype(BF16), wq_ref[...], preferred_element_type=F32).astype(BF16)
    kk = PEER_TOPK
    for h in range(PEER_HEADS):
        tops = []
        for half in range(2):
            c0 = (2 * h + half) * PEER_HALF
            s = lax.dot_general(keys_ref[2 * h + half], q[:, c0:c0 + PEER_HALF], _NT,
                                preferred_element_type=F32)
            tops.append(_topk_rows(s, kk))
        (s1, i1), (s2, i2) = tops
        half = kk // 2
        sub = lax.broadcasted_iota(jnp.int32, (half, s1.shape[1]), 0)
        blocks = [s1[0:1] + s2]
        experts = [i1[0:1] * PEER_N_KEYS + i2]
        for a in range(1, half):
            blocks.append(jnp.where(sub < kk // (a + 1), s1[a:a + 1] + s2[:half], -jnp.inf))
            experts.append(i1[a:a + 1] * PEER_N_KEYS + i2[:half])
        blocks.append(s1[half:] + s2[0:1])
        experts.append(i1[half:] * PEER_N_KEYS + i2[0:1])
        cand = jnp.concatenate(blocks, axis=0)
        cand_expert = jnp.concatenate(experts, axis=0)
        top_s, expert = _topk_rows(cand, kk, payload=cand_expert)
        e = jnp.exp(top_s - top_s[0:1])
        gate = e / jnp.sum(e, axis=0, keepdims=True)
        rows = slice(h * kk, (h + 1) * kk)
        pair_ref[rows, :] = expert >> 1
        parity_ref[rows, :] = expert & 1
        gate_ref[rows, :] = gate


def _peer_route(x2d, g, wq_bf16, keys_bf16, *, tm):
    t, d = x2d.shape
    col = lambda i: (0, i)
    slot_shape = jax.ShapeDtypeStruct((PEER_SLOTS, t), jnp.int32)
    return pl.pallas_call(
        _peer_route_kernel,
        grid=(t // tm,),
        in_specs=[pl.BlockSpec((tm, d), lambda i: (i, 0)),
                  _resident((1, d)),
                  _resident(wq_bf16.shape),
                  _resident(keys_bf16.shape)],
        out_specs=[pl.BlockSpec((tm, d), lambda i: (i, 0)),
                   pl.BlockSpec((PEER_SLOTS, tm), col),
                   pl.BlockSpec((PEER_SLOTS, tm), col),
                   pl.BlockSpec((PEER_SLOTS, tm), col)],
        out_shape=[jax.ShapeDtypeStruct((t, d), F32), slot_shape, slot_shape,
                   jax.ShapeDtypeStruct((PEER_SLOTS, t), F32)],
        compiler_params=_params(1),
        name="peer_route",
    )(x2d, g.reshape(1, d), wq_bf16, keys_bf16)


_HI_MASK = 0xFFFF0000


def _pack_expert_table(u):
    e, d = u.shape
    bits = lax.bitcast_convert_type(u.astype(BF16), jnp.uint16).astype(jnp.uint32)
    bits = bits.reshape(e // 2, 2, d)
    packed = (bits[:, 0] << 16) | bits[:, 1]
    return packed.reshape(e // 2, d // LANES, LANES)


_BIT_REVERSED = (0, 4, 2, 6, 1, 5, 3, 7)


def _bf16_pair_words(x):
    bits = pltpu.bitcast(x, jnp.uint32)
    top = (bits + jnp.uint32(0x7FFF) + ((bits >> 16) & jnp.uint32(1))) >> 16
    return (top << 16) | top


def _packed_add(a, b):
    return pltpu.bitcast(pltpu.bitcast(a, BF16) + pltpu.bitcast(b, BF16), jnp.uint32)


def _sublane_sums(words, sub):
    level = [words[i] for i in _BIT_REVERSED]
    for k in (4, 2, 1):
        low = (sub & k) == 0
        merged = []
        for a, b in zip(level[0::2], level[1::2]):
            if k == 4:
                merged.append(_packed_add(jnp.where(low, a, b),
                                          pltpu.roll(jnp.where(low, b, a), k, 0)))
            else:
                merged.append(jnp.where(low,
                                        _packed_add(a, pltpu.roll(a, SUBLANES - k, 0)),
                                        _packed_add(b, pltpu.roll(b, k, 0))))
        level = merged
    return level[0]


def _peer_up_kernel(pair_ref, x_ref, tab_ref, gate_ref, par_ref, spread_ref, wexp_ref, w2_ref):
    tm = x_ref.shape[0]
    sub = lax.broadcasted_iota(jnp.int32, (SUBLANES, LANES), 0)
    hi_mask = jnp.uint32(_HI_MASK)

    def token_sums(t, m):
        xx = pltpu.bitcast(_bf16_pair_words(x_ref[t]), BF16)
        even, odd = [], []
        for g in range(PEER_SLOTS // SUBLANES):
            words = []
            for i in range(SUBLANES):
                slab = pltpu.bitcast(tab_ref[pair_ref[t, g * SUBLANES + i]], BF16)
                words.append(pltpu.bitcast(slab * xx, jnp.uint32))
            q = _sublane_sums(words, sub)
            even.append(pltpu.bitcast(q & hi_mask, F32))
            odd.append(pltpu.bitcast(q << 16, F32))
        lane_parts = jnp.concatenate(even + odd, axis=0).astype(BF16)
        onehot = (sub == m).astype(BF16)
        return lax.dot_general(onehot, lane_parts, _NT, preferred_element_type=F32)

    def group(gi, _):
        base = pl.multiple_of(gi * SUBLANES, SUBLANES)
        h2 = token_sums(base, 0)
        for m in range(1, SUBLANES):
            h2 = h2 + token_sums(base + m, m)
        odd_slot = par_ref[pl.ds(base, SUBLANES), :] != 0
        h = jnp.where(odd_slot, h2[:, PEER_SLOTS:], h2[:, :PEER_SLOTS])
        act = 0.5 * h * (1.0 + lax.erf(h * (1.0 / math.sqrt(2.0))))
        w = gate_ref[pl.ds(base, SUBLANES), :] * act
        zero = jnp.zeros_like(w)
        w2_ref[pl.ds(base, SUBLANES), 0:PEER_SLOTS] = jnp.where(odd_slot, zero, w)
        w2_ref[pl.ds(base, SUBLANES), PEER_SLOTS:2 * PEER_SLOTS] = jnp.where(odd_slot, w, zero)
        return 0

    lax.fori_loop(0, tm // SUBLANES, group, 0)
    hi, lo = _split_bf16(w2_ref[...])
    wexp_ref[...] = (jnp.dot(hi, spread_ref[...], preferred_element_type=F32)
                     + jnp.dot(lo, spread_ref[...], preferred_element_type=F32))


PAIR_ROWS = 2 * SUBLANES
DOWN_K = PEER_SLOTS * PAIR_ROWS
DOWN_PAIRS_PER_STEP = 4


def _peer_down_kernel(pair_ref, wexp_ref, diag_ref, x_ref, tab_ref, o_ref):
    tm = x_ref.shape[0]

    def left_rows(t):
        row = wexp_ref[pl.ds(t, 1), :]
        hi, lo = _split_bf16(row * diag_ref[...])
        return [hi, lo]

    def token_pair(ta):
        tb = ta + 1
        rhs = jnp.concatenate(
            [jnp.concatenate([tab_ref[pair_ref[ta, j]], tab_ref[pair_ref[tb, j]]], axis=1)
             for j in range(PEER_SLOTS)], axis=0)
        lhs = jnp.concatenate(left_rows(ta) + left_rows(tb), axis=0)
        out = jnp.dot(lhs, rhs, preferred_element_type=F32)
        s = SUBLANES
        o_ref[ta] = x_ref[ta] + (out[0:s, :LANES] + out[s:2 * s, :LANES])
        o_ref[tb] = x_ref[tb] + (out[2 * s:3 * s, LANES:] + out[3 * s:4 * s, LANES:])

    def step(i, _):
        for k in range(DOWN_PAIRS_PER_STEP):
            token_pair(2 * (DOWN_PAIRS_PER_STEP * i + k))
        return 0

    lax.fori_loop(0, tm // (2 * DOWN_PAIRS_PER_STEP), step, 0)


def _smem_rows(tm):
    return pl.BlockSpec((tm, PEER_SLOTS), lambda i: (i, 0), memory_space=pltpu.SMEM)


def _peer_up(pair, parity, xn3, table, gate, spread, *, tm, n_tokens):
    rows = pl.BlockSpec((tm, PEER_SLOTS), lambda i: (i, 0))
    return pl.pallas_call(
        _peer_up_kernel,
        grid=(n_tokens // tm,),
        in_specs=[_smem_rows(tm),
                  pl.BlockSpec((tm, SUBLANES, LANES), lambda i: (i, 0, 0)),
                  _resident(table.shape), rows, rows, _resident(spread.shape)],
        out_specs=pl.BlockSpec((tm, DOWN_K), lambda i: (i, 0)),
        out_shape=jax.ShapeDtypeStruct((n_tokens, DOWN_K), F32),
        scratch_shapes=[pltpu.VMEM((tm, 2 * PEER_SLOTS), F32)],
        compiler_params=_params(1),
        name="peer_up",
    )(pair, xn3, table, gate, parity, spread)


def _peer_down(pair, wexp, diag, x3, table, *, tm, n_tokens):
    tile = pl.BlockSpec((tm, SUBLANES, LANES), lambda i: (i, 0, 0))
    return pl.pallas_call(
        _peer_down_kernel,
        grid=(n_tokens // tm,),
        in_specs=[_smem_rows(tm), pl.BlockSpec((tm, DOWN_K), lambda i: (i, 0)),
                  _resident(diag.shape), tile, _resident(table.shape)],
        out_specs=tile,
        out_shape=jax.ShapeDtypeStruct((n_tokens,) + x3.shape[1:], F32),
        compiler_params=_params(1),
        name="peer_down",
    )(pair, wexp, diag, x3, table)


SC_WORKERS = 32
SC_LANES = 16
SC_HALF = PEER_SLOTS // 2
SC_ROWS_PER_STEP = 4
SC_CHUNKS = 8


SC_TABLE_ROWS = 64


def _sc_word_table(v):
    e, d = v.shape
    per_worker = e // SC_WORKERS
    words = d // 2
    mesh = plsc.VectorSubcoreMesh(core_axis_name="c", subcore_axis_name="s")

    @functools.partial(
        pl.kernel, mesh=mesh,
        out_type=jax.ShapeDtypeStruct((e, words), jnp.int32),
        scratch_types=[pltpu.VMEM((SC_TABLE_ROWS, d), F32),
                       pltpu.VMEM((SC_TABLE_ROWS, words), jnp.int32)],
        compiler_params=pltpu.CompilerParams(needs_layout_passes=False),
        name="sc_word_table",
    )
    def run(v_hbm, out_hbm, in_v, out_v):
        worker = lax.axis_index("s") * 2 + lax.axis_index("c")
        base = worker * per_worker

        def bf16_bits(x):
            bits = lax.bitcast_convert_type(x, jnp.int32)
            lsb = lax.shift_right_logical(bits, 16) & 1
            rounded = (bits + 0x7FFF + lsb) & jnp.int32(-65536)
            subnormal = (bits & jnp.int32(0x7F800000)) == 0
            return jnp.where(subnormal, bits & jnp.int32(-2147483648), rounded)

        @pl.loop(0, per_worker, step=SC_TABLE_ROWS)
        def _(r0):
            pltpu.sync_copy(v_hbm.at[pl.ds(base + r0, SC_TABLE_ROWS)], in_v)

            @pl.loop(0, SC_TABLE_ROWS)
            def _(r):
                for m in range(words // SC_LANES):
                    hi = bf16_bits(in_v[r, pl.ds(2 * m * SC_LANES, SC_LANES)])
                    lo = bf16_bits(in_v[r, pl.ds((2 * m + 1) * SC_LANES, SC_LANES)])
                    out_v[r, pl.ds(m * SC_LANES, SC_LANES)] = hi | lax.shift_right_logical(lo, 16)

            pltpu.sync_copy(out_v, out_hbm.at[pl.ds(base + r0, SC_TABLE_ROWS)])

    return run(v)


def _sc_peer_up(expert, xn2d, table, *, first, n_tokens):
    t, d = xn2d.shape
    per_worker = n_tokens // SC_WORKERS
    words = d // 2
    mesh = plsc.VectorSubcoreMesh(core_axis_name="c", subcore_axis_name="s")

    @functools.partial(
        pl.kernel, mesh=mesh,
        out_type=jax.ShapeDtypeStruct((n_tokens, PEER_SLOTS), F32),
        scratch_types=[
            pltpu.VMEM((PEER_SLOTS,), jnp.int32),
            pltpu.VMEM((d,), F32),
            pltpu.VMEM((2, SC_HALF, words), jnp.int32),
            pltpu.VMEM((PEER_SLOTS * SC_LANES,), F32),
            pltpu.VMEM((PEER_SLOTS,), F32),
            pltpu.SemaphoreType.DMA,
            pltpu.SemaphoreType.DMA,
        ],
        compiler_params=pltpu.CompilerParams(needs_layout_passes=False),
        name="sc_peer_up",
    )
    def run(expert_hbm, x_hbm, table_hbm, h_hbm, idx_v, x_v, rows_v, part_v, h_v, sem0, sem1):
        worker = lax.axis_index("s") * 2 + lax.axis_index("c")
        local = worker * per_worker
        base = first + local
        lane = lax.iota(jnp.int32, SC_LANES)

        def partial_sums(h):
            @pl.loop(0, SC_HALF, step=SC_ROWS_PER_STEP)
            def _(j0):
                sums = [None] * SC_ROWS_PER_STEP
                for m in range(words // SC_LANES):
                    xa = x_v[pl.ds(2 * m * SC_LANES, SC_LANES)]
                    xb = x_v[pl.ds((2 * m + 1) * SC_LANES, SC_LANES)]
                    for r in range(SC_ROWS_PER_STEP):
                        word = rows_v[h, j0 + r, pl.ds(m * SC_LANES, SC_LANES)]
                        term = (xa * lax.bitcast_convert_type(word & jnp.int32(-65536), F32)
                                + xb * lax.bitcast_convert_type(word << 16, F32))
                        sums[r] = term if sums[r] is None else sums[r] + term
                for r in range(SC_ROWS_PER_STEP):
                    part_v[pl.ds((h * SC_HALF + j0 + r) * SC_LANES, SC_LANES)] = sums[r]

        @pl.loop(0, per_worker)
        def _(i):
            tok = base + i
            pltpu.sync_copy(expert_hbm.at[tok], idx_v)
            first_half = pltpu.async_copy(table_hbm.at[idx_v.at[pl.ds(0, SC_HALF)]],
                                          rows_v.at[0], sem0)
            second_half = pltpu.async_copy(table_hbm.at[idx_v.at[pl.ds(SC_HALF, SC_HALF)]],
                                           rows_v.at[1], sem1)
            pltpu.sync_copy(x_hbm.at[tok], x_v)
            first_half.wait()
            partial_sums(0)
            second_half.wait()
            partial_sums(1)
            for g in range(PEER_SLOTS // SC_LANES):
                total = None
                for k in range(SC_LANES):
                    column = plsc.load_gather(
                        part_v, [g * SC_LANES * SC_LANES + lane * SC_LANES + k])
                    total = column if total is None else total + column
                h_v[pl.ds(g * SC_LANES, SC_LANES)] = total
            pltpu.sync_copy(h_v, h_hbm.at[local + i])

    return run(expert, xn2d, table)


def _peer_act_kernel(h_ref, gate_ref, rep_ref, after_ref, o_ref):
    del after_ref
    h = h_ref[...]
    w = gate_ref[...] * (0.5 * h * (1.0 + lax.erf(h * (1.0 / math.sqrt(2.0)))))
    hi, lo = _split_bf16(w)
    o_ref[...] = (jnp.dot(hi, rep_ref[...], preferred_element_type=F32)
                  + jnp.dot(lo, rep_ref[...], preferred_element_type=F32))


def _peer_act(h, gate, rep, after, *, first, tm):
    n = h.shape[0]
    off = first // tm
    return pl.pallas_call(
        _peer_act_kernel,
        grid=(n // tm,),
        in_specs=[pl.BlockSpec((tm, PEER_SLOTS), lambda i: (i, 0)),
                  pl.BlockSpec((tm, PEER_SLOTS), lambda i: (i + off, 0)),
                  _resident(rep.shape),
                  pl.BlockSpec(memory_space=pl.ANY)],
        out_specs=pl.BlockSpec((tm, PEER_SLOTS * SC_LANES), lambda i: (i, 0)),
        out_shape=jax.ShapeDtypeStruct((n, PEER_SLOTS * SC_LANES), F32),
        compiler_params=_params(1),
        name="peer_act",
    )(h, gate, rep, after)


def _sc_peer_down(expert, w_lanes, x2d, table, *, first, n_tokens):
    t, d = x2d.shape
    per_worker = n_tokens // SC_WORKERS
    words = d // 2
    mesh = plsc.VectorSubcoreMesh(core_axis_name="c", subcore_axis_name="s")

    @functools.partial(
        pl.kernel, mesh=mesh,
        out_type=jax.ShapeDtypeStruct((t, d), F32),
        scratch_types=[
            pltpu.VMEM((PEER_SLOTS,), jnp.int32),
            pltpu.VMEM((PEER_SLOTS * SC_LANES,), F32),
            pltpu.VMEM((d,), F32),
            pltpu.VMEM((2, SC_HALF, words), jnp.int32),
            pltpu.SemaphoreType.DMA,
            pltpu.SemaphoreType.DMA,
        ],
        compiler_params=pltpu.CompilerParams(needs_layout_passes=False),
        name="sc_peer_down",
    )
    def run(expert_hbm, w_hbm, x_hbm, table_hbm, out_hbm, idx_v, w_v, out_v, rows_v, sem0, sem1):
        worker = lax.axis_index("s") * 2 + lax.axis_index("c")
        local = worker * per_worker
        base = first + local

        def accumulate(h):
            @pl.loop(0, SC_HALF, step=SC_ROWS_PER_STEP)
            def _(j0):
                weights = [w_v[pl.ds((h * SC_HALF + j0 + r) * SC_LANES, SC_LANES)]
                           for r in range(SC_ROWS_PER_STEP)]
                for g in range(words // SC_LANES // SC_CHUNKS):
                    loaded = [[rows_v[h, j0 + r, pl.ds((g * SC_CHUNKS + q) * SC_LANES, SC_LANES)]
                               for q in range(SC_CHUNKS)] for r in range(SC_ROWS_PER_STEP)]
                    for q in range(SC_CHUNKS):
                        hi_sum = lo_sum = None
                        for r in range(SC_ROWS_PER_STEP):
                            word = loaded[r][q]
                            hi = weights[r] * lax.bitcast_convert_type(word & jnp.int32(-65536), F32)
                            lo = weights[r] * lax.bitcast_convert_type(word << 16, F32)
                            hi_sum = hi if hi_sum is None else hi_sum + hi
                            lo_sum = lo if lo_sum is None else lo_sum + lo
                        col = (g * SC_CHUNKS + q) * 2 * SC_LANES
                        plsc.addupdate(out_v.at[pl.ds(col, SC_LANES)], hi_sum)
                        plsc.addupdate(out_v.at[pl.ds(col + SC_LANES, SC_LANES)], lo_sum)

        @pl.loop(0, per_worker)
        def _(i):
            tok = base + i
            pltpu.sync_copy(expert_hbm.at[tok], idx_v)
            first_half = pltpu.async_copy(table_hbm.at[idx_v.at[pl.ds(0, SC_HALF)]],
                                          rows_v.at[0], sem0)
            second_half = pltpu.async_copy(table_hbm.at[idx_v.at[pl.ds(SC_HALF, SC_HALF)]],
                                           rows_v.at[1], sem1)
            pltpu.sync_copy(w_hbm.at[local + i], w_v)
            pltpu.sync_copy(x_hbm.at[tok], out_v)
            first_half.wait()
            accumulate(0)
            second_half.wait()
            accumulate(1)
            pltpu.sync_copy(out_v, out_hbm.at[tok])

    return run(expert, w_lanes, x2d, table)


def _peer_constants():
    col = jnp.arange(DOWN_K)
    src = ((col // SUBLANES) % 2) * PEER_SLOTS + col // PAIR_ROWS
    spread = (jnp.arange(2 * PEER_SLOTS)[:, None] == src[None, :]).astype(BF16)
    diag = (col[None, :] % SUBLANES == jnp.arange(SUBLANES)[:, None]).astype(F32)
    lane_repeat = jnp.repeat(jnp.eye(PEER_SLOTS, dtype=BF16), SC_LANES, axis=1)
    return spread, diag, lane_repeat


def _pair_slabs(v):
    e, d = v.shape
    return v.astype(BF16).reshape(e // 2, 2 * d // LANES, LANES)


def _peer(x2d, g, wq_bf16, keys_bf16, u_packed, u_words, v_slabs, v_words, consts, *, tm_route,
          tm_expert, n_sparsecore):
    t, d = x2d.shape
    spread, diag, lane_repeat = consts
    n_tc = t - n_sparsecore
    xn, pair, parity, gate = _peer_route(x2d, g, wq_bf16, keys_bf16, tm=tm_route)
    pair, parity, gate = pair.T, parity.T, gate.T
    xn3 = xn.reshape(t, d // LANES, LANES)
    expert = pair * 2 + parity
    h_sc = _sc_peer_up(expert, xn, u_words, first=n_tc, n_tokens=n_sparsecore)
    wexp = _peer_up(pair, parity, xn3, u_packed, gate, spread, tm=tm_expert, n_tokens=n_tc)
    w_lanes = _peer_act(h_sc, gate, lane_repeat, wexp, first=n_tc, tm=tm_expert)
    out = _sc_peer_down(expert, w_lanes, x2d, v_words, first=n_tc, n_tokens=n_sparsecore)
    out3 = _peer_down(pair, wexp, diag, x2d.reshape(t, d // LANES, LANES), v_slabs, tm=tm_expert,
                      n_tokens=n_tc)
    return lax.dynamic_update_slice(out, out3.reshape(n_tc, d), (0, 0))


def _final_norm_kernel(x_ref, g_ref, o_ref):
    o_ref[...] = _rms_scale(x_ref[...], g_ref[...])


def _final_norm(x2d, g, *, tm):
    t, d = x2d.shape
    return pl.pallas_call(
        _final_norm_kernel,
        grid=(t // tm,),
        in_specs=[pl.BlockSpec((tm, d), lambda i: (i, 0)), _resident((1, d))],
        out_specs=pl.BlockSpec((tm, d), lambda i: (i, 0)),
        out_shape=jax.ShapeDtypeStruct((t, d), F32),
        compiler_params=_params(1),
        name="final_norm",
    )(x2d, g.reshape(1, d))


def _tiles(seq):
    return dict(tm_proj=min(256, seq), tm_out=min(512, seq), tq=min(512, seq), tk=min(1024, seq),
                tm_route=LANES, tm_expert=LANES, tm_norm=min(512, seq))


SC_SHARE = 0.328125


def _sparsecore_tokens(t, tm):
    unit = tm * SC_WORKERS // math.gcd(tm, SC_WORKERS)
    return int(t * SC_SHARE) // unit * unit


def _to_residue_major(a2d, batch, seq, dilation, c0, c1):
    cols = a2d[:, c0:c1].reshape(batch, seq // dilation, dilation, c1 - c0)
    return cols.transpose(0, 2, 1, 3)


def _from_residue_major(a4):
    b, d, length, c = a4.shape
    return a4.transpose(0, 2, 1, 3).reshape(b * d * length, c)


def kernel(x, mem, norm_mix, a_w_in, a_lambda, a_subln, b_w_in, norm_mem, w_mem_kv, w_out,
           norm_ffn, peer_wq, peer_keys, peer_u, peer_v, shared_norm, shared_w_kv, final_norm):
    batch, seq, d = x.shape
    mem_tokens = mem.shape[1]
    depth = norm_mix.shape[0]
    n_a = a_w_in.shape[0]
    t = batch * seq
    tiles = _tiles(seq)
    rope = _rope_tables(seq)
    scale = HEAD_DIM ** -0.5

    diff_qk = DIFF_HEADS * 2 * HEAD_DIM
    dil_qk = len(DIL_GROUPS) * DIL_GROUP_WIDTH
    peer_consts = _peer_constants()
    expand = jnp.repeat(jnp.eye(LANES, DIL_HEADS, dtype=BF16), HEAD_DIM, axis=1)

    x2d = x.reshape(t, d)
    mem2d = mem.reshape(batch * mem_tokens, d)
    shared = None

    for l in range(depth):
        memkv = _proj(mem2d, norm_mem[l], w_mem_kv[l].astype(BF16), tm=mem_tokens)
        w_o = w_out[l].astype(BF16)
        if l < n_a:
            col_scale = jnp.concatenate([jnp.full((diff_qk,), scale, F32),
                                         jnp.ones((a_w_in.shape[2] - diff_qk - MEM_Q_WIDTH,), F32),
                                         jnp.full((MEM_Q_WIDTH,), scale, F32)])
            w_in = (a_w_in[l] * col_scale).astype(BF16)
            proj = _proj(x2d, norm_mix[l], w_in, tm=tiles["tm_proj"], n_rope=2 * diff_qk,
                         rope=rope, seq=seq)
            lambda_init = 0.8 - 0.6 * math.exp(-0.3 * l)
            mix = _diff_attention(proj, a_lambda[l], a_subln[l], batch=batch, seq=seq,
                                  lambda_init=lambda_init, tq=tiles["tq"], tk=tiles["tk"])
            x2d = _outproj(x2d, proj, memkv, w_o, seq=seq, mem_tokens=mem_tokens,
                           tm=tiles["tm_out"], mix=mix)
        else:
            w_in = (b_w_in[l - n_a] * scale).astype(BF16)
            proj = _proj(x2d, norm_mix[l], w_in, tm=tiles["tm_proj"], n_rope=dil_qk,
                         rope=rope, seq=seq)
            outs, stats = [], []
            for gi, (window, dilation) in enumerate(DIL_GROUPS):
                c0 = gi * DIL_GROUP_WIDTH
                blk0 = c0 // LANES
                if dilation == 1:
                    q4, q_off = proj.reshape(batch, 1, seq, proj.shape[1]), blk0
                else:
                    q4, q_off = _to_residue_major(proj, batch, seq, dilation,
                                                  c0, c0 + DIL_GROUP_WIDTH), 0
                k4, k_off, v4, v_off = shared[gi]
                o4, st4 = _dilated_group(q4, k4, v4, q_off=q_off, k_off=k_off, v_off=v_off,
                                         n_steps=window // dilation)
                outs.append(_from_residue_major(o4))
                stats.append(_from_residue_major(st4))
            x2d = _outproj(x2d, proj, memkv, w_o, seq=seq, mem_tokens=mem_tokens,
                           tm=tiles["tm_out"], group_out=outs, group_stats=stats, expand=expand)

        x2d = _peer(x2d, norm_ffn[l], peer_wq[l].astype(BF16),
                    peer_keys[l].reshape(2 * PEER_HEADS, PEER_N_KEYS, PEER_HALF).astype(BF16),
                    _pack_expert_table(peer_u[l]), _sc_word_table(peer_u[l]),
                    _pair_slabs(peer_v[l]), _sc_word_table(peer_v[l]), peer_consts,
                    tm_route=tiles["tm_route"], tm_expert=tiles["tm_expert"],
                    n_sparsecore=_sparsecore_tokens(t, tiles["tm_expert"]))

        if l == n_a - 1:
            kv = _proj(x2d, shared_norm, shared_w_kv.astype(BF16), tm=tiles["tm_proj"],
                       n_rope=dil_qk, rope=rope, seq=seq)
            shared = []
            for gi, (window, dilation) in enumerate(DIL_GROUPS):
                c0 = gi * DIL_GROUP_WIDTH
                if dilation == 1:
                    kv4 = kv.reshape(batch, 1, seq, kv.shape[1])
                    shared.append((kv4, c0 // LANES, kv4, (dil_qk + c0) // LANES))
                else:
                    k4 = _to_residue_major(kv, batch, seq, dilation, c0, c0 + DIL_GROUP_WIDTH)
                    v4 = _to_residue_major(kv, batch, seq, dilation, dil_qk + c0,
                                           dil_qk + c0 + DIL_GROUP_WIDTH)
                    shared.append((k4, 0, v4, 0))

    return _final_norm(x2d, final_norm, tm=tiles["tm_norm"]).reshape(batch, seq, d)
```

```python
import functools
import math

import jax
import jax.numpy as jnp
from jax import lax
from jax.experimental import pallas as pl
from jax.experimental.pallas import tpu as pltpu
from jax.experimental.pallas import tpu_sc as plsc

F32 = jnp.float32
BF16 = jnp.bfloat16

HEAD_DIM = 64
ROPE_DIMS = HEAD_DIM // 4
ROPE_HALF = ROPE_DIMS // 2
ROPE_THETA = 500000.0
NORM_EPS = 1e-5
DIFF_HEADS = 6
MEM_HEADS = 4
MEM_Q_WIDTH = MEM_HEADS * HEAD_DIM
DIL_GROUPS = ((128, 1), (512, 4), (2048, 16))
DIL_HEADS = 12
DIL_BLOCK = 128
DIL_GROUP_WIDTH = DIL_HEADS * HEAD_DIM
PEER_HEADS = 8
PEER_N_KEYS = 128
PEER_TOPK = 16
PEER_HALF = 128
PEER_SLOTS = PEER_HEADS * PEER_TOPK

LANES = 128
SUBLANES = 8
VMEM_LIMIT_BYTES = 56 * 1024 * 1024

_NT = (((1,), (1,)), ((), ()))


def _params(n_axes):
    return pltpu.CompilerParams(dimension_semantics=("arbitrary",) * n_axes,
                                vmem_limit_bytes=VMEM_LIMIT_BYTES)


def _resident(shape):
    zeros = (0,) * len(shape)
    return pl.BlockSpec(shape, lambda *_: zeros, pipeline_mode=pl.Buffered(1))


def _rms_scale(x, g):
    ms = jnp.mean(x * x, axis=-1, keepdims=True)
    return x * lax.rsqrt(ms + NORM_EPS) * g


def _rope_tables(seq):
    inv = ROPE_THETA ** (-jnp.arange(0, ROPE_DIMS, 2, dtype=F32) / ROPE_DIMS)
    ang = jnp.arange(seq, dtype=F32)[:, None] * inv[None, :]
    cos, sin = jnp.cos(ang), jnp.sin(ang)
    ones = jnp.ones((seq, HEAD_DIM - ROPE_DIMS), F32)
    zeros = jnp.zeros((seq, HEAD_DIM - ROPE_DIMS), F32)
    zh = jnp.zeros((seq, ROPE_HALF), F32)
    c = jnp.concatenate([cos, cos, ones], axis=1)
    sa = jnp.concatenate([-sin, zh, zeros], axis=1)
    sb = jnp.concatenate([zh, sin, zeros], axis=1)
    rep = LANES // HEAD_DIM
    return tuple(jnp.tile(t, (1, rep)) for t in (c, sa, sb))


def _proj_kernel(*refs, n_rope, chunk):
    if n_rope:
        x_ref, g_ref, w_ref, c_ref, sa_ref, sb_ref, o_ref = refs
    else:
        x_ref, g_ref, w_ref, o_ref = refs
    y = _rms_scale(x_ref[...], g_ref[...]).astype(BF16)
    n = o_ref.shape[1]
    for c0 in range(0, n, chunk):
        acc = jnp.dot(y, w_ref[:, c0:c0 + chunk], preferred_element_type=F32)
        if c0 < n_rope:
            for k0 in range(0, chunk, LANES):
                a = acc[:, k0:k0 + LANES]
                a = (a * c_ref[...]
                     + pltpu.roll(a, LANES - ROPE_HALF, 1) * sa_ref[...]
                     + pltpu.roll(a, ROPE_HALF, 1) * sb_ref[...])
                o_ref[:, c0 + k0:c0 + k0 + LANES] = a.astype(o_ref.dtype)
        else:
            o_ref[:, c0:c0 + chunk] = acc.astype(o_ref.dtype)


def _proj(x2d, g, w_bf16, *, tm, n_rope=0, rope=None, seq=None):
    t, d = x2d.shape
    n = w_bf16.shape[1]
    chunk = 256
    assert t % tm == 0 and n % chunk == 0 and n_rope % chunk == 0
    in_specs = [pl.BlockSpec((tm, d), lambda i: (i, 0)),
                _resident((1, d)),
                _resident((d, n))]
    args = [x2d, g.reshape(1, d), w_bf16]
    if n_rope:
        nblk = seq // tm
        spec = pl.BlockSpec((tm, LANES), lambda i: (i % nblk, 0))
        in_specs += [spec, spec, spec]
        args += list(rope)
    return pl.pallas_call(
        functools.partial(_proj_kernel, n_rope=n_rope, chunk=chunk),
        grid=(t // tm,),
        in_specs=in_specs,
        out_specs=pl.BlockSpec((tm, n), lambda i: (i, 0)),
        out_shape=jax.ShapeDtypeStruct((t, n), BF16),
        compiler_params=_params(1),
        name="norm_proj",
    )(*args)


def _online_softmax_step(q, k, v, carry, mask):
    m, l, acc = carry
    s = lax.dot_general(q, k, _NT, preferred_element_type=F32)
    if mask is not None:
        s = jnp.where(mask, s, -jnp.inf)
    m_new = jnp.maximum(m, jnp.max(s, axis=-1, keepdims=True))
    alpha = jnp.exp(m - m_new)
    p = jnp.exp(s - m_new)
    l = alpha * l + jnp.sum(p, axis=-1, keepdims=True)
    acc = alpha * acc + jnp.dot(p.astype(BF16), v, preferred_element_type=F32)
    return m_new, l, acc


def _diff_attn_kernel(q_ref, k_ref, v_ref, lp_ref, g_ref, o_ref, *, tq, tk, lambda_init):
    qi = pl.program_id(2)
    q = q_ref[...]
    lane = lax.broadcasted_iota(jnp.int32, q.shape, 1)
    zero = jnp.zeros_like(q)
    qa = jnp.where(lane < HEAD_DIM, q, zero)
    qb = jnp.where(lane >= HEAD_DIM, q, zero)

    def init():
        return (jnp.full((tq, 1), -jnp.inf, F32), jnp.zeros((tq, 1), F32),
                jnp.zeros((tq, 2 * HEAD_DIM), F32))

    def body(j, carry):
        c1, c2 = carry
        k = k_ref[pl.ds(pl.multiple_of(j * tk, tk), tk), :]
        v = v_ref[pl.ds(pl.multiple_of(j * tk, tk), tk), :]
        return (_online_softmax_step(qa, k, v, c1, None),
                _online_softmax_step(qb, k, v, c2, None))

    n_full = (qi * tq) // tk
    c1, c2 = lax.fori_loop(0, n_full, body, (init(), init()))
    row = qi * tq + lax.broadcasted_iota(jnp.int32, (tq, tk), 0)
    col = n_full * tk + lax.broadcasted_iota(jnp.int32, (tq, tk), 1)
    causal = col <= row
    start = pl.multiple_of(n_full * tk, tk)
    k = k_ref[pl.ds(start, tk), :]
    v = v_ref[pl.ds(start, tk), :]
    m1, l1, a1 = _online_softmax_step(qa, k, v, c1, causal)
    m2, l2, a2 = _online_softmax_step(qb, k, v, c2, causal)

    lp = lp_ref[...]
    lam = (jnp.exp(jnp.sum(lp[0:1] * lp[1:2], axis=-1, keepdims=True))
           - jnp.exp(jnp.sum(lp[2:3] * lp[3:4], axis=-1, keepdims=True)) + lambda_init)
    o = a1 / l1 - lam * (a2 / l2)
    o = _rms_scale(o, g_ref[...]) * (1.0 - lambda_init)
    o_ref[...] = o.astype(o_ref.dtype)


def _diff_attention(proj, lp, subln, *, batch, seq, lambda_init, tq, tk):
    assert tk % tq == 0 and seq % tk == 0
    nq = seq // tq
    width = 2 * HEAD_DIM
    return pl.pallas_call(
        functools.partial(_diff_attn_kernel, tq=tq, tk=tk, lambda_init=lambda_init),
        grid=(batch, DIFF_HEADS, nq),
        in_specs=[
            pl.BlockSpec((tq, width), lambda b, h, i: (b * nq + i, h)),
            pl.BlockSpec((seq, width), lambda b, h, i: (b, DIFF_HEADS + h)),
            pl.BlockSpec((seq, width), lambda b, h, i: (b, 2 * DIFF_HEADS + h)),
            _resident((4, HEAD_DIM)),
            _resident((1, width)),
        ],
        out_specs=pl.BlockSpec((tq, width), lambda b, h, i: (b * nq + i, h)),
        out_shape=jax.ShapeDtypeStruct((batch * seq, DIFF_HEADS * width), BF16),
        compiler_params=_params(3),
        name="diff_attention",
    )(proj, proj, proj, lp, subln.reshape(1, width))


def _dilated_kernel(q_ref, kc_ref, kp_ref, vc_ref, vp_ref, o_ref, st_ref, *, tq, n_steps):
    n = pl.program_id(2)
    hp = pl.program_id(3)
    blk = DIL_BLOCK

    @pl.when(hp == 0)
    def _():
        st_ref[...] = jnp.zeros_like(st_ref)

    qrow = lax.broadcasted_iota(jnp.int32, (blk, 2 * blk), 0)
    kcol = lax.broadcasted_iota(jnp.int32, (blk, 2 * blk), 1)
    dist = qrow + blk - kcol
    band = (dist >= 0) & (dist <= n_steps)
    lane_q = lax.broadcasted_iota(jnp.int32, (blk, LANES), 1)
    lane_s = lax.broadcasted_iota(jnp.int32, (blk, LANES), 1)

    for r, i in [(r, i) for r in range(q_ref.shape[0]) for i in range(tq // blk)]:
        rows = slice(i * blk, (i + 1) * blk)
        q = q_ref[r, rows, :]
        if i == 0:
            k_prev, v_prev = kp_ref[r], vp_ref[r]
            valid = band & ((n > 0) | (kcol >= blk))
        else:
            prev = slice((i - 1) * blk, i * blk)
            k_prev, v_prev = kc_ref[r, prev, :], vc_ref[r, prev, :]
            valid = band
        keys = jnp.concatenate([k_prev, kc_ref[r, rows, :]], axis=0)
        vals = jnp.concatenate([v_prev, vc_ref[r, rows, :]], axis=0)
        out = jnp.zeros((blk, LANES), F32)
        stats = st_ref[r, rows, :]
        for e in range(LANES // HEAD_DIM):
            head_lanes = (lane_q // HEAD_DIM) == e
            qe = jnp.where(head_lanes, q, jnp.zeros_like(q))
            s = lax.dot_general(qe, keys, _NT, preferred_element_type=F32)
            s = jnp.where(valid, s, -jnp.inf)
            m = jnp.max(s, axis=-1, keepdims=True)
            p = jnp.exp(s - m)
            den = jnp.sum(p, axis=-1, keepdims=True)
            oe = jnp.dot((p / den).astype(BF16), vals, preferred_element_type=F32)
            out = jnp.where(head_lanes, oe, out)
            lse = m + jnp.log(den)
            stats = jnp.where(lane_s == hp * (LANES // HEAD_DIM) + e, lse, stats)
        o_ref[r, rows, :] = out.astype(o_ref.dtype)
        st_ref[r, rows, :] = stats


def _dilated_group(q4, k4, v4, *, q_off, k_off, v_off, n_steps):
    b, d, length, _ = q4.shape
    rows_per_step = 1024
    tq = min(rows_per_step, length)
    nres = min(d, rows_per_step // tq)
    assert length % tq == 0 and tq % DIL_BLOCK == 0 and d % nres == 0
    sub = tq // DIL_BLOCK
    pairs = DIL_GROUP_WIDTH // LANES

    def cur(off):
        return pl.BlockSpec((None, nres, tq, LANES), lambda bi, r, n, h: (bi, r, n, off + h))

    def prev(off):
        return pl.BlockSpec((None, nres, DIL_BLOCK, LANES),
                            lambda bi, r, n, h: (bi, r, jnp.maximum(n * sub - 1, 0), off + h))

    return pl.pallas_call(
        functools.partial(_dilated_kernel, tq=tq, n_steps=n_steps),
        grid=(b, d // nres, length // tq, pairs),
        in_specs=[cur(q_off), cur(k_off), prev(k_off), cur(v_off), prev(v_off)],
        out_specs=[pl.BlockSpec((None, nres, tq, LANES), lambda bi, r, n, h: (bi, r, n, h)),
                   pl.BlockSpec((None, nres, tq, LANES), lambda bi, r, n, h: (bi, r, n, 0))],
        out_shape=[jax.ShapeDtypeStruct((b, d, length, DIL_GROUP_WIDTH), BF16),
                   jax.ShapeDtypeStruct((b, d, length, LANES), F32)],
        compiler_params=_params(4),
        name="dilated_attention",
    )(q4, k4, k4, v4, v4)


def _memory_attention(q, k, v):
    lane = lax.broadcasted_iota(jnp.int32, q.shape, 1)
    out = jnp.zeros(q.shape, F32)
    for h in range(MEM_HEADS):
        head_lanes = (lane // HEAD_DIM) == h
        qh = jnp.where(head_lanes, q, jnp.zeros_like(q))
        s = lax.dot_general(qh, k, _NT, preferred_element_type=F32)
        m = jnp.max(s, axis=-1, keepdims=True)
        p = jnp.exp(s - m)
        den = jnp.sum(p, axis=-1, keepdims=True)
        oh = jnp.dot((p / den).astype(BF16), v, preferred_element_type=F32)
        out = jnp.where(head_lanes, oh, out)
    return out


def _split_bf16(x):
    hi = x.astype(BF16)
    lo = (x - hi.astype(F32)).astype(BF16)
    return hi, lo


def _outproj_kernel(*refs, n_groups):
    if n_groups:
        x_ref, qm_ref, mk_ref, mv_ref, w_ref = refs[:5]
        o_refs = refs[5:5 + n_groups]
        st_refs = refs[5 + n_groups:5 + 2 * n_groups]
        ex_ref, out_ref = refs[5 + 2 * n_groups:]
        lses = [r[...] for r in st_refs]
        top = functools.reduce(jnp.maximum, lses)
        es = [jnp.exp(l - top) for l in lses]
        den = functools.reduce(lambda a, b: a + b, es)
        mix = None
        for e, o_ref in zip(es, o_refs):
            hi, lo = _split_bf16(e / den)
            wide = (jnp.dot(hi, ex_ref[...], preferred_element_type=F32)
                    + jnp.dot(lo, ex_ref[...], preferred_element_type=F32))
            term = wide * o_ref[...].astype(F32)
            mix = term if mix is None else mix + term
        mix = mix.astype(BF16)
    else:
        x_ref, qm_ref, mk_ref, mv_ref, w_ref, mix_ref, out_ref = refs
        mix = mix_ref[...]
    mo = _memory_attention(qm_ref[...], mk_ref[...], mv_ref[...]).astype(BF16)
    k_mix = mix.shape[1]
    acc = jnp.dot(mix, w_ref[:k_mix, :], preferred_element_type=F32)
    acc += jnp.dot(mo, w_ref[k_mix:, :], preferred_element_type=F32)
    out_ref[...] = x_ref[...] + acc


def _outproj(x2d, proj, memkv, w_bf16, *, seq, mem_tokens, tm, mix=None, group_out=None,
             group_stats=None, expand=None):
    t, d = x2d.shape
    per_batch = seq // tm
    qm_block = (proj.shape[1] - MEM_Q_WIDTH) // MEM_Q_WIDTH
    row = lambda i: (i, 0)
    in_specs = [pl.BlockSpec((tm, d), row),
                pl.BlockSpec((tm, MEM_Q_WIDTH), lambda i: (i, qm_block)),
                pl.BlockSpec((mem_tokens, MEM_Q_WIDTH), lambda i: (i // per_batch, 0)),
                pl.BlockSpec((mem_tokens, MEM_Q_WIDTH), lambda i: (i // per_batch, 1)),
                _resident(w_bf16.shape)]
    args = [x2d, proj, memkv, memkv, w_bf16]
    if mix is not None:
        n_groups = 0
        in_specs.append(pl.BlockSpec((tm, mix.shape[1]), row))
        args.append(mix)
    else:
        n_groups = len(group_out)
        in_specs += [pl.BlockSpec((tm, DIL_GROUP_WIDTH), row)] * n_groups
        in_specs += [pl.BlockSpec((tm, LANES), row)] * n_groups
        in_specs.append(_resident(expand.shape))
        args += list(group_out) + list(group_stats) + [expand]
    return pl.pallas_call(
        functools.partial(_outproj_kernel, n_groups=n_groups),
        grid=(t // tm,),
        in_specs=in_specs,
        out_specs=pl.BlockSpec((tm, d), row),
        out_shape=jax.ShapeDtypeStruct((t, d), F32),
        compiler_params=_params(1),
        name="mix_outproj",
    )(*args)


def _topk_rows(s, k, payload=None):
    n = s.shape[0]
    row = lax.broadcasted_iota(jnp.int32, s.shape, 0)
    vals, picked = [], []
    for _ in range(k):
        m = jnp.max(s, axis=0, keepdims=True)
        i = jnp.min(jnp.where(s == m, row, n), axis=0, keepdims=True)
        hit = row == i
        vals.append(m)
        if payload is None:
            picked.append(i)
        else:
            picked.append(jnp.sum(jnp.where(hit, payload, 0), axis=0, keepdims=True))
        s = jnp.where(hit, -jnp.inf, s)
    return jnp.concatenate(vals, axis=0), jnp.concatenate(picked, axis=0)


def _peer_route_kernel(x_ref, g_ref, wq_ref, keys_ref, xn_ref, pair_ref, parity_ref, gate_ref):
    xn = _rms_scale(x_ref[...], g_ref[...])
    xn_ref[...] = xn
    q = jnp.dot(xn.astype(BF16), wq_ref[...], preferred_element_type=F32).astype(BF16)
    kk = PEER_TOPK
    for h in range(PEER_HEADS):
        tops = []
        for half in range(2):
            c0 = (2 * h + half) * PEER_HALF
            s = lax.dot_general(keys_ref[2 * h + half], q[:, c0:c0 + PEER_HALF], _NT,
                                preferred_element_type=F32)
            tops.append(_topk_rows(s, kk))
        (s1, i1), (s2, i2) = tops
        half = kk // 2
        sub = lax.broadcasted_iota(jnp.int32, (half, s1.shape[1]), 0)
        blocks = [s1[0:1] + s2]
        experts = [i1[0:1] * PEER_N_KEYS + i2]
        for a in range(1, half):
            blocks.append(jnp.where(sub < kk // (a + 1), s1[a:a + 1] + s2[:half], -jnp.inf))
            experts.append(i1[a:a + 1] * PEER_N_KEYS + i2[:half])
        blocks.append(s1[half:] + s2[0:1])
        experts.append(i1[half:] * PEER_N_KEYS + i2[0:1])
        cand = jnp.concatenate(blocks, axis=0)
        cand_expert = jnp.concatenate(experts, axis=0)
        top_s, expert = _topk_rows(cand, kk, payload=cand_expert)
        e = jnp.exp(top_s - top_s[0:1])
        gate = e / jnp.sum(e, axis=0, keepdims=True)
        rows = slice(h * kk, (h + 1) * kk)
        pair_ref[rows, :] = expert >> 1
        parity_ref[rows, :] = expert & 1
        gate_ref[rows, :] = gate


def _peer_route(x2d, g, wq_bf16, keys_bf16, *, tm):
    t, d = x2d.shape
    col = lambda i: (0, i)
    slot_shape = jax.ShapeDtypeStruct((PEER_SLOTS, t), jnp.int32)
    return pl.pallas_call(
        _peer_route_kernel,
        grid=(t // tm,),
        in_specs=[pl.BlockSpec((tm, d), lambda i: (i, 0)),
                  _resident((1, d)),
                  _resident(wq_bf16.shape),
                  _resident(keys_bf16.shape)],
        out_specs=[pl.BlockSpec((tm, d), lambda i: (i, 0)),
                   pl.BlockSpec((PEER_SLOTS, tm), col),
                   pl.BlockSpec((PEER_SLOTS, tm), col),
                   pl.BlockSpec((PEER_SLOTS, tm), col)],
        out_shape=[jax.ShapeDtypeStruct((t, d), F32), slot_shape, slot_shape,
                   jax.ShapeDtypeStruct((PEER_SLOTS, t), F32)],
        compiler_params=_params(1),
        name="peer_route",
    )(x2d, g.reshape(1, d), wq_bf16, keys_bf16)


_HI_MASK = 0xFFFF0000


def _pack_expert_table(u):
    e, d = u.shape
    bits = lax.bitcast_convert_type(u.astype(BF16), jnp.uint16).astype(jnp.uint32)
    bits = bits.reshape(e // 2, 2, d)
    packed = (bits[:, 0] << 16) | bits[:, 1]
    return packed.reshape(e // 2, d // LANES, LANES)


_BIT_REVERSED = (0, 4, 2, 6, 1, 5, 3, 7)


def _bf16_pair_words(x):
    bits = pltpu.bitcast(x, jnp.uint32)
    top = (bits + jnp.uint32(0x7FFF) + ((bits >> 16) & jnp.uint32(1))) >> 16
    return (top << 16) | top


def _packed_add(a, b):
    return pltpu.bitcast(pltpu.bitcast(a, BF16) + pltpu.bitcast(b, BF16), jnp.uint32)


def _sublane_sums(words, sub):
    level = [words[i] for i in _BIT_REVERSED]
    for k in (4, 2, 1):
        low = (sub & k) == 0
        merged = []
        for a, b in zip(level[0::2], level[1::2]):
            if k == 4:
                merged.append(_packed_add(jnp.where(low, a, b),
                                          pltpu.roll(jnp.where(low, b, a), k, 0)))
            else:
                merged.append(jnp.where(low,
                                        _packed_add(a, pltpu.roll(a, SUBLANES - k, 0)),
                                        _packed_add(b, pltpu.roll(b, k, 0))))
        level = merged
    return level[0]


def _peer_up_kernel(pair_ref, x_ref, tab_ref, gate_ref, par_ref, spread_ref, wexp_ref, w2_ref):
    tm = x_ref.shape[0]
    sub = lax.broadcasted_iota(jnp.int32, (SUBLANES, LANES), 0)
    hi_mask = jnp.uint32(_HI_MASK)

    def token_sums(t, m):
        xx = pltpu.bitcast(_bf16_pair_words(x_ref[t]), BF16)
        even, odd = [], []
        for g in range(PEER_SLOTS // SUBLANES):
            words = []
            for i in range(SUBLANES):
                slab = pltpu.bitcast(tab_ref[pair_ref[t, g * SUBLANES + i]], BF16)
                words.append(pltpu.bitcast(slab * xx, jnp.uint32))
            q = _sublane_sums(words, sub)
            even.append(pltpu.bitcast(q & hi_mask, F32))
            odd.append(pltpu.bitcast(q << 16, F32))
        lane_parts = jnp.concatenate(even + odd, axis=0).astype(BF16)
        onehot = (sub == m).astype(BF16)
        return lax.dot_general(onehot, lane_parts, _NT, preferred_element_type=F32)

    def group(gi, _):
        base = pl.multiple_of(gi * SUBLANES, SUBLANES)
        h2 = token_sums(base, 0)
        for m in range(1, SUBLANES):
            h2 = h2 + token_sums(base + m, m)
        odd_slot = par_ref[pl.ds(base, SUBLANES), :] != 0
        h = jnp.where(odd_slot, h2[:, PEER_SLOTS:], h2[:, :PEER_SLOTS])
        act = 0.5 * h * (1.0 + lax.erf(h * (1.0 / math.sqrt(2.0))))
        w = gate_ref[pl.ds(base, SUBLANES), :] * act
        zero = jnp.zeros_like(w)
        w2_ref[pl.ds(base, SUBLANES), 0:PEER_SLOTS] = jnp.where(odd_slot, zero, w)
        w2_ref[pl.ds(base, SUBLANES), PEER_SLOTS:2 * PEER_SLOTS] = jnp.where(odd_slot, w, zero)
        return 0

    lax.fori_loop(0, tm // SUBLANES, group, 0)
    hi, lo = _split_bf16(w2_ref[...])
    wexp_ref[...] = (jnp.dot(hi, spread_ref[...], preferred_element_type=F32)
                     + jnp.dot(lo, spread_ref[...], preferred_element_type=F32))


PAIR_ROWS = 2 * SUBLANES
DOWN_K = PEER_SLOTS * PAIR_ROWS
DOWN_PAIRS_PER_STEP = 4


def _peer_down_kernel(pair_ref, wexp_ref, diag_ref, x_ref, tab_ref, o_ref):
    tm = x_ref.shape[0]

    def left_rows(t):
        row = wexp_ref[pl.ds(t, 1), :]
        hi, lo = _split_bf16(row * diag_ref[...])
        return [hi, lo]

    def token_pair(ta):
        tb = ta + 1
        rhs = jnp.concatenate(
            [jnp.concatenate([tab_ref[pair_ref[ta, j]], tab_ref[pair_ref[tb, j]]], axis=1)
             for j in range(PEER_SLOTS)], axis=0)
        lhs = jnp.concatenate(left_rows(ta) + left_rows(tb), axis=0)
        out = jnp.dot(lhs, rhs, preferred_element_type=F32)
        s = SUBLANES
        o_ref[ta] = x_ref[ta] + (out[0:s, :LANES] + out[s:2 * s, :LANES])
        o_ref[tb] = x_ref[tb] + (out[2 * s:3 * s, LANES:] + out[3 * s:4 * s, LANES:])

    def step(i, _):
        for k in range(DOWN_PAIRS_PER_STEP):
            token_pair(2 * (DOWN_PAIRS_PER_STEP * i + k))
        return 0

    lax.fori_loop(0, tm // (2 * DOWN_PAIRS_PER_STEP), step, 0)


def _smem_rows(tm):
    return pl.BlockSpec((tm, PEER_SLOTS), lambda i: (i, 0), memory_space=pltpu.SMEM)


def _peer_up(pair, parity, xn3, table, gate, spread, *, tm, n_tokens):
    rows = pl.BlockSpec((tm, PEER_SLOTS), lambda i: (i, 0))
    return pl.pallas_call(
        _peer_up_kernel,
        grid=(n_tokens // tm,),
        in_specs=[_smem_rows(tm),
                  pl.BlockSpec((tm, SUBLANES, LANES), lambda i: (i, 0, 0)),
                  _resident(table.shape), rows, rows, _resident(spread.shape)],
        out_specs=pl.BlockSpec((tm, DOWN_K), lambda i: (i, 0)),
        out_shape=jax.ShapeDtypeStruct((n_tokens, DOWN_K), F32),
        scratch_shapes=[pltpu.VMEM((tm, 2 * PEER_SLOTS), F32)],
        compiler_params=_params(1),
        name="peer_up",
    )(pair, xn3, table, gate, parity, spread)


def _peer_down(pair, wexp, diag, x3, table, *, tm, n_tokens):
    tile = pl.BlockSpec((tm, SUBLANES, LANES), lambda i: (i, 0, 0))
    return pl.pallas_call(
        _peer_down_kernel,
        grid=(n_tokens // tm,),
        in_specs=[_smem_rows(tm), pl.BlockSpec((tm, DOWN_K), lambda i: (i, 0)),
                  _resident(diag.shape), tile, _resident(table.shape)],
        out_specs=tile,
        out_shape=jax.ShapeDtypeStruct((n_tokens,) + x3.shape[1:], F32),
        compiler_params=_params(1),
        name="peer_down",
    )(pair, wexp, diag, x3, table)


SC_CORES = 2
SC_SUBCORES = 16
SC_WORKERS = SC_CORES * SC_SUBCORES
SC_LANES = 16
SC_HALF = PEER_SLOTS // 2
SC_ROWS_PER_STEP = 4
SC_CHUNKS = 8


SC_TABLE_ROWS = 64


def _sc_word_table(v):
    e, d = v.shape
    per_worker = e // SC_WORKERS
    words = d // 2
    mesh = plsc.VectorSubcoreMesh(core_axis_name="c", subcore_axis_name="s")

    @functools.partial(
        pl.kernel, mesh=mesh,
        out_type=jax.ShapeDtypeStruct((e, words), jnp.int32),
        scratch_types=[pltpu.VMEM((SC_TABLE_ROWS, d), F32),
                       pltpu.VMEM((SC_TABLE_ROWS, words), jnp.int32)],
        compiler_params=pltpu.CompilerParams(needs_layout_passes=False),
        name="sc_word_table",
    )
    def run(v_hbm, out_hbm, in_v, out_v):
        worker = lax.axis_index("s") * SC_CORES + lax.axis_index("c")
        base = worker * per_worker

        def bf16_bits(x):
            bits = lax.bitcast_convert_type(x, jnp.int32)
            lsb = lax.shift_right_logical(bits, 16) & 1
            rounded = (bits + 0x7FFF + lsb) & jnp.int32(-65536)
            subnormal = (bits & jnp.int32(0x7F800000)) == 0
            return jnp.where(subnormal, bits & jnp.int32(-2147483648), rounded)

        @pl.loop(0, per_worker, step=SC_TABLE_ROWS)
        def _(r0):
            pltpu.sync_copy(v_hbm.at[pl.ds(base + r0, SC_TABLE_ROWS)], in_v)

            @pl.loop(0, SC_TABLE_ROWS)
            def _(r):
                for m in range(words // SC_LANES):
                    hi = bf16_bits(in_v[r, pl.ds(2 * m * SC_LANES, SC_LANES)])
                    lo = bf16_bits(in_v[r, pl.ds((2 * m + 1) * SC_LANES, SC_LANES)])
                    out_v[r, pl.ds(m * SC_LANES, SC_LANES)] = hi | lax.shift_right_logical(lo, 16)

            pltpu.sync_copy(out_v, out_hbm.at[pl.ds(base + r0, SC_TABLE_ROWS)])

    return run(v)


def _sc_peer_up(expert, xn2d, table, *, first, n_tokens):
    t, d = xn2d.shape
    per_worker = n_tokens // SC_WORKERS
    words = d // 2
    mesh = plsc.VectorSubcoreMesh(core_axis_name="c", subcore_axis_name="s")

    @functools.partial(
        pl.kernel, mesh=mesh,
        out_type=jax.ShapeDtypeStruct((n_tokens, PEER_SLOTS), F32),
        scratch_types=[
            pltpu.VMEM((PEER_SLOTS,), jnp.int32),
            pltpu.VMEM((d,), F32),
            pltpu.VMEM((2, SC_HALF, words), jnp.int32),
            pltpu.VMEM((PEER_SLOTS * SC_LANES,), F32),
            pltpu.VMEM((PEER_SLOTS,), F32),
            pltpu.SemaphoreType.DMA,
            pltpu.SemaphoreType.DMA,
        ],
        compiler_params=pltpu.CompilerParams(needs_layout_passes=False),
        name="sc_peer_up",
    )
    def run(expert_hbm, x_hbm, table_hbm, h_hbm, idx_v, x_v, rows_v, part_v, h_v, sem0, sem1):
        worker = lax.axis_index("s") * SC_CORES + lax.axis_index("c")
        local = worker * per_worker
        base = first + local
        lane = lax.iota(jnp.int32, SC_LANES)

        def partial_sums(h):
            @pl.loop(0, SC_HALF, step=SC_ROWS_PER_STEP)
            def _(j0):
                sums = [None] * SC_ROWS_PER_STEP
                for m in range(words // SC_LANES):
                    xa = x_v[pl.ds(2 * m * SC_LANES, SC_LANES)]
                    xb = x_v[pl.ds((2 * m + 1) * SC_LANES, SC_LANES)]
                    for r in range(SC_ROWS_PER_STEP):
                        word = rows_v[h, j0 + r, pl.ds(m * SC_LANES, SC_LANES)]
                        term = (xa * lax.bitcast_convert_type(word & jnp.int32(-65536), F32)
                                + xb * lax.bitcast_convert_type(word << 16, F32))
                        sums[r] = term if sums[r] is None else sums[r] + term
                for r in range(SC_ROWS_PER_STEP):
                    part_v[pl.ds((h * SC_HALF + j0 + r) * SC_LANES, SC_LANES)] = sums[r]

        @pl.loop(0, per_worker)
        def _(i):
            tok = base + i
            pltpu.sync_copy(expert_hbm.at[tok], idx_v)
            first_half = pltpu.async_copy(table_hbm.at[idx_v.at[pl.ds(0, SC_HALF)]],
                                          rows_v.at[0], sem0)
            second_half = pltpu.async_copy(table_hbm.at[idx_v.at[pl.ds(SC_HALF, SC_HALF)]],
                                           rows_v.at[1], sem1)
            pltpu.sync_copy(x_hbm.at[tok], x_v)
            first_half.wait()
            partial_sums(0)
            second_half.wait()
            partial_sums(1)
            for g in range(PEER_SLOTS // SC_LANES):
                total = None
                for k in range(SC_LANES):
                    column = plsc.load_gather(
                        part_v, [g * SC_LANES * SC_LANES + lane * SC_LANES + k])
                    total = column if total is None else total + column
                h_v[pl.ds(g * SC_LANES, SC_LANES)] = total
            pltpu.sync_copy(h_v, h_hbm.at[local + i])

    return run(expert, xn2d, table)


def _peer_act_kernel(h_ref, gate_ref, rep_ref, after_ref, o_ref):
    del after_ref
    h = h_ref[...]
    w = gate_ref[...] * (0.5 * h * (1.0 + lax.erf(h * (1.0 / math.sqrt(2.0)))))
    hi, lo = _split_bf16(w)
    o_ref[...] = (jnp.dot(hi, rep_ref[...], preferred_element_type=F32)
                  + jnp.dot(lo, rep_ref[...], preferred_element_type=F32))


def _peer_act(h, gate, rep, after, *, first, tm):
    n = h.shape[0]
    off = first // tm
    return pl.pallas_call(
        _peer_act_kernel,
        grid=(n // tm,),
        in_specs=[pl.BlockSpec((tm, PEER_SLOTS), lambda i: (i, 0)),
                  pl.BlockSpec((tm, PEER_SLOTS), lambda i: (i + off, 0)),
                  _resident(rep.shape),
                  pl.BlockSpec(memory_space=pl.ANY)],
        out_specs=pl.BlockSpec((tm, PEER_SLOTS * SC_LANES), lambda i: (i, 0)),
        out_shape=jax.ShapeDtypeStruct((n, PEER_SLOTS * SC_LANES), F32),
        compiler_params=_params(1),
        name="peer_act",
    )(h, gate, rep, after)


def _sc_peer_down(expert, w_lanes, x2d, table, *, first, n_tokens):
    t, d = x2d.shape
    per_worker = n_tokens // SC_WORKERS
    words = d // 2
    mesh = plsc.VectorSubcoreMesh(core_axis_name="c", subcore_axis_name="s")

    @functools.partial(
        pl.kernel, mesh=mesh,
        out_type=jax.ShapeDtypeStruct((t, d), F32),
        scratch_types=[
            pltpu.VMEM((PEER_SLOTS,), jnp.int32),
            pltpu.VMEM((PEER_SLOTS * SC_LANES,), F32),
            pltpu.VMEM((d,), F32),
            pltpu.VMEM((2, SC_HALF, words), jnp.int32),
            pltpu.SemaphoreType.DMA,
            pltpu.SemaphoreType.DMA,
        ],
        compiler_params=pltpu.CompilerParams(needs_layout_passes=False),
        name="sc_peer_down",
    )
    def run(expert_hbm, w_hbm, x_hbm, table_hbm, out_hbm, idx_v, w_v, out_v, rows_v, sem0, sem1):
        worker = lax.axis_index("s") * SC_CORES + lax.axis_index("c")
        local = worker * per_worker
        base = first + local

        def accumulate(h):
            @pl.loop(0, SC_HALF, step=SC_ROWS_PER_STEP)
            def _(j0):
                weights = [w_v[pl.ds((h * SC_HALF + j0 + r) * SC_LANES, SC_LANES)]
                           for r in range(SC_ROWS_PER_STEP)]
                for g in range(words // SC_LANES // SC_CHUNKS):
                    loaded = [[rows_v[h, j0 + r, pl.ds((g * SC_CHUNKS + q) * SC_LANES, SC_LANES)]
                               for q in range(SC_CHUNKS)] for r in range(SC_ROWS_PER_STEP)]
                    for q in range(SC_CHUNKS):
                        hi_sum = lo_sum = None
                        for r in range(SC_ROWS_PER_STEP):
                            word = loaded[r][q]
                            hi = weights[r] * lax.bitcast_convert_type(word & jnp.int32(-65536), F32)
                            lo = weights[r] * lax.bitcast_convert_type(word << 16, F32)
                            hi_sum = hi if hi_sum is None else hi_sum + hi
                            lo_sum = lo if lo_sum is None else lo_sum + lo
                        col = (g * SC_CHUNKS + q) * 2 * SC_LANES
                        plsc.addupdate(out_v.at[pl.ds(col, SC_LANES)], hi_sum)
                        plsc.addupdate(out_v.at[pl.ds(col + SC_LANES, SC_LANES)], lo_sum)

        @pl.loop(0, per_worker)
        def _(i):
            tok = base + i
            pltpu.sync_copy(expert_hbm.at[tok], idx_v)
            first_half = pltpu.async_copy(table_hbm.at[idx_v.at[pl.ds(0, SC_HALF)]],
                                          rows_v.at[0], sem0)
            second_half = pltpu.async_copy(table_hbm.at[idx_v.at[pl.ds(SC_HALF, SC_HALF)]],
                                           rows_v.at[1], sem1)
            pltpu.sync_copy(w_hbm.at[local + i], w_v)
            pltpu.sync_copy(x_hbm.at[tok], out_v)
            first_half.wait()
            accumulate(0)
            second_half.wait()
            accumulate(1)
            pltpu.sync_copy(out_v, out_hbm.at[tok])

    return run(expert, w_lanes, x2d, table)


def _peer_constants():
    col = jnp.arange(DOWN_K)
    src = ((col // SUBLANES) % 2) * PEER_SLOTS + col // PAIR_ROWS
    spread = (jnp.arange(2 * PEER_SLOTS)[:, None] == src[None, :]).astype(BF16)
    diag = (col[None, :] % SUBLANES == jnp.arange(SUBLANES)[:, None]).astype(F32)
    lane_repeat = jnp.repeat(jnp.eye(PEER_SLOTS, dtype=BF16), SC_LANES, axis=1)
    return spread, diag, lane_repeat


def _pair_slabs(v):
    e, d = v.shape
    return v.astype(BF16).reshape(e // 2, 2 * d // LANES, LANES)


def _peer(x2d, g, wq_bf16, keys_bf16, u_packed, u_words, v_slabs, v_words, consts, *, tm_route,
          tm_expert, n_sparsecore):
    t, d = x2d.shape
    spread, diag, lane_repeat = consts
    n_tc = t - n_sparsecore
    assert n_sparsecore > 0 and n_sparsecore % SC_WORKERS == 0 and n_tc % tm_expert == 0
    xn, pair, parity, gate = _peer_route(x2d, g, wq_bf16, keys_bf16, tm=tm_route)
    pair, parity, gate = pair.T, parity.T, gate.T
    xn3 = xn[:n_tc].reshape(n_tc, d // LANES, LANES)
    x3 = x2d[:n_tc].reshape(n_tc, d // LANES, LANES)
    expert = pair * 2 + parity
    h_sc = _sc_peer_up(expert, xn, u_words, first=n_tc, n_tokens=n_sparsecore)
    wexp = _peer_up(pair, parity, xn3, u_packed, gate, spread, tm=tm_expert, n_tokens=n_tc)
    w_lanes = _peer_act(h_sc, gate, lane_repeat, wexp, first=n_tc, tm=tm_expert)
    out = _sc_peer_down(expert, w_lanes, x2d, v_words, first=n_tc, n_tokens=n_sparsecore)
    out3 = _peer_down(pair, wexp, diag, x3, v_slabs, tm=tm_expert, n_tokens=n_tc)
    return lax.dynamic_update_slice(out, out3.reshape(n_tc, d), (0, 0))


def _final_norm_kernel(x_ref, g_ref, o_ref):
    o_ref[...] = _rms_scale(x_ref[...], g_ref[...])


def _final_norm(x2d, g, *, tm):
    t, d = x2d.shape
    return pl.pallas_call(
        _final_norm_kernel,
        grid=(t // tm,),
        in_specs=[pl.BlockSpec((tm, d), lambda i: (i, 0)), _resident((1, d))],
        out_specs=pl.BlockSpec((tm, d), lambda i: (i, 0)),
        out_shape=jax.ShapeDtypeStruct((t, d), F32),
        compiler_params=_params(1),
        name="final_norm",
    )(x2d, g.reshape(1, d))


def _tiles(seq):
    return dict(tm_proj=min(256, seq), tm_out=min(512, seq), tq=min(512, seq), tk=min(1024, seq),
                tm_route=LANES, tm_expert=LANES, tm_norm=min(512, seq))


SC_SHARE = 0.34375


def _sparsecore_tokens(t, tm):
    unit = tm * SC_WORKERS // math.gcd(tm, SC_WORKERS)
    return int(t * SC_SHARE) // unit * unit


def _to_residue_major(a2d, batch, seq, dilation, c0, c1):
    cols = a2d[:, c0:c1].reshape(batch, seq // dilation, dilation, c1 - c0)
    return cols.transpose(0, 2, 1, 3)


def _from_residue_major(a4):
    b, d, length, c = a4.shape
    return a4.transpose(0, 2, 1, 3).reshape(b * d * length, c)


def kernel(x, mem, norm_mix, a_w_in, a_lambda, a_subln, b_w_in, norm_mem, w_mem_kv, w_out,
           norm_ffn, peer_wq, peer_keys, peer_u, peer_v, shared_norm, shared_w_kv, final_norm):
    batch, seq, d = x.shape
    mem_tokens = mem.shape[1]
    depth = norm_mix.shape[0]
    n_a = a_w_in.shape[0]
    t = batch * seq
    tiles = _tiles(seq)
    rope = _rope_tables(seq)
    scale = HEAD_DIM ** -0.5

    diff_qk = DIFF_HEADS * 2 * HEAD_DIM
    dil_qk = len(DIL_GROUPS) * DIL_GROUP_WIDTH
    peer_consts = _peer_constants()
    expand = jnp.repeat(jnp.eye(LANES, DIL_HEADS, dtype=BF16), HEAD_DIM, axis=1)

    x2d = x.reshape(t, d)
    mem2d = mem.reshape(batch * mem_tokens, d)
    shared = None

    for l in range(depth):
        memkv = _proj(mem2d, norm_mem[l], w_mem_kv[l].astype(BF16), tm=mem_tokens)
        w_o = w_out[l].astype(BF16)
        if l < n_a:
            col_scale = jnp.concatenate([jnp.full((diff_qk,), scale, F32),
                                         jnp.ones((a_w_in.shape[2] - diff_qk - MEM_Q_WIDTH,), F32),
                                         jnp.full((MEM_Q_WIDTH,), scale, F32)])
            w_in = (a_w_in[l] * col_scale).astype(BF16)
            proj = _proj(x2d, norm_mix[l], w_in, tm=tiles["tm_proj"], n_rope=2 * diff_qk,
                         rope=rope, seq=seq)
            lambda_init = 0.8 - 0.6 * math.exp(-0.3 * l)
            mix = _diff_attention(proj, a_lambda[l], a_subln[l], batch=batch, seq=seq,
                                  lambda_init=lambda_init, tq=tiles["tq"], tk=tiles["tk"])
            x2d = _outproj(x2d, proj, memkv, w_o, seq=seq, mem_tokens=mem_tokens,
                           tm=tiles["tm_out"], mix=mix)
        else:
            w_in = (b_w_in[l - n_a] * scale).astype(BF16)
            proj = _proj(x2d, norm_mix[l], w_in, tm=tiles["tm_proj"], n_rope=dil_qk,
                         rope=rope, seq=seq)
            outs, stats = [], []
            for gi, (window, dilation) in enumerate(DIL_GROUPS):
                c0 = gi * DIL_GROUP_WIDTH
                blk0 = c0 // LANES
                if dilation == 1:
                    q4, q_off = proj.reshape(batch, 1, seq, proj.shape[1]), blk0
                else:
                    q4, q_off = _to_residue_major(proj, batch, seq, dilation,
                                                  c0, c0 + DIL_GROUP_WIDTH), 0
                k4, k_off, v4, v_off = shared[gi]
                o4, st4 = _dilated_group(q4, k4, v4, q_off=q_off, k_off=k_off, v_off=v_off,
                                         n_steps=window // dilation)
                outs.append(_from_residue_major(o4))
                stats.append(_from_residue_major(st4))
            x2d = _outproj(x2d, proj, memkv, w_o, seq=seq, mem_tokens=mem_tokens,
                           tm=tiles["tm_out"], group_out=outs, group_stats=stats, expand=expand)

        x2d = _peer(x2d, norm_ffn[l], peer_wq[l].astype(BF16),
                    peer_keys[l].reshape(2 * PEER_HEADS, PEER_N_KEYS, PEER_HALF).astype(BF16),
                    _pack_expert_table(peer_u[l]), _sc_word_table(peer_u[l]),
                    _pair_slabs(peer_v[l]), _sc_word_table(peer_v[l]), peer_consts,
                    tm_route=tiles["tm_route"], tm_expert=tiles["tm_expert"],
                    n_sparsecore=_sparsecore_tokens(t, tiles["tm_expert"]))

        if l == n_a - 1:
            kv = _proj(x2d, shared_norm, shared_w_kv.astype(BF16), tm=tiles["tm_proj"],
                       n_rope=dil_qk, rope=rope, seq=seq)
            shared = []
            for gi, (window, dilation) in enumerate(DIL_GROUPS):
                c0 = gi * DIL_GROUP_WIDTH
                if dilation == 1:
                    kv4 = kv.reshape(batch, 1, seq, kv.shape[1])
                    shared.append((kv4, c0 // LANES, kv4, (dil_qk + c0) // LANES))
                else:
                    k4 = _to_residue_major(kv, batch, seq, dilation, c0, c0 + DIL_GROUP_WIDTH)
                    v4 = _to_residue_major(kv, batch, seq, dilation, dil_qk + c0,
                                           dil_qk + c0 + DIL_GROUP_WIDTH)
                    shared.append((k4, 0, v4, 0))

    return _final_norm(x2d, final_norm, tm=tiles["tm_norm"]).reshape(batch, seq, d)
```

```python
import functools
import math

import jax
import jax.numpy as jnp
from jax import lax
from jax.experimental import pallas as pl
from jax.experimental.pallas import tpu as pltpu
from jax.experimental.pallas import tpu_sc as plsc

F32 = jnp.float32
BF16 = jnp.bfloat16

HEAD_DIM = 64
ROPE_DIMS = HEAD_DIM // 4
ROPE_HALF = ROPE_DIMS // 2
ROPE_THETA = 500000.0
NORM_EPS = 1e-5
DIFF_HEADS = 6
MEM_HEADS = 4
MEM_Q_WIDTH = MEM_HEADS * HEAD_DIM
DIL_GROUPS = ((128, 1), (512, 4), (2048, 16))
DIL_HEADS = 12
DIL_BLOCK = 128
DIL_GROUP_WIDTH = DIL_HEADS * HEAD_DIM
PEER_HEADS = 8
PEER_N_KEYS = 128
PEER_TOPK = 16
PEER_HALF = 128
PEER_SLOTS = PEER_HEADS * PEER_TOPK

LANES = 128
SUBLANES = 8
VMEM_LIMIT_BYTES = 56 * 1024 * 1024

_NT = (((1,), (1,)), ((), ()))


def _params(n_axes):
    return pltpu.CompilerParams(dimension_semantics=("arbitrary",) * n_axes,
                                vmem_limit_bytes=VMEM_LIMIT_BYTES)


def _resident(shape):
    zeros = (0,) * len(shape)
    return pl.BlockSpec(shape, lambda *_: zeros, pipeline_mode=pl.Buffered(1))


def _rms_scale(x, g):
    ms = jnp.mean(x * x, axis=-1, keepdims=True)
    return x * lax.rsqrt(ms + NORM_EPS) * g


def _rope_tables(seq):
    inv = ROPE_THETA ** (-jnp.arange(0, ROPE_DIMS, 2, dtype=F32) / ROPE_DIMS)
    ang = jnp.arange(seq, dtype=F32)[:, None] * inv[None, :]
    cos, sin = jnp.cos(ang), jnp.sin(ang)
    ones = jnp.ones((seq, HEAD_DIM - ROPE_DIMS), F32)
    zeros = jnp.zeros((seq, HEAD_DIM - ROPE_DIMS), F32)
    zh = jnp.zeros((seq, ROPE_HALF), F32)
    c = jnp.concatenate([cos, cos, ones], axis=1)
    sa = jnp.concatenate([-sin, zh, zeros], axis=1)
    sb = jnp.concatenate([zh, sin, zeros], axis=1)
    rep = LANES // HEAD_DIM
    return tuple(jnp.tile(t, (1, rep)) for t in (c, sa, sb))


def _proj_kernel(*refs, n_rope, chunk):
    if n_rope:
        x_ref, g_ref, w_ref, c_ref, sa_ref, sb_ref, o_ref = refs
    else:
        x_ref, g_ref, w_ref, o_ref = refs
    y = _rms_scale(x_ref[...], g_ref[...]).astype(BF16)
    n = o_ref.shape[1]
    for c0 in range(0, n, chunk):
        acc = jnp.dot(y, w_ref[:, c0:c0 + chunk], preferred_element_type=F32)
        if c0 < n_rope:
            for k0 in range(0, chunk, LANES):
                a = acc[:, k0:k0 + LANES]
                a = (a * c_ref[...]
                     + pltpu.roll(a, LANES - ROPE_HALF, 1) * sa_ref[...]
                     + pltpu.roll(a, ROPE_HALF, 1) * sb_ref[...])
                o_ref[:, c0 + k0:c0 + k0 + LANES] = a.astype(o_ref.dtype)
        else:
            o_ref[:, c0:c0 + chunk] = acc.astype(o_ref.dtype)


def _proj(x2d, g, w_bf16, *, tm, n_rope=0, rope=None, seq=None):
    t, d = x2d.shape
    n = w_bf16.shape[1]
    chunk = 256
    assert t % tm == 0 and n % chunk == 0 and n_rope % chunk == 0
    in_specs = [pl.BlockSpec((tm, d), lambda i: (i, 0)),
                _resident((1, d)),
                _resident((d, n))]
    args = [x2d, g.reshape(1, d), w_bf16]
    if n_rope:
        nblk = seq // tm
        spec = pl.BlockSpec((tm, LANES), lambda i: (i % nblk, 0))
        in_specs += [spec, spec, spec]
        args += list(rope)
    return pl.pallas_call(
        functools.partial(_proj_kernel, n_rope=n_rope, chunk=chunk),
        grid=(t // tm,),
        in_specs=in_specs,
        out_specs=pl.BlockSpec((tm, n), lambda i: (i, 0)),
        out_shape=jax.ShapeDtypeStruct((t, n), BF16),
        compiler_params=_params(1),
        name="norm_proj",
    )(*args)


def _online_softmax_step(q, k, v, carry, mask):
    m, l, acc = carry
    s = lax.dot_general(q, k, _NT, preferred_element_type=F32)
    if mask is not None:
        s = jnp.where(mask, s, -jnp.inf)
    m_new = jnp.maximum(m, jnp.max(s, axis=-1, keepdims=True))
    alpha = jnp.exp(m - m_new)
    p = jnp.exp(s - m_new)
    l = alpha * l + jnp.sum(p, axis=-1, keepdims=True)
    acc = alpha * acc + jnp.dot(p.astype(BF16), v, preferred_element_type=F32)
    return m_new, l, acc


def _diff_attn_kernel(q_ref, k_ref, v_ref, lp_ref, g_ref, o_ref, *, tq, tk, lambda_init):
    qi = pl.program_id(2)
    q = q_ref[...]
    lane = lax.broadcasted_iota(jnp.int32, q.shape, 1)
    zero = jnp.zeros_like(q)
    qa = jnp.where(lane < HEAD_DIM, q, zero)
    qb = jnp.where(lane >= HEAD_DIM, q, zero)

    def init():
        return (jnp.full((tq, 1), -jnp.inf, F32), jnp.zeros((tq, 1), F32),
                jnp.zeros((tq, 2 * HEAD_DIM), F32))

    def body(j, carry):
        c1, c2 = carry
        k = k_ref[pl.ds(pl.multiple_of(j * tk, tk), tk), :]
        v = v_ref[pl.ds(pl.multiple_of(j * tk, tk), tk), :]
        return (_online_softmax_step(qa, k, v, c1, None),
                _online_softmax_step(qb, k, v, c2, None))

    n_full = (qi * tq) // tk
    c1, c2 = lax.fori_loop(0, n_full, body, (init(), init()))
    row = qi * tq + lax.broadcasted_iota(jnp.int32, (tq, tk), 0)
    col = n_full * tk + lax.broadcasted_iota(jnp.int32, (tq, tk), 1)
    causal = col <= row
    start = pl.multiple_of(n_full * tk, tk)
    k = k_ref[pl.ds(start, tk), :]
    v = v_ref[pl.ds(start, tk), :]
    m1, l1, a1 = _online_softmax_step(qa, k, v, c1, causal)
    m2, l2, a2 = _online_softmax_step(qb, k, v, c2, causal)

    lp = lp_ref[...]
    lam = (jnp.exp(jnp.sum(lp[0:1] * lp[1:2], axis=-1, keepdims=True))
           - jnp.exp(jnp.sum(lp[2:3] * lp[3:4], axis=-1, keepdims=True)) + lambda_init)
    o = a1 / l1 - lam * (a2 / l2)
    o = _rms_scale(o, g_ref[...]) * (1.0 - lambda_init)
    o_ref[...] = o.astype(o_ref.dtype)


def _diff_attention(proj, lp, subln, *, batch, seq, lambda_init, tq, tk):
    assert tk % tq == 0 and seq % tk == 0
    nq = seq // tq
    width = 2 * HEAD_DIM
    return pl.pallas_call(
        functools.partial(_diff_attn_kernel, tq=tq, tk=tk, lambda_init=lambda_init),
        grid=(batch, DIFF_HEADS, nq),
        in_specs=[
            pl.BlockSpec((tq, width), lambda b, h, i: (b * nq + i, h)),
            pl.BlockSpec((seq, width), lambda b, h, i: (b, DIFF_HEADS + h)),
            pl.BlockSpec((seq, width), lambda b, h, i: (b, 2 * DIFF_HEADS + h)),
            _resident((4, HEAD_DIM)),
            _resident((1, width)),
        ],
        out_specs=pl.BlockSpec((tq, width), lambda b, h, i: (b * nq + i, h)),
        out_shape=jax.ShapeDtypeStruct((batch * seq, DIFF_HEADS * width), BF16),
        compiler_params=_params(3),
        name="diff_attention",
    )(proj, proj, proj, lp, subln.reshape(1, width))


def _dilated_kernel(q_ref, kc_ref, kp_ref, vc_ref, vp_ref, o_ref, st_ref, *, tq, n_steps):
    n = pl.program_id(2)
    hp = pl.program_id(3)
    blk = DIL_BLOCK

    @pl.when(hp == 0)
    def _():
        st_ref[...] = jnp.zeros_like(st_ref)

    qrow = lax.broadcasted_iota(jnp.int32, (blk, 2 * blk), 0)
    kcol = lax.broadcasted_iota(jnp.int32, (blk, 2 * blk), 1)
    dist = qrow + blk - kcol
    band = (dist >= 0) & (dist <= n_steps)
    lane_q = lax.broadcasted_iota(jnp.int32, (blk, LANES), 1)
    lane_s = lax.broadcasted_iota(jnp.int32, (blk, LANES), 1)

    for r, i in [(r, i) for r in range(q_ref.shape[0]) for i in range(tq // blk)]:
        rows = slice(i * blk, (i + 1) * blk)
        q = q_ref[r, rows, :]
        if i == 0:
            k_prev, v_prev = kp_ref[r], vp_ref[r]
            valid = band & ((n > 0) | (kcol >= blk))
        else:
            prev = slice((i - 1) * blk, i * blk)
            k_prev, v_prev = kc_ref[r, prev, :], vc_ref[r, prev, :]
            valid = band
        keys = jnp.concatenate([k_prev, kc_ref[r, rows, :]], axis=0)
        vals = jnp.concatenate([v_prev, vc_ref[r, rows, :]], axis=0)
        out = jnp.zeros((blk, LANES), F32)
        stats = st_ref[r, rows, :]
        for e in range(LANES // HEAD_DIM):
            head_lanes = (lane_q // HEAD_DIM) == e
            qe = jnp.where(head_lanes, q, jnp.zeros_like(q))
            s = lax.dot_general(qe, keys, _NT, preferred_element_type=F32)
            s = jnp.where(valid, s, -jnp.inf)
            m = jnp.max(s, axis=-1, keepdims=True)
            p = jnp.exp(s - m)
            den = jnp.sum(p, axis=-1, keepdims=True)
            oe = jnp.dot((p / den).astype(BF16), vals, preferred_element_type=F32)
            out = jnp.where(head_lanes, oe, out)
            lse = m + jnp.log(den)
            stats = jnp.where(lane_s == hp * (LANES // HEAD_DIM) + e, lse, stats)
        o_ref[r, rows, :] = out.astype(o_ref.dtype)
        st_ref[r, rows, :] = stats


def _dilated_group(q4, k4, v4, *, q_off, k_off, v_off, n_steps):
    b, d, length, _ = q4.shape
    rows_per_step = 1024
    tq = min(rows_per_step, length)
    nres = min(d, rows_per_step // tq)
    assert length % tq == 0 and tq % DIL_BLOCK == 0 and d % nres == 0
    sub = tq // DIL_BLOCK
    pairs = DIL_GROUP_WIDTH // LANES

    def cur(off):
        return pl.BlockSpec((None, nres, tq, LANES), lambda bi, r, n, h: (bi, r, n, off + h))

    def prev(off):
        return pl.BlockSpec((None, nres, DIL_BLOCK, LANES),
                            lambda bi, r, n, h: (bi, r, jnp.maximum(n * sub - 1, 0), off + h))

    return pl.pallas_call(
        functools.partial(_dilated_kernel, tq=tq, n_steps=n_steps),
        grid=(b, d // nres, length // tq, pairs),
        in_specs=[cur(q_off), cur(k_off), prev(k_off), cur(v_off), prev(v_off)],
        out_specs=[pl.BlockSpec((None, nres, tq, LANES), lambda bi, r, n, h: (bi, r, n, h)),
                   pl.BlockSpec((None, nres, tq, LANES), lambda bi, r, n, h: (bi, r, n, 0))],
        out_shape=[jax.ShapeDtypeStruct((b, d, length, DIL_GROUP_WIDTH), BF16),
                   jax.ShapeDtypeStruct((b, d, length, LANES), F32)],
        compiler_params=_params(4),
        name="dilated_attention",
    )(q4, k4, k4, v4, v4)


def _memory_attention(q, k, v):
    lane = lax.broadcasted_iota(jnp.int32, q.shape, 1)
    out = jnp.zeros(q.shape, F32)
    for h in range(MEM_HEADS):
        head_lanes = (lane // HEAD_DIM) == h
        qh = jnp.where(head_lanes, q, jnp.zeros_like(q))
        s = lax.dot_general(qh, k, _NT, preferred_element_type=F32)
        m = jnp.max(s, axis=-1, keepdims=True)
        p = jnp.exp(s - m)
        den = jnp.sum(p, axis=-1, keepdims=True)
        oh = jnp.dot((p / den).astype(BF16), v, preferred_element_type=F32)
        out = jnp.where(head_lanes, oh, out)
    return out


def _split_bf16(x):
    hi = x.astype(BF16)
    lo = (x - hi.astype(F32)).astype(BF16)
    return hi, lo


def _outproj_kernel(*refs, n_groups):
    if n_groups:
        x_ref, qm_ref, mk_ref, mv_ref, w_ref = refs[:5]
        o_refs = refs[5:5 + n_groups]
        st_refs = refs[5 + n_groups:5 + 2 * n_groups]
        ex_ref, out_ref = refs[5 + 2 * n_groups:]
        lses = [r[...] for r in st_refs]
        top = functools.reduce(jnp.maximum, lses)
        es = [jnp.exp(l - top) for l in lses]
        den = functools.reduce(lambda a, b: a + b, es)
        mix = None
        for e, o_ref in zip(es, o_refs):
            hi, lo = _split_bf16(e / den)
            wide = (jnp.dot(hi, ex_ref[...], preferred_element_type=F32)
                    + jnp.dot(lo, ex_ref[...], preferred_element_type=F32))
            term = wide * o_ref[...].astype(F32)
            mix = term if mix is None else mix + term
        mix = mix.astype(BF16)
    else:
        x_ref, qm_ref, mk_ref, mv_ref, w_ref, mix_ref, out_ref = refs
        mix = mix_ref[...]
    mo = _memory_attention(qm_ref[...], mk_ref[...], mv_ref[...]).astype(BF16)
    k_mix = mix.shape[1]
    acc = jnp.dot(mix, w_ref[:k_mix, :], preferred_element_type=F32)
    acc += jnp.dot(mo, w_ref[k_mix:, :], preferred_element_type=F32)
    out_ref[...] = x_ref[...] + acc


def _outproj(x2d, proj, memkv, w_bf16, *, seq, mem_tokens, tm, mix=None, group_out=None,
             group_stats=None, expand=None):
    t, d = x2d.shape
    per_batch = seq // tm
    qm_block = (proj.shape[1] - MEM_Q_WIDTH) // MEM_Q_WIDTH
    row = lambda i: (i, 0)
    in_specs = [pl.BlockSpec((tm, d), row),
                pl.BlockSpec((tm, MEM_Q_WIDTH), lambda i: (i, qm_block)),
                pl.BlockSpec((mem_tokens, MEM_Q_WIDTH), lambda i: (i // per_batch, 0)),
                pl.BlockSpec((mem_tokens, MEM_Q_WIDTH), lambda i: (i // per_batch, 1)),
                _resident(w_bf16.shape)]
    args = [x2d, proj, memkv, memkv, w_bf16]
    if mix is not None:
        n_groups = 0
        in_specs.append(pl.BlockSpec((tm, mix.shape[1]), row))
        args.append(mix)
    else:
        n_groups = len(group_out)
        in_specs += [pl.BlockSpec((tm, DIL_GROUP_WIDTH), row)] * n_groups
        in_specs += [pl.BlockSpec((tm, LANES), row)] * n_groups
        in_specs.append(_resident(expand.shape))
        args += list(group_out) + list(group_stats) + [expand]
    return pl.pallas_call(
        functools.partial(_outproj_kernel, n_groups=n_groups),
        grid=(t // tm,),
        in_specs=in_specs,
        out_specs=pl.BlockSpec((tm, d), row),
        out_shape=jax.ShapeDtypeStruct((t, d), F32),
        compiler_params=_params(1),
        name="mix_outproj",
    )(*args)


def _topk_rows(s, k, payload=None):
    n = s.shape[0]
    row = lax.broadcasted_iota(jnp.int32, s.shape, 0)
    vals, picked = [], []
    for _ in range(k):
        m = jnp.max(s, axis=0, keepdims=True)
        i = jnp.min(jnp.where(s == m, row, n), axis=0, keepdims=True)
        hit = row == i
        vals.append(m)
        if payload is None:
            picked.append(i)
        else:
            picked.append(jnp.sum(jnp.where(hit, payload, 0), axis=0, keepdims=True))
        s = jnp.where(hit, -jnp.inf, s)
    return jnp.concatenate(vals, axis=0), jnp.concatenate(picked, axis=0)


def _peer_route_kernel(x_ref, g_ref, wq_ref, keys_ref, xn_ref, pair_ref, parity_ref, gate_ref):
    xn = _rms_scale(x_ref[...], g_ref[...])
    xn_ref[...] = xn
    q = jnp.dot(xn.astype(BF16), wq_ref[...], preferred_element_type=F32).astype(BF16)
    kk = PEER_TOPK
    for h in range(PEER_HEADS):
        tops = []
        for half in range(2):
            c0 = (2 * h + half) * PEER_HALF
            s = lax.dot_general(keys_ref[2 * h + half], q[:, c0:c0 + PEER_HALF], _NT,
                                preferred_element_type=F32)
            tops.append(_topk_rows(s, kk))
        (s1, i1), (s2, i2) = tops
        half = kk // 2
        sub = lax.broadcasted_iota(jnp.int32, (half, s1.shape[1]), 0)
        blocks = [s1[0:1] + s2]
        experts = [i1[0:1] * PEER_N_KEYS + i2]
        for a in range(1, half):
            blocks.append(jnp.where(sub < kk // (a + 1), s1[a:a + 1] + s2[:half], -jnp.inf))
            experts.append(i1[a:a + 1] * PEER_N_KEYS + i2[:half])
        blocks.append(s1[half:] + s2[0:1])
        experts.append(i1[half:] * PEER_N_KEYS + i2[0:1])
        cand = jnp.concatenate(blocks, axis=0)
        cand_expert = jnp.concatenate(experts, axis=0)
        top_s, expert = _topk_rows(cand, kk, payload=cand_expert)
        e = jnp.exp(top_s - top_s[0:1])
        gate = e / jnp.sum(e, axis=0, keepdims=True)
        rows = slice(h * kk, (h + 1) * kk)
        pair_ref[rows, :] = expert >> 1
        parity_ref[rows, :] = expert & 1
        gate_ref[rows, :] = gate


def _peer_route(x2d, g, wq_bf16, keys_bf16, *, tm):
    t, d = x2d.shape
    col = lambda i: (0, i)
    slot_shape = jax.ShapeDtypeStruct((PEER_SLOTS, t), jnp.int32)
    return pl.pallas_call(
        _peer_route_kernel,
        grid=(t // tm,),
        in_specs=[pl.BlockSpec((tm, d), lambda i: (i, 0)),
                  _resident((1, d)),
                  _resident(wq_bf16.shape),
                  _resident(keys_bf16.shape)],
        out_specs=[pl.BlockSpec((tm, d), lambda i: (i, 0)),
                   pl.BlockSpec((PEER_SLOTS, tm), col),
                   pl.BlockSpec((PEER_SLOTS, tm), col),
                   pl.BlockSpec((PEER_SLOTS, tm), col)],
        out_shape=[jax.ShapeDtypeStruct((t, d), F32), slot_shape, slot_shape,
                   jax.ShapeDtypeStruct((PEER_SLOTS, t), F32)],
        compiler_params=_params(1),
        name="peer_route",
    )(x2d, g.reshape(1, d), wq_bf16, keys_bf16)


_HI_MASK = 0xFFFF0000


def _pack_expert_table(u):
    e, d = u.shape
    bits = lax.bitcast_convert_type(u.astype(BF16), jnp.uint16).astype(jnp.uint32)
    bits = bits.reshape(e // 2, 2, d)
    packed = (bits[:, 0] << 16) | bits[:, 1]
    return packed.reshape(e // 2, d // LANES, LANES)


_BIT_REVERSED = (0, 4, 2, 6, 1, 5, 3, 7)


def _bf16_pair_words(x):
    bits = pltpu.bitcast(x, jnp.uint32)
    top = (bits + jnp.uint32(0x7FFF) + ((bits >> 16) & jnp.uint32(1))) >> 16
    return (top << 16) | top


def _packed_add(a, b):
    return pltpu.bitcast(pltpu.bitcast(a, BF16) + pltpu.bitcast(b, BF16), jnp.uint32)


def _sublane_sums(words, sub):
    level = [words[i] for i in _BIT_REVERSED]
    for k in (4, 2, 1):
        low = (sub & k) == 0
        merged = []
        for a, b in zip(level[0::2], level[1::2]):
            if k == 4:
                merged.append(_packed_add(jnp.where(low, a, b),
                                          pltpu.roll(jnp.where(low, b, a), k, 0)))
            else:
                merged.append(jnp.where(low,
                                        _packed_add(a, pltpu.roll(a, SUBLANES - k, 0)),
                                        _packed_add(b, pltpu.roll(b, k, 0))))
        level = merged
    return level[0]


def _peer_up_kernel(pair_ref, x_ref, tab_ref, gate_ref, par_ref, spread_ref, wexp_ref, w2_ref):
    tm = x_ref.shape[0]
    sub = lax.broadcasted_iota(jnp.int32, (SUBLANES, LANES), 0)
    hi_mask = jnp.uint32(_HI_MASK)

    def token_sums(t, m):
        xx = pltpu.bitcast(_bf16_pair_words(x_ref[t]), BF16)
        even, odd = [], []
        for g in range(PEER_SLOTS // SUBLANES):
            words = []
            for i in range(SUBLANES):
                slab = pltpu.bitcast(tab_ref[pair_ref[t, g * SUBLANES + i]], BF16)
                words.append(pltpu.bitcast(slab * xx, jnp.uint32))
            q = _sublane_sums(words, sub)
            even.append(pltpu.bitcast(q & hi_mask, F32))
            odd.append(pltpu.bitcast(q << 16, F32))
        lane_parts = jnp.concatenate(even + odd, axis=0).astype(BF16)
        onehot = (sub == m).astype(BF16)
        return lax.dot_general(onehot, lane_parts, _NT, preferred_element_type=F32)

    def group(gi, _):
        base = pl.multiple_of(gi * SUBLANES, SUBLANES)
        h2 = token_sums(base, 0)
        for m in range(1, SUBLANES):
            h2 = h2 + token_sums(base + m, m)
        odd_slot = par_ref[pl.ds(base, SUBLANES), :] != 0
        h = jnp.where(odd_slot, h2[:, PEER_SLOTS:], h2[:, :PEER_SLOTS])
        act = 0.5 * h * (1.0 + lax.erf(h * (1.0 / math.sqrt(2.0))))
        w = gate_ref[pl.ds(base, SUBLANES), :] * act
        zero = jnp.zeros_like(w)
        w2_ref[pl.ds(base, SUBLANES), 0:PEER_SLOTS] = jnp.where(odd_slot, zero, w)
        w2_ref[pl.ds(base, SUBLANES), PEER_SLOTS:2 * PEER_SLOTS] = jnp.where(odd_slot, w, zero)
        return 0

    lax.fori_loop(0, tm // SUBLANES, group, 0)
    hi, lo = _split_bf16(w2_ref[...])
    wexp_ref[...] = (jnp.dot(hi, spread_ref[...], preferred_element_type=F32)
                     + jnp.dot(lo, spread_ref[...], preferred_element_type=F32))


PAIR_ROWS = 2 * SUBLANES
DOWN_K = PEER_SLOTS * PAIR_ROWS
DOWN_PAIRS_PER_STEP = 4


def _peer_down_kernel(pair_ref, wexp_ref, diag_ref, x_ref, tab_ref, o_ref):
    tm = x_ref.shape[0]

    def left_rows(t):
        row = wexp_ref[pl.ds(t, 1), :]
        hi, lo = _split_bf16(row * diag_ref[...])
        return [hi, lo]

    def token_pair(ta):
        tb = ta + 1
        rhs = jnp.concatenate(
            [jnp.concatenate([tab_ref[pair_ref[ta, j]], tab_ref[pair_ref[tb, j]]], axis=1)
             for j in range(PEER_SLOTS)], axis=0)
        lhs = jnp.concatenate(left_rows(ta) + left_rows(tb), axis=0)
        out = jnp.dot(lhs, rhs, preferred_element_type=F32)
        s = SUBLANES
        o_ref[ta] = x_ref[ta] + (out[0:s, :LANES] + out[s:2 * s, :LANES])
        o_ref[tb] = x_ref[tb] + (out[2 * s:3 * s, LANES:] + out[3 * s:4 * s, LANES:])

    def step(i, _):
        for k in range(DOWN_PAIRS_PER_STEP):
            token_pair(2 * (DOWN_PAIRS_PER_STEP * i + k))
        return 0

    lax.fori_loop(0, tm // (2 * DOWN_PAIRS_PER_STEP), step, 0)


def _smem_rows(tm):
    return pl.BlockSpec((tm, PEER_SLOTS), lambda i: (i, 0), memory_space=pltpu.SMEM)


def _peer_up(pair, parity, xn3, table, gate, spread, *, tm, n_tokens):
    rows = pl.BlockSpec((tm, PEER_SLOTS), lambda i: (i, 0))
    return pl.pallas_call(
        _peer_up_kernel,
        grid=(n_tokens // tm,),
        in_specs=[_smem_rows(tm),
                  pl.BlockSpec((tm, SUBLANES, LANES), lambda i: (i, 0, 0)),
                  _resident(table.shape), rows, rows, _resident(spread.shape)],
        out_specs=pl.BlockSpec((tm, DOWN_K), lambda i: (i, 0)),
        out_shape=jax.ShapeDtypeStruct((n_tokens, DOWN_K), F32),
        scratch_shapes=[pltpu.VMEM((tm, 2 * PEER_SLOTS), F32)],
        compiler_params=_params(1),
        name="peer_up",
    )(pair, xn3, table, gate, parity, spread)


def _peer_down(pair, wexp, diag, x3, table, *, tm, n_tokens):
    tile = pl.BlockSpec((tm, SUBLANES, LANES), lambda i: (i, 0, 0))
    return pl.pallas_call(
        _peer_down_kernel,
        grid=(n_tokens // tm,),
        in_specs=[_smem_rows(tm), pl.BlockSpec((tm, DOWN_K), lambda i: (i, 0)),
                  _resident(diag.shape), tile, _resident(table.shape)],
        out_specs=tile,
        out_shape=jax.ShapeDtypeStruct((n_tokens,) + x3.shape[1:], F32),
        compiler_params=_params(1),
        name="peer_down",
    )(pair, wexp, diag, x3, table)


SC_CORES = 2
SC_SUBCORES = 16
SC_WORKERS = SC_CORES * SC_SUBCORES
SC_LANES = 16
SC_HALF = PEER_SLOTS // 2
SC_ROWS_PER_STEP = 4
SC_CHUNKS = 8


SC_TABLE_ROWS = 64


def _sc_word_table(v):
    e, d = v.shape
    per_worker = e // SC_WORKERS
    words = d // 2
    mesh = plsc.VectorSubcoreMesh(core_axis_name="c", subcore_axis_name="s")

    @functools.partial(
        pl.kernel, mesh=mesh,
        out_type=jax.ShapeDtypeStruct((e, words), jnp.int32),
        scratch_types=[pltpu.VMEM((SC_TABLE_ROWS, d), F32),
                       pltpu.VMEM((SC_TABLE_ROWS, words), jnp.int32)],
        compiler_params=pltpu.CompilerParams(needs_layout_passes=False),
        name="sc_word_table",
    )
    def run(v_hbm, out_hbm, in_v, out_v):
        worker = lax.axis_index("s") * SC_CORES + lax.axis_index("c")
        base = worker * per_worker

        def bf16_bits(x):
            bits = lax.bitcast_convert_type(x, jnp.int32)
            lsb = lax.shift_right_logical(bits, 16) & 1
            rounded = (bits + 0x7FFF + lsb) & jnp.int32(-65536)
            subnormal = (bits & jnp.int32(0x7F800000)) == 0
            return jnp.where(subnormal, bits & jnp.int32(-2147483648), rounded)

        @pl.loop(0, per_worker, step=SC_TABLE_ROWS)
        def _(r0):
            pltpu.sync_copy(v_hbm.at[pl.ds(base + r0, SC_TABLE_ROWS)], in_v)

            @pl.loop(0, SC_TABLE_ROWS)
            def _(r):
                for m in range(words // SC_LANES):
                    hi = bf16_bits(in_v[r, pl.ds(2 * m * SC_LANES, SC_LANES)])
                    lo = bf16_bits(in_v[r, pl.ds((2 * m + 1) * SC_LANES, SC_LANES)])
                    out_v[r, pl.ds(m * SC_LANES, SC_LANES)] = hi | lax.shift_right_logical(lo, 16)

            pltpu.sync_copy(out_v, out_hbm.at[pl.ds(base + r0, SC_TABLE_ROWS)])

    return run(v)


def _sc_peer_up(expert, xn2d, table, *, first, n_tokens):
    t, d = xn2d.shape
    per_worker = n_tokens // SC_WORKERS
    words = d // 2
    mesh = plsc.VectorSubcoreMesh(core_axis_name="c", subcore_axis_name="s")

    @functools.partial(
        pl.kernel, mesh=mesh,
        out_type=jax.ShapeDtypeStruct((n_tokens, PEER_SLOTS), F32),
        scratch_types=[
            pltpu.VMEM((PEER_SLOTS,), jnp.int32),
            pltpu.VMEM((d,), F32),
            pltpu.VMEM((2, SC_HALF, words), jnp.int32),
            pltpu.VMEM((PEER_SLOTS * SC_LANES,), F32),
            pltpu.VMEM((PEER_SLOTS,), F32),
            pltpu.SemaphoreType.DMA,
            pltpu.SemaphoreType.DMA,
        ],
        compiler_params=pltpu.CompilerParams(needs_layout_passes=False),
        name="sc_peer_up",
    )
    def run(expert_hbm, x_hbm, table_hbm, h_hbm, idx_v, x_v, rows_v, part_v, h_v, sem0, sem1):
        worker = lax.axis_index("s") * SC_CORES + lax.axis_index("c")
        local = worker * per_worker
        base = first + local
        lane = lax.iota(jnp.int32, SC_LANES)

        def partial_sums(h):
            @pl.loop(0, SC_HALF, step=SC_ROWS_PER_STEP)
            def _(j0):
                sums = [None] * SC_ROWS_PER_STEP
                for m in range(words // SC_LANES):
                    xa = x_v[pl.ds(2 * m * SC_LANES, SC_LANES)]
                    xb = x_v[pl.ds((2 * m + 1) * SC_LANES, SC_LANES)]
                    for r in range(SC_ROWS_PER_STEP):
                        word = rows_v[h, j0 + r, pl.ds(m * SC_LANES, SC_LANES)]
                        term = (xa * lax.bitcast_convert_type(word & jnp.int32(-65536), F32)
                                + xb * lax.bitcast_convert_type(word << 16, F32))
                        sums[r] = term if sums[r] is None else sums[r] + term
                for r in range(SC_ROWS_PER_STEP):
                    part_v[pl.ds((h * SC_HALF + j0 + r) * SC_LANES, SC_LANES)] = sums[r]

        @pl.loop(0, per_worker)
        def _(i):
            tok = base + i
            pltpu.sync_copy(expert_hbm.at[tok], idx_v)
            first_half = pltpu.async_copy(table_hbm.at[idx_v.at[pl.ds(0, SC_HALF)]],
                                          rows_v.at[0], sem0)
            second_half = pltpu.async_copy(table_hbm.at[idx_v.at[pl.ds(SC_HALF, SC_HALF)]],
                                           rows_v.at[1], sem1)
            pltpu.sync_copy(x_hbm.at[tok], x_v)
            first_half.wait()
            partial_sums(0)
            second_half.wait()
            partial_sums(1)
            for g in range(PEER_SLOTS // SC_LANES):
                total = None
                for k in range(SC_LANES):
                    column = plsc.load_gather(
                        part_v, [g * SC_LANES * SC_LANES + lane * SC_LANES + k])
                    total = column if total is None else total + column
                h_v[pl.ds(g * SC_LANES, SC_LANES)] = total
            pltpu.sync_copy(h_v, h_hbm.at[local + i])

    return run(expert, xn2d, table)


def _peer_act_kernel(h_ref, gate_ref, rep_ref, after_ref, o_ref):
    del after_ref
    h = h_ref[...]
    w = gate_ref[...] * (0.5 * h * (1.0 + lax.erf(h * (1.0 / math.sqrt(2.0)))))
    hi, lo = _split_bf16(w)
    o_ref[...] = (jnp.dot(hi, rep_ref[...], preferred_element_type=F32)
                  + jnp.dot(lo, rep_ref[...], preferred_element_type=F32))


def _peer_act(h, gate, rep, after, *, first, tm):
    n = h.shape[0]
    off = first // tm
    return pl.pallas_call(
        _peer_act_kernel,
        grid=(n // tm,),
        in_specs=[pl.BlockSpec((tm, PEER_SLOTS), lambda i: (i, 0)),
                  pl.BlockSpec((tm, PEER_SLOTS), lambda i: (i + off, 0)),
                  _resident(rep.shape),
                  pl.BlockSpec(memory_space=pl.ANY)],
        out_specs=pl.BlockSpec((tm, PEER_SLOTS * SC_LANES), lambda i: (i, 0)),
        out_shape=jax.ShapeDtypeStruct((n, PEER_SLOTS * SC_LANES), F32),
        compiler_params=_params(1),
        name="peer_act",
    )(h, gate, rep, after)


def _sc_peer_down(expert, w_lanes, x2d, table, *, first, n_tokens):
    t, d = x2d.shape
    per_worker = n_tokens // SC_WORKERS
    words = d // 2
    mesh = plsc.VectorSubcoreMesh(core_axis_name="c", subcore_axis_name="s")

    @functools.partial(
        pl.kernel, mesh=mesh,
        out_type=jax.ShapeDtypeStruct((t, d), F32),
        scratch_types=[
            pltpu.VMEM((PEER_SLOTS,), jnp.int32),
            pltpu.VMEM((PEER_SLOTS * SC_LANES,), F32),
            pltpu.VMEM((d,), F32),
            pltpu.VMEM((2, SC_HALF, words), jnp.int32),
            pltpu.SemaphoreType.DMA,
            pltpu.SemaphoreType.DMA,
        ],
        compiler_params=pltpu.CompilerParams(needs_layout_passes=False),
        name="sc_peer_down",
    )
    def run(expert_hbm, w_hbm, x_hbm, table_hbm, out_hbm, idx_v, w_v, out_v, rows_v, sem0, sem1):
        worker = lax.axis_index("s") * SC_CORES + lax.axis_index("c")
        local = worker * per_worker
        base = first + local

        def accumulate(h):
            @pl.loop(0, SC_HALF, step=SC_ROWS_PER_STEP)
            def _(j0):
                weights = [w_v[pl.ds((h * SC_HALF + j0 + r) * SC_LANES, SC_LANES)]
                           for r in range(SC_ROWS_PER_STEP)]
                for g in range(words // SC_LANES // SC_CHUNKS):
                    loaded = [[rows_v[h, j0 + r, pl.ds((g * SC_CHUNKS + q) * SC_LANES, SC_LANES)]
                               for q in range(SC_CHUNKS)] for r in range(SC_ROWS_PER_STEP)]
                    for q in range(SC_CHUNKS):
                        hi_sum = lo_sum = None
                        for r in range(SC_ROWS_PER_STEP):
                            word = loaded[r][q]
                            hi = weights[r] * lax.bitcast_convert_type(word & jnp.int32(-65536), F32)
                            lo = weights[r] * lax.bitcast_convert_type(word << 16, F32)
                            hi_sum = hi if hi_sum is None else hi_sum + hi
                            lo_sum = lo if lo_sum is None else lo_sum + lo
                        col = (g * SC_CHUNKS + q) * 2 * SC_LANES
                        plsc.addupdate(out_v.at[pl.ds(col, SC_LANES)], hi_sum)
                        plsc.addupdate(out_v.at[pl.ds(col + SC_LANES, SC_LANES)], lo_sum)

        @pl.loop(0, per_worker)
        def _(i):
            tok = base + i
            pltpu.sync_copy(expert_hbm.at[tok], idx_v)
            first_half = pltpu.async_copy(table_hbm.at[idx_v.at[pl.ds(0, SC_HALF)]],
                                          rows_v.at[0], sem0)
            second_half = pltpu.async_copy(table_hbm.at[idx_v.at[pl.ds(SC_HALF, SC_HALF)]],
                                           rows_v.at[1], sem1)
            pltpu.sync_copy(w_hbm.at[local + i], w_v)
            pltpu.sync_copy(x_hbm.at[tok], out_v)
            first_half.wait()
            accumulate(0)
            second_half.wait()
            accumulate(1)
            pltpu.sync_copy(out_v, out_hbm.at[tok])

    return run(expert, w_lanes, x2d, table)


def _peer_constants():
    col = jnp.arange(DOWN_K)
    src = ((col // SUBLANES) % 2) * PEER_SLOTS + col // PAIR_ROWS
    spread = (jnp.arange(2 * PEER_SLOTS)[:, None] == src[None, :]).astype(BF16)
    diag = (col[None, :] % SUBLANES == jnp.arange(SUBLANES)[:, None]).astype(F32)
    lane_repeat = jnp.repeat(jnp.eye(PEER_SLOTS, dtype=BF16), SC_LANES, axis=1)
    return spread, diag, lane_repeat


def _pair_slabs(v):
    e, d = v.shape
    return v.astype(BF16).reshape(e // 2, 2 * d // LANES, LANES)


def _peer(x2d, g, wq_bf16, keys_bf16, u_packed, u_words, v_slabs, v_words, consts, *, tm_route,
          tm_expert, n_sparsecore):
    t, d = x2d.shape
    spread, diag, lane_repeat = consts
    n_tc = t - n_sparsecore
    assert n_sparsecore > 0 and n_sparsecore % SC_WORKERS == 0 and n_tc % tm_expert == 0
    xn, pair, parity, gate = _peer_route(x2d, g, wq_bf16, keys_bf16, tm=tm_route)
    pair, parity, gate = pair.T, parity.T, gate.T
    xn3 = xn.reshape(t, d // LANES, LANES)
    x3 = x2d.reshape(t, d // LANES, LANES)
    expert = pair * 2 + parity
    h_sc = _sc_peer_up(expert, xn, u_words, first=n_tc, n_tokens=n_sparsecore)
    wexp = _peer_up(pair, parity, xn3, u_packed, gate, spread, tm=tm_expert, n_tokens=n_tc)
    w_lanes = _peer_act(h_sc, gate, lane_repeat, wexp, first=n_tc, tm=tm_expert)
    out = _sc_peer_down(expert, w_lanes, x2d, v_words, first=n_tc, n_tokens=n_sparsecore)
    out3 = _peer_down(pair, wexp, diag, x3, v_slabs, tm=tm_expert, n_tokens=n_tc)
    return lax.dynamic_update_slice(out, out3.reshape(n_tc, d), (0, 0))


def _final_norm_kernel(x_ref, g_ref, o_ref):
    o_ref[...] = _rms_scale(x_ref[...], g_ref[...])


def _final_norm(x2d, g, *, tm):
    t, d = x2d.shape
    return pl.pallas_call(
        _final_norm_kernel,
        grid=(t // tm,),
        in_specs=[pl.BlockSpec((tm, d), lambda i: (i, 0)), _resident((1, d))],
        out_specs=pl.BlockSpec((tm, d), lambda i: (i, 0)),
        out_shape=jax.ShapeDtypeStruct((t, d), F32),
        compiler_params=_params(1),
        name="final_norm",
    )(x2d, g.reshape(1, d))


def _tiles(seq):
    return dict(tm_proj=min(256, seq), tm_out=min(512, seq), tq=min(512, seq), tk=min(1024, seq),
                tm_route=LANES, tm_expert=LANES, tm_norm=min(512, seq))


SC_SHARE = 0.3515625


def _sparsecore_tokens(t, tm):
    unit = tm * SC_WORKERS // math.gcd(tm, SC_WORKERS)
    return int(t * SC_SHARE) // unit * unit


def _to_residue_major(a2d, batch, seq, dilation, c0, c1):
    cols = a2d[:, c0:c1].reshape(batch, seq // dilation, dilation, c1 - c0)
    return cols.transpose(0, 2, 1, 3)


def _from_residue_major(a4):
    b, d, length, c = a4.shape
    return a4.transpose(0, 2, 1, 3).reshape(b * d * length, c)


def kernel(x, mem, norm_mix, a_w_in, a_lambda, a_subln, b_w_in, norm_mem, w_mem_kv, w_out,
           norm_ffn, peer_wq, peer_keys, peer_u, peer_v, shared_norm, shared_w_kv, final_norm):
    batch, seq, d = x.shape
    mem_tokens = mem.shape[1]
    depth = norm_mix.shape[0]
    n_a = a_w_in.shape[0]
    t = batch * seq
    tiles = _tiles(seq)
    rope = _rope_tables(seq)
    scale = HEAD_DIM ** -0.5

    diff_qk = DIFF_HEADS * 2 * HEAD_DIM
    dil_qk = len(DIL_GROUPS) * DIL_GROUP_WIDTH
    peer_consts = _peer_constants()
    n_experts = peer_u.shape[1]
    stacked = (depth * n_experts, d)
    u_words = _sc_word_table(peer_u.reshape(stacked)).reshape(depth, n_experts, d // 2)
    v_words = _sc_word_table(peer_v.reshape(stacked)).reshape(depth, n_experts, d // 2)
    expand = jnp.repeat(jnp.eye(LANES, DIL_HEADS, dtype=BF16), HEAD_DIM, axis=1)

    x2d = x.reshape(t, d)
    mem2d = mem.reshape(batch * mem_tokens, d)
    shared = None

    for l in range(depth):
        memkv = _proj(mem2d, norm_mem[l], w_mem_kv[l].astype(BF16), tm=mem_tokens)
        w_o = w_out[l].astype(BF16)
        if l < n_a:
            col_scale = jnp.concatenate([jnp.full((diff_qk,), scale, F32),
                                         jnp.ones((a_w_in.shape[2] - diff_qk - MEM_Q_WIDTH,), F32),
                                         jnp.full((MEM_Q_WIDTH,), scale, F32)])
            w_in = (a_w_in[l] * col_scale).astype(BF16)
            proj = _proj(x2d, norm_mix[l], w_in, tm=tiles["tm_proj"], n_rope=2 * diff_qk,
                         rope=rope, seq=seq)
            lambda_init = 0.8 - 0.6 * math.exp(-0.3 * l)
            mix = _diff_attention(proj, a_lambda[l], a_subln[l], batch=batch, seq=seq,
                                  lambda_init=lambda_init, tq=tiles["tq"], tk=tiles["tk"])
            x2d = _outproj(x2d, proj, memkv, w_o, seq=seq, mem_tokens=mem_tokens,
                           tm=tiles["tm_out"], mix=mix)
        else:
            w_in = (b_w_in[l - n_a] * scale).astype(BF16)
            proj = _proj(x2d, norm_mix[l], w_in, tm=tiles["tm_proj"], n_rope=dil_qk,
                         rope=rope, seq=seq)
            outs, stats = [], []
            for gi, (window, dilation) in enumerate(DIL_GROUPS):
                c0 = gi * DIL_GROUP_WIDTH
                blk0 = c0 // LANES
                if dilation == 1:
                    q4, q_off = proj.reshape(batch, 1, seq, proj.shape[1]), blk0
                else:
                    q4, q_off = _to_residue_major(proj, batch, seq, dilation,
                                                  c0, c0 + DIL_GROUP_WIDTH), 0
                k4, k_off, v4, v_off = shared[gi]
                o4, st4 = _dilated_group(q4, k4, v4, q_off=q_off, k_off=k_off, v_off=v_off,
                                         n_steps=window // dilation)
                outs.append(_from_residue_major(o4))
                stats.append(_from_residue_major(st4))
            x2d = _outproj(x2d, proj, memkv, w_o, seq=seq, mem_tokens=mem_tokens,
                           tm=tiles["tm_out"], group_out=outs, group_stats=stats, expand=expand)

        x2d = _peer(x2d, norm_ffn[l], peer_wq[l].astype(BF16),
                    peer_keys[l].reshape(2 * PEER_HEADS, PEER_N_KEYS, PEER_HALF).astype(BF16),
                    _pack_expert_table(peer_u[l]), u_words[l],
                    _pair_slabs(peer_v[l]), v_words[l], peer_consts,
                    tm_route=tiles["tm_route"], tm_expert=tiles["tm_expert"],
                    n_sparsecore=_sparsecore_tokens(t, tiles["tm_expert"]))

        if l == n_a - 1:
            kv = _proj(x2d, shared_norm, shared_w_kv.astype(BF16), tm=tiles["tm_proj"],
                       n_rope=dil_qk, rope=rope, seq=seq)
            shared = []
            for gi, (window, dilation) in enumerate(DIL_GROUPS):
                c0 = gi * DIL_GROUP_WIDTH
                if dilation == 1:
                    kv4 = kv.reshape(batch, 1, seq, kv.shape[1])
                    shared.append((kv4, c0 // LANES, kv4, (dil_qk + c0) // LANES))
                else:
                    k4 = _to_residue_major(kv, batch, seq, dilation, c0, c0 + DIL_GROUP_WIDTH)
                    v4 = _to_residue_major(kv, batch, seq, dilation, dil_qk + c0,
                                           dil_qk + c0 + DIL_GROUP_WIDTH)
                    shared.append((k4, 0, v4, 0))

    return _final_norm(x2d, final_norm, tm=tiles["tm_norm"]).reshape(batch, seq, d)
```

```python
import functools
import math

import jax
import jax.numpy as jnp
from jax import lax
from jax.experimental import pallas as pl
from jax.experimental.pallas import tpu as pltpu
from jax.experimental.pallas import tpu_sc as plsc

F32 = jnp.float32
BF16 = jnp.bfloat16

HEAD_DIM = 64
ROPE_DIMS = HEAD_DIM // 4
ROPE_HALF = ROPE_DIMS // 2
ROPE_THETA = 500000.0
NORM_EPS = 1e-5
DIFF_HEADS = 6
MEM_HEADS = 4
MEM_Q_WIDTH = MEM_HEADS * HEAD_DIM
DIL_GROUPS = ((128, 1), (512, 4), (2048, 16))
DIL_HEADS = 12
DIL_BLOCK = 128
DIL_GROUP_WIDTH = DIL_HEADS * HEAD_DIM
PEER_HEADS = 8
PEER_N_KEYS = 128
PEER_TOPK = 16
PEER_HALF = 128
PEER_SLOTS = PEER_HEADS * PEER_TOPK

LANES = 128
SUBLANES = 8
VMEM_LIMIT_BYTES = 56 * 1024 * 1024

_NT = (((1,), (1,)), ((), ()))


def _params(n_axes):
    return pltpu.CompilerParams(dimension_semantics=("arbitrary",) * n_axes,
                                vmem_limit_bytes=VMEM_LIMIT_BYTES)


def _resident(shape):
    zeros = (0,) * len(shape)
    return pl.BlockSpec(shape, lambda *_: zeros, pipeline_mode=pl.Buffered(1))


def _rms_scale(x, g):
    ms = jnp.mean(x * x, axis=-1, keepdims=True)
    return x * lax.rsqrt(ms + NORM_EPS) * g


def _rope_tables(seq):
    inv = ROPE_THETA ** (-jnp.arange(0, ROPE_DIMS, 2, dtype=F32) / ROPE_DIMS)
    ang = jnp.arange(seq, dtype=F32)[:, None] * inv[None, :]
    cos, sin = jnp.cos(ang), jnp.sin(ang)
    ones = jnp.ones((seq, HEAD_DIM - ROPE_DIMS), F32)
    zeros = jnp.zeros((seq, HEAD_DIM - ROPE_DIMS), F32)
    zh = jnp.zeros((seq, ROPE_HALF), F32)
    c = jnp.concatenate([cos, cos, ones], axis=1)
    sa = jnp.concatenate([-sin, zh, zeros], axis=1)
    sb = jnp.concatenate([zh, sin, zeros], axis=1)
    rep = LANES // HEAD_DIM
    return tuple(jnp.tile(t, (1, rep)) for t in (c, sa, sb))


def _proj_kernel(*refs, n_rope, chunk):
    if n_rope:
        x_ref, g_ref, w_ref, c_ref, sa_ref, sb_ref, o_ref = refs
    else:
        x_ref, g_ref, w_ref, o_ref = refs
    y = _rms_scale(x_ref[...], g_ref[...]).astype(BF16)
    n = o_ref.shape[1]
    for c0 in range(0, n, chunk):
        acc = jnp.dot(y, w_ref[:, c0:c0 + chunk], preferred_element_type=F32)
        if c0 < n_rope:
            for k0 in range(0, chunk, LANES):
                a = acc[:, k0:k0 + LANES]
                a = (a * c_ref[...]
                     + pltpu.roll(a, LANES - ROPE_HALF, 1) * sa_ref[...]
                     + pltpu.roll(a, ROPE_HALF, 1) * sb_ref[...])
                o_ref[:, c0 + k0:c0 + k0 + LANES] = a.astype(o_ref.dtype)
        else:
            o_ref[:, c0:c0 + chunk] = acc.astype(o_ref.dtype)


def _proj(x2d, g, w_bf16, *, tm, n_rope=0, rope=None, seq=None):
    t, d = x2d.shape
    n = w_bf16.shape[1]
    chunk = 256
    assert t % tm == 0 and n % chunk == 0 and n_rope % chunk == 0
    in_specs = [pl.BlockSpec((tm, d), lambda i: (i, 0)),
                _resident((1, d)),
                _resident((d, n))]
    args = [x2d, g.reshape(1, d), w_bf16]
    if n_rope:
        nblk = seq // tm
        spec = pl.BlockSpec((tm, LANES), lambda i: (i % nblk, 0))
        in_specs += [spec, spec, spec]
        args += list(rope)
    return pl.pallas_call(
        functools.partial(_proj_kernel, n_rope=n_rope, chunk=chunk),
        grid=(t // tm,),
        in_specs=in_specs,
        out_specs=pl.BlockSpec((tm, n), lambda i: (i, 0)),
        out_shape=jax.ShapeDtypeStruct((t, n), BF16),
        compiler_params=_params(1),
        name="norm_proj",
    )(*args)


def _online_softmax_step(q, k, v, carry, mask):
    m, l, acc = carry
    s = lax.dot_general(q, k, _NT, preferred_element_type=F32)
    if mask is not None:
        s = jnp.where(mask, s, -jnp.inf)
    m_new = jnp.maximum(m, jnp.max(s, axis=-1, keepdims=True))
    alpha = jnp.exp(m - m_new)
    p = jnp.exp(s - m_new)
    l = alpha * l + jnp.sum(p, axis=-1, keepdims=True)
    acc = alpha * acc + jnp.dot(p.astype(BF16), v, preferred_element_type=F32)
    return m_new, l, acc


def _diff_attn_kernel(q_ref, k_ref, v_ref, lp_ref, g_ref, o_ref, *, tq, tk, lambda_init):
    qi = pl.program_id(2)
    q = q_ref[...]
    lane = lax.broadcasted_iota(jnp.int32, q.shape, 1)
    zero = jnp.zeros_like(q)
    qa = jnp.where(lane < HEAD_DIM, q, zero)
    qb = jnp.where(lane >= HEAD_DIM, q, zero)

    def init():
        return (jnp.full((tq, 1), -jnp.inf, F32), jnp.zeros((tq, 1), F32),
                jnp.zeros((tq, 2 * HEAD_DIM), F32))

    def body(j, carry):
        c1, c2 = carry
        k = k_ref[pl.ds(pl.multiple_of(j * tk, tk), tk), :]
        v = v_ref[pl.ds(pl.multiple_of(j * tk, tk), tk), :]
        return (_online_softmax_step(qa, k, v, c1, None),
                _online_softmax_step(qb, k, v, c2, None))

    n_full = (qi * tq) // tk
    c1, c2 = lax.fori_loop(0, n_full, body, (init(), init()))
    row = qi * tq + lax.broadcasted_iota(jnp.int32, (tq, tk), 0)
    col = n_full * tk + lax.broadcasted_iota(jnp.int32, (tq, tk), 1)
    causal = col <= row
    start = pl.multiple_of(n_full * tk, tk)
    k = k_ref[pl.ds(start, tk), :]
    v = v_ref[pl.ds(start, tk), :]
    m1, l1, a1 = _online_softmax_step(qa, k, v, c1, causal)
    m2, l2, a2 = _online_softmax_step(qb, k, v, c2, causal)

    lp = lp_ref[...]
    lam = (jnp.exp(jnp.sum(lp[0:1] * lp[1:2], axis=-1, keepdims=True))
           - jnp.exp(jnp.sum(lp[2:3] * lp[3:4], axis=-1, keepdims=True)) + lambda_init)
    o = a1 / l1 - lam * (a2 / l2)
    o = _rms_scale(o, g_ref[...]) * (1.0 - lambda_init)
    o_ref[...] = o.astype(o_ref.dtype)


def _diff_attention(proj, lp, subln, *, batch, seq, lambda_init, tq, tk):
    assert tk % tq == 0 and seq % tk == 0
    nq = seq // tq
    width = 2 * HEAD_DIM
    return pl.pallas_call(
        functools.partial(_diff_attn_kernel, tq=tq, tk=tk, lambda_init=lambda_init),
        grid=(batch, DIFF_HEADS, nq),
        in_specs=[
            pl.BlockSpec((tq, width), lambda b, h, i: (b * nq + i, h)),
            pl.BlockSpec((seq, width), lambda b, h, i: (b, DIFF_HEADS + h)),
            pl.BlockSpec((seq, width), lambda b, h, i: (b, 2 * DIFF_HEADS + h)),
            _resident((4, HEAD_DIM)),
            _resident((1, width)),
        ],
        out_specs=pl.BlockSpec((tq, width), lambda b, h, i: (b * nq + i, h)),
        out_shape=jax.ShapeDtypeStruct((batch * seq, DIFF_HEADS * width), BF16),
        compiler_params=_params(3),
        name="diff_attention",
    )(proj, proj, proj, lp, subln.reshape(1, width))


def _dilated_kernel(q_ref, kc_ref, kp_ref, vc_ref, vp_ref, o_ref, st_ref, *, tq, n_steps):
    n = pl.program_id(2)
    hp = pl.program_id(3)
    blk = DIL_BLOCK

    @pl.when(hp == 0)
    def _():
        st_ref[...] = jnp.zeros_like(st_ref)

    qrow = lax.broadcasted_iota(jnp.int32, (blk, 2 * blk), 0)
    kcol = lax.broadcasted_iota(jnp.int32, (blk, 2 * blk), 1)
    dist = qrow + blk - kcol
    band = (dist >= 0) & (dist <= n_steps)
    lane_q = lax.broadcasted_iota(jnp.int32, (blk, LANES), 1)
    lane_s = lax.broadcasted_iota(jnp.int32, (blk, LANES), 1)

    for r, i in [(r, i) for r in range(q_ref.shape[0]) for i in range(tq // blk)]:
        rows = slice(i * blk, (i + 1) * blk)
        q = q_ref[r, rows, :]
        if i == 0:
            k_prev, v_prev = kp_ref[r], vp_ref[r]
            valid = band & ((n > 0) | (kcol >= blk))
        else:
            prev = slice((i - 1) * blk, i * blk)
            k_prev, v_prev = kc_ref[r, prev, :], vc_ref[r, prev, :]
            valid = band
        keys = jnp.concatenate([k_prev, kc_ref[r, rows, :]], axis=0)
        vals = jnp.concatenate([v_prev, vc_ref[r, rows, :]], axis=0)
        out = jnp.zeros((blk, LANES), F32)
        stats = st_ref[r, rows, :]
        for e in range(LANES // HEAD_DIM):
            head_lanes = (lane_q // HEAD_DIM) == e
            qe = jnp.where(head_lanes, q, jnp.zeros_like(q))
            s = lax.dot_general(qe, keys, _NT, preferred_element_type=F32)
            s = jnp.where(valid, s, -jnp.inf)
            m = jnp.max(s, axis=-1, keepdims=True)
            p = jnp.exp(s - m)
            den = jnp.sum(p, axis=-1, keepdims=True)
            oe = jnp.dot((p / den).astype(BF16), vals, preferred_element_type=F32)
            out = jnp.where(head_lanes, oe, out)
            lse = m + jnp.log(den)
            stats = jnp.where(lane_s == hp * (LANES // HEAD_DIM) + e, lse, stats)
        o_ref[r, rows, :] = out.astype(o_ref.dtype)
        st_ref[r, rows, :] = stats


def _dilated_group(q4, k4, v4, *, q_off, k_off, v_off, n_steps):
    b, d, length, _ = q4.shape
    rows_per_step = 1024
    tq = min(rows_per_step, length)
    nres = min(d, rows_per_step // tq)
    assert length % tq == 0 and tq % DIL_BLOCK == 0 and d % nres == 0
    sub = tq // DIL_BLOCK
    pairs = DIL_GROUP_WIDTH // LANES

    def cur(off):
        return pl.BlockSpec((None, nres, tq, LANES), lambda bi, r, n, h: (bi, r, n, off + h))

    def prev(off):
        return pl.BlockSpec((None, nres, DIL_BLOCK, LANES),
                            lambda bi, r, n, h: (bi, r, jnp.maximum(n * sub - 1, 0), off + h))

    return pl.pallas_call(
        functools.partial(_dilated_kernel, tq=tq, n_steps=n_steps),
        grid=(b, d // nres, length // tq, pairs),
        in_specs=[cur(q_off), cur(k_off), prev(k_off), cur(v_off), prev(v_off)],
        out_specs=[pl.BlockSpec((None, nres, tq, LANES), lambda bi, r, n, h: (bi, r, n, h)),
                   pl.BlockSpec((None, nres, tq, LANES), lambda bi, r, n, h: (bi, r, n, 0))],
        out_shape=[jax.ShapeDtypeStruct((b, d, length, DIL_GROUP_WIDTH), BF16),
                   jax.ShapeDtypeStruct((b, d, length, LANES), F32)],
        compiler_params=_params(4),
        name="dilated_attention",
    )(q4, k4, k4, v4, v4)


def _memory_attention(q, k, v):
    lane = lax.broadcasted_iota(jnp.int32, q.shape, 1)
    out = jnp.zeros(q.shape, F32)
    for h in range(MEM_HEADS):
        head_lanes = (lane // HEAD_DIM) == h
        qh = jnp.where(head_lanes, q, jnp.zeros_like(q))
        s = lax.dot_general(qh, k, _NT, preferred_element_type=F32)
        m = jnp.max(s, axis=-1, keepdims=True)
        p = jnp.exp(s - m)
        den = jnp.sum(p, axis=-1, keepdims=True)
        oh = jnp.dot((p / den).astype(BF16), v, preferred_element_type=F32)
        out = jnp.where(head_lanes, oh, out)
    return out


def _split_bf16(x):
    hi = x.astype(BF16)
    lo = (x - hi.astype(F32)).astype(BF16)
    return hi, lo


def _outproj_kernel(*refs, n_groups):
    if n_groups:
        x_ref, qm_ref, mk_ref, mv_ref, w_ref = refs[:5]
        o_refs = refs[5:5 + n_groups]
        st_refs = refs[5 + n_groups:5 + 2 * n_groups]
        ex_ref, out_ref = refs[5 + 2 * n_groups:]
        lses = [r[...] for r in st_refs]
        top = functools.reduce(jnp.maximum, lses)
        es = [jnp.exp(l - top) for l in lses]
        den = functools.reduce(lambda a, b: a + b, es)
        mix = None
        for e, o_ref in zip(es, o_refs):
            hi, lo = _split_bf16(e / den)
            wide = (jnp.dot(hi, ex_ref[...], preferred_element_type=F32)
                    + jnp.dot(lo, ex_ref[...], preferred_element_type=F32))
            term = wide * o_ref[...].astype(F32)
            mix = term if mix is None else mix + term
        mix = mix.astype(BF16)
    else:
        x_ref, qm_ref, mk_ref, mv_ref, w_ref, mix_ref, out_ref = refs
        mix = mix_ref[...]
    mo = _memory_attention(qm_ref[...], mk_ref[...], mv_ref[...]).astype(BF16)
    k_mix = mix.shape[1]
    acc = jnp.dot(mix, w_ref[:k_mix, :], preferred_element_type=F32)
    acc += jnp.dot(mo, w_ref[k_mix:, :], preferred_element_type=F32)
    out_ref[...] = x_ref[...] + acc


def _outproj(x2d, proj, memkv, w_bf16, *, seq, mem_tokens, tm, mix=None, group_out=None,
             group_stats=None, expand=None):
    t, d = x2d.shape
    per_batch = seq // tm
    qm_block = (proj.shape[1] - MEM_Q_WIDTH) // MEM_Q_WIDTH
    row = lambda i: (i, 0)
    in_specs = [pl.BlockSpec((tm, d), row),
                pl.BlockSpec((tm, MEM_Q_WIDTH), lambda i: (i, qm_block)),
                pl.BlockSpec((mem_tokens, MEM_Q_WIDTH), lambda i: (i // per_batch, 0)),
                pl.BlockSpec((mem_tokens, MEM_Q_WIDTH), lambda i: (i // per_batch, 1)),
                _resident(w_bf16.shape)]
    args = [x2d, proj, memkv, memkv, w_bf16]
    if mix is not None:
        n_groups = 0
        in_specs.append(pl.BlockSpec((tm, mix.shape[1]), row))
        args.append(mix)
    else:
        n_groups = len(group_out)
        in_specs += [pl.BlockSpec((tm, DIL_GROUP_WIDTH), row)] * n_groups
        in_specs += [pl.BlockSpec((tm, LANES), row)] * n_groups
        in_specs.append(_resident(expand.shape))
        args += list(group_out) + list(group_stats) + [expand]
    return pl.pallas_call(
        functools.partial(_outproj_kernel, n_groups=n_groups),
        grid=(t // tm,),
        in_specs=in_specs,
        out_specs=pl.BlockSpec((tm, d), row),
        out_shape=jax.ShapeDtypeStruct((t, d), F32),
        compiler_params=_params(1),
        name="mix_outproj",
    )(*args)


def _topk_rows(s, k, payload=None):
    n = s.shape[0]
    row = lax.broadcasted_iota(jnp.int32, s.shape, 0)
    vals, picked = [], []
    for _ in range(k):
        m = jnp.max(s, axis=0, keepdims=True)
        i = jnp.min(jnp.where(s == m, row, n), axis=0, keepdims=True)
        hit = row == i
        vals.append(m)
        if payload is None:
            picked.append(i)
        else:
            picked.append(jnp.sum(jnp.where(hit, payload, 0), axis=0, keepdims=True))
        s = jnp.where(hit, -jnp.inf, s)
    return jnp.concatenate(vals, axis=0), jnp.concatenate(picked, axis=0)


def _peer_route_kernel(x_ref, g_ref, wq_ref, keys_ref, xn_ref, pair_ref, parity_ref, gate_ref):
    xn = _rms_scale(x_ref[...], g_ref[...])
    xn_ref[...] = xn
    q = jnp.dot(xn.astype(BF16), wq_ref[...], preferred_element_type=F32).astype(BF16)
    kk = PEER_TOPK
    for h in range(PEER_HEADS):
        tops = []
        for half in range(2):
            c0 = (2 * h + half) * PEER_HALF
            s = lax.dot_general(keys_ref[2 * h + half], q[:, c0:c0 + PEER_HALF], _NT,
                                preferred_element_type=F32)
            tops.append(_topk_rows(s, kk))
        (s1, i1), (s2, i2) = tops
        half = kk // 2
        sub = lax.broadcasted_iota(jnp.int32, (half, s1.shape[1]), 0)
        blocks = [s1[0:1] + s2]
        experts = [i1[0:1] * PEER_N_KEYS + i2]
        for a in range(1, half):
            blocks.append(jnp.where(sub < kk // (a + 1), s1[a:a + 1] + s2[:half], -jnp.inf))
            experts.append(i1[a:a + 1] * PEER_N_KEYS + i2[:half])
        blocks.append(s1[half:] + s2[0:1])
        experts.append(i1[half:] * PEER_N_KEYS + i2[0:1])
        cand = jnp.concatenate(blocks, axis=0)
        cand_expert = jnp.concatenate(experts, axis=0)
        top_s, expert = _topk_rows(cand, kk, payload=cand_expert)
        e = jnp.exp(top_s - top_s[0:1])
        gate = e / jnp.sum(e, axis=0, keepdims=True)
        rows = slice(h * kk, (h + 1) * kk)
        pair_ref[rows, :] = expert >> 1
        parity_ref[rows, :] = expert & 1
        gate_ref[rows, :] = gate


def _peer_route(x2d, g, wq_bf16, keys_bf16, *, tm):
    t, d = x2d.shape
    col = lambda i: (0, i)
    slot_shape = jax.ShapeDtypeStruct((PEER_SLOTS, t), jnp.int32)
    return pl.pallas_call(
        _peer_route_kernel,
        grid=(t // tm,),
        in_specs=[pl.BlockSpec((tm, d), lambda i: (i, 0)),
                  _resident((1, d)),
                  _resident(wq_bf16.shape),
                  _resident(keys_bf16.shape)],
        out_specs=[pl.BlockSpec((tm, d), lambda i: (i, 0)),
                   pl.BlockSpec((PEER_SLOTS, tm), col),
                   pl.BlockSpec((PEER_SLOTS, tm), col),
                   pl.BlockSpec((PEER_SLOTS, tm), col)],
        out_shape=[jax.ShapeDtypeStruct((t, d), F32), slot_shape, slot_shape,
                   jax.ShapeDtypeStruct((PEER_SLOTS, t), F32)],
        compiler_params=_params(1),
        name="peer_route",
    )(x2d, g.reshape(1, d), wq_bf16, keys_bf16)


_HI_MASK = 0xFFFF0000


def _pack_expert_table(u):
    e, d = u.shape
    bits = lax.bitcast_convert_type(u.astype(BF16), jnp.uint16).astype(jnp.uint32)
    bits = bits.reshape(e // 2, 2, d)
    packed = (bits[:, 0] << 16) | bits[:, 1]
    return packed.reshape(e // 2, d // LANES, LANES)


_BIT_REVERSED = (0, 4, 2, 6, 1, 5, 3, 7)


def _bf16_pair_words(x):
    bits = pltpu.bitcast(x, jnp.uint32)
    top = (bits + jnp.uint32(0x7FFF) + ((bits >> 16) & jnp.uint32(1))) >> 16
    return (top << 16) | top


def _packed_add(a, b):
    return pltpu.bitcast(pltpu.bitcast(a, BF16) + pltpu.bitcast(b, BF16), jnp.uint32)


def _sublane_sums(words, sub):
    level = [words[i] for i in _BIT_REVERSED]
    for k in (4, 2, 1):
        low = (sub & k) == 0
        merged = []
        for a, b in zip(level[0::2], level[1::2]):
            if k == 4:
                merged.append(_packed_add(jnp.where(low, a, b),
                                          pltpu.roll(jnp.where(low, b, a), k, 0)))
            else:
                merged.append(jnp.where(low,
                                        _packed_add(a, pltpu.roll(a, SUBLANES - k, 0)),
                                        _packed_add(b, pltpu.roll(b, k, 0))))
        level = merged
    return level[0]


def _peer_up_kernel(pair_ref, x_ref, tab_ref, gate_ref, par_ref, spread_ref, wexp_ref, w2_ref):
    tm = x_ref.shape[0]
    sub = lax.broadcasted_iota(jnp.int32, (SUBLANES, LANES), 0)
    hi_mask = jnp.uint32(_HI_MASK)

    def token_sums(t, m):
        xx = pltpu.bitcast(_bf16_pair_words(x_ref[t]), BF16)
        even, odd = [], []
        for g in range(PEER_SLOTS // SUBLANES):
            words = []
            for i in range(SUBLANES):
                slab = pltpu.bitcast(tab_ref[pair_ref[t, g * SUBLANES + i]], BF16)
                words.append(pltpu.bitcast(slab * xx, jnp.uint32))
            q = _sublane_sums(words, sub)
            even.append(pltpu.bitcast(q & hi_mask, F32))
            odd.append(pltpu.bitcast(q << 16, F32))
        lane_parts = jnp.concatenate(even + odd, axis=0).astype(BF16)
        onehot = (sub == m).astype(BF16)
        return lax.dot_general(onehot, lane_parts, _NT, preferred_element_type=F32)

    def group(gi, _):
        base = pl.multiple_of(gi * SUBLANES, SUBLANES)
        h2 = token_sums(base, 0)
        for m in range(1, SUBLANES):
            h2 = h2 + token_sums(base + m, m)
        odd_slot = par_ref[pl.ds(base, SUBLANES), :] != 0
        h = jnp.where(odd_slot, h2[:, PEER_SLOTS:], h2[:, :PEER_SLOTS])
        act = 0.5 * h * (1.0 + lax.erf(h * (1.0 / math.sqrt(2.0))))
        w = gate_ref[pl.ds(base, SUBLANES), :] * act
        zero = jnp.zeros_like(w)
        w2_ref[pl.ds(base, SUBLANES), 0:PEER_SLOTS] = jnp.where(odd_slot, zero, w)
        w2_ref[pl.ds(base, SUBLANES), PEER_SLOTS:2 * PEER_SLOTS] = jnp.where(odd_slot, w, zero)
        return 0

    lax.fori_loop(0, tm // SUBLANES, group, 0)
    hi, lo = _split_bf16(w2_ref[...])
    wexp_ref[...] = (jnp.dot(hi, spread_ref[...], preferred_element_type=F32)
                     + jnp.dot(lo, spread_ref[...], preferred_element_type=F32))


PAIR_ROWS = 2 * SUBLANES
DOWN_K = PEER_SLOTS * PAIR_ROWS
DOWN_PAIRS_PER_STEP = 4


def _peer_down_kernel(pair_ref, wexp_ref, diag_ref, x_ref, tab_ref, o_ref):
    tm = x_ref.shape[0]

    def left_rows(t):
        row = wexp_ref[pl.ds(t, 1), :]
        hi, lo = _split_bf16(row * diag_ref[...])
        return [hi, lo]

    def token_pair(ta):
        tb = ta + 1
        rhs = jnp.concatenate(
            [jnp.concatenate([tab_ref[pair_ref[ta, j]], tab_ref[pair_ref[tb, j]]], axis=1)
             for j in range(PEER_SLOTS)], axis=0)
        lhs = jnp.concatenate(left_rows(ta) + left_rows(tb), axis=0)
        out = jnp.dot(lhs, rhs, preferred_element_type=F32)
        s = SUBLANES
        o_ref[ta] = x_ref[ta] + (out[0:s, :LANES] + out[s:2 * s, :LANES])
        o_ref[tb] = x_ref[tb] + (out[2 * s:3 * s, LANES:] + out[3 * s:4 * s, LANES:])

    def step(i, _):
        for k in range(DOWN_PAIRS_PER_STEP):
            token_pair(2 * (DOWN_PAIRS_PER_STEP * i + k))
        return 0

    lax.fori_loop(0, tm // (2 * DOWN_PAIRS_PER_STEP), step, 0)


def _smem_rows(tm):
    return pl.BlockSpec((tm, PEER_SLOTS), lambda i: (i, 0), memory_space=pltpu.SMEM)


def _peer_up(pair, parity, xn3, table, gate, spread, *, tm, n_tokens):
    rows = pl.BlockSpec((tm, PEER_SLOTS), lambda i: (i, 0))
    return pl.pallas_call(
        _peer_up_kernel,
        grid=(n_tokens // tm,),
        in_specs=[_smem_rows(tm),
                  pl.BlockSpec((tm, SUBLANES, LANES), lambda i: (i, 0, 0)),
                  _resident(table.shape), rows, rows, _resident(spread.shape)],
        out_specs=pl.BlockSpec((tm, DOWN_K), lambda i: (i, 0)),
        out_shape=jax.ShapeDtypeStruct((n_tokens, DOWN_K), F32),
        scratch_shapes=[pltpu.VMEM((tm, 2 * PEER_SLOTS), F32)],
        compiler_params=_params(1),
        name="peer_up",
    )(pair, xn3, table, gate, parity, spread)


def _peer_down(pair, wexp, diag, x3, table, *, tm, n_tokens):
    tile = pl.BlockSpec((tm, SUBLANES, LANES), lambda i: (i, 0, 0))
    return pl.pallas_call(
        _peer_down_kernel,
        grid=(n_tokens // tm,),
        in_specs=[_smem_rows(tm), pl.BlockSpec((tm, DOWN_K), lambda i: (i, 0)),
                  _resident(diag.shape), tile, _resident(table.shape)],
        out_specs=tile,
        out_shape=jax.ShapeDtypeStruct((n_tokens,) + x3.shape[1:], F32),
        compiler_params=_params(1),
        name="peer_down",
    )(pair, wexp, diag, x3, table)


SC_CORES = 2
SC_SUBCORES = 16
SC_WORKERS = SC_CORES * SC_SUBCORES
SC_LANES = 16
SC_HALF = PEER_SLOTS // 2
SC_ROWS_PER_STEP = 4
SC_CHUNKS = 8


SC_TABLE_ROWS = 64


def _sc_word_table(v):
    e, d = v.shape
    per_worker = e // SC_WORKERS
    words = d // 2
    mesh = plsc.VectorSubcoreMesh(core_axis_name="c", subcore_axis_name="s")

    @functools.partial(
        pl.kernel, mesh=mesh,
        out_type=jax.ShapeDtypeStruct((e, words), jnp.int32),
        scratch_types=[pltpu.VMEM((SC_TABLE_ROWS, d), F32),
                       pltpu.VMEM((SC_TABLE_ROWS, words), jnp.int32)],
        compiler_params=pltpu.CompilerParams(needs_layout_passes=False),
        name="sc_word_table",
    )
    def run(v_hbm, out_hbm, in_v, out_v):
        worker = lax.axis_index("s") * SC_CORES + lax.axis_index("c")
        base = worker * per_worker

        def bf16_bits(x):
            bits = lax.bitcast_convert_type(x, jnp.int32)
            lsb = lax.shift_right_logical(bits, 16) & 1
            rounded = (bits + 0x7FFF + lsb) & jnp.int32(-65536)
            subnormal = (bits & jnp.int32(0x7F800000)) == 0
            return jnp.where(subnormal, bits & jnp.int32(-2147483648), rounded)

        @pl.loop(0, per_worker, step=SC_TABLE_ROWS)
        def _(r0):
            pltpu.sync_copy(v_hbm.at[pl.ds(base + r0, SC_TABLE_ROWS)], in_v)

            @pl.loop(0, SC_TABLE_ROWS)
            def _(r):
                for m in range(words // SC_LANES):
                    hi = bf16_bits(in_v[r, pl.ds(2 * m * SC_LANES, SC_LANES)])
                    lo = bf16_bits(in_v[r, pl.ds((2 * m + 1) * SC_LANES, SC_LANES)])
                    out_v[r, pl.ds(m * SC_LANES, SC_LANES)] = hi | lax.shift_right_logical(lo, 16)

            pltpu.sync_copy(out_v, out_hbm.at[pl.ds(base + r0, SC_TABLE_ROWS)])

    return run(v)


def _sc_peer_up(expert, xn2d, table, *, first, n_tokens):
    t, d = xn2d.shape
    per_worker = n_tokens // SC_WORKERS
    words = d // 2
    mesh = plsc.VectorSubcoreMesh(core_axis_name="c", subcore_axis_name="s")

    @functools.partial(
        pl.kernel, mesh=mesh,
        out_type=jax.ShapeDtypeStruct((n_tokens, PEER_SLOTS), F32),
        scratch_types=[
            pltpu.VMEM((PEER_SLOTS,), jnp.int32),
            pltpu.VMEM((d,), F32),
            pltpu.VMEM((2, SC_HALF, words), jnp.int32),
            pltpu.VMEM((PEER_SLOTS * SC_LANES,), F32),
            pltpu.VMEM((PEER_SLOTS,), F32),
            pltpu.SemaphoreType.DMA,
            pltpu.SemaphoreType.DMA,
        ],
        compiler_params=pltpu.CompilerParams(needs_layout_passes=False),
        name="sc_peer_up",
    )
    def run(expert_hbm, x_hbm, table_hbm, h_hbm, idx_v, x_v, rows_v, part_v, h_v, sem0, sem1):
        worker = lax.axis_index("s") * SC_CORES + lax.axis_index("c")
        local = worker * per_worker
        base = first + local
        lane = lax.iota(jnp.int32, SC_LANES)

        def partial_sums(h):
            @pl.loop(0, SC_HALF, step=SC_ROWS_PER_STEP)
            def _(j0):
                sums = [None] * SC_ROWS_PER_STEP
                for m in range(words // SC_LANES):
                    xa = x_v[pl.ds(2 * m * SC_LANES, SC_LANES)]
                    xb = x_v[pl.ds((2 * m + 1) * SC_LANES, SC_LANES)]
                    for r in range(SC_ROWS_PER_STEP):
                        word = rows_v[h, j0 + r, pl.ds(m * SC_LANES, SC_LANES)]
                        term = (xa * lax.bitcast_convert_type(word & jnp.int32(-65536), F32)
                                + xb * lax.bitcast_convert_type(word << 16, F32))
                        sums[r] = term if sums[r] is None else sums[r] + term
                for r in range(SC_ROWS_PER_STEP):
                    part_v[pl.ds((h * SC_HALF + j0 + r) * SC_LANES, SC_LANES)] = sums[r]

        @pl.loop(0, per_worker)
        def _(i):
            tok = base + i
            pltpu.sync_copy(expert_hbm.at[tok], idx_v)
            first_half = pltpu.async_copy(table_hbm.at[idx_v.at[pl.ds(0, SC_HALF)]],
                                          rows_v.at[0], sem0)
            second_half = pltpu.async_copy(table_hbm.at[idx_v.at[pl.ds(SC_HALF, SC_HALF)]],
                                           rows_v.at[1], sem1)
            pltpu.sync_copy(x_hbm.at[tok], x_v)
            first_half.wait()
            partial_sums(0)
            second_half.wait()
            partial_sums(1)
            for g in range(PEER_SLOTS // SC_LANES):
                total = None
                for k in range(SC_LANES):
                    column = plsc.load_gather(
                        part_v, [g * SC_LANES * SC_LANES + lane * SC_LANES + k])
                    total = column if total is None else total + column
                h_v[pl.ds(g * SC_LANES, SC_LANES)] = total
            pltpu.sync_copy(h_v, h_hbm.at[local + i])

    return run(expert, xn2d, table)


def _peer_act_kernel(h_ref, gate_ref, rep_ref, after_ref, o_ref):
    del after_ref
    h = h_ref[...]
    w = gate_ref[...] * (0.5 * h * (1.0 + lax.erf(h * (1.0 / math.sqrt(2.0)))))
    hi, lo = _split_bf16(w)
    o_ref[...] = (jnp.dot(hi, rep_ref[...], preferred_element_type=F32)
                  + jnp.dot(lo, rep_ref[...], preferred_element_type=F32))


def _peer_act(h, gate, rep, after, *, first, tm):
    n = h.shape[0]
    off = first // tm
    return pl.pallas_call(
        _peer_act_kernel,
        grid=(n // tm,),
        in_specs=[pl.BlockSpec((tm, PEER_SLOTS), lambda i: (i, 0)),
                  pl.BlockSpec((tm, PEER_SLOTS), lambda i: (i + off, 0)),
                  _resident(rep.shape),
                  pl.BlockSpec(memory_space=pl.ANY)],
        out_specs=pl.BlockSpec((tm, PEER_SLOTS * SC_LANES), lambda i: (i, 0)),
        out_shape=jax.ShapeDtypeStruct((n, PEER_SLOTS * SC_LANES), F32),
        compiler_params=_params(1),
        name="peer_act",
    )(h, gate, rep, after)


def _sc_peer_down(expert, w_lanes, x2d, table, *, first, n_tokens):
    t, d = x2d.shape
    per_worker = n_tokens // SC_WORKERS
    words = d // 2
    mesh = plsc.VectorSubcoreMesh(core_axis_name="c", subcore_axis_name="s")

    @functools.partial(
        pl.kernel, mesh=mesh,
        out_type=jax.ShapeDtypeStruct((t, d), F32),
        scratch_types=[
            pltpu.VMEM((PEER_SLOTS,), jnp.int32),
            pltpu.VMEM((PEER_SLOTS * SC_LANES,), F32),
            pltpu.VMEM((d,), F32),
            pltpu.VMEM((2, SC_HALF, words), jnp.int32),
            pltpu.SemaphoreType.DMA,
            pltpu.SemaphoreType.DMA,
        ],
        compiler_params=pltpu.CompilerParams(needs_layout_passes=False),
        name="sc_peer_down",
    )
    def run(expert_hbm, w_hbm, x_hbm, table_hbm, out_hbm, idx_v, w_v, out_v, rows_v, sem0, sem1):
        worker = lax.axis_index("s") * SC_CORES + lax.axis_index("c")
        local = worker * per_worker
        base = first + local

        def accumulate(h):
            @pl.loop(0, SC_HALF, step=SC_ROWS_PER_STEP)
            def _(j0):
                weights = [w_v[pl.ds((h * SC_HALF + j0 + r) * SC_LANES, SC_LANES)]
                           for r in range(SC_ROWS_PER_STEP)]
                for g in range(words // SC_LANES // SC_CHUNKS):
                    loaded = [[rows_v[h, j0 + r, pl.ds((g * SC_CHUNKS + q) * SC_LANES, SC_LANES)]
                               for q in range(SC_CHUNKS)] for r in range(SC_ROWS_PER_STEP)]
                    for q in range(SC_CHUNKS):
                        hi_sum = lo_sum = None
                        for r in range(SC_ROWS_PER_STEP):
                            word = loaded[r][q]
                            hi = weights[r] * lax.bitcast_convert_type(word & jnp.int32(-65536), F32)
                            lo = weights[r] * lax.bitcast_convert_type(word << 16, F32)
                            hi_sum = hi if hi_sum is None else hi_sum + hi
                            lo_sum = lo if lo_sum is None else lo_sum + lo
                        col = (g * SC_CHUNKS + q) * 2 * SC_LANES
                        plsc.addupdate(out_v.at[pl.ds(col, SC_LANES)], hi_sum)
                        plsc.addupdate(out_v.at[pl.ds(col + SC_LANES, SC_LANES)], lo_sum)

        @pl.loop(0, per_worker)
        def _(i):
            tok = base + i
            pltpu.sync_copy(expert_hbm.at[tok], idx_v)
            first_half = pltpu.async_copy(table_hbm.at[idx_v.at[pl.ds(0, SC_HALF)]],
                                          rows_v.at[0], sem0)
            second_half = pltpu.async_copy(table_hbm.at[idx_v.at[pl.ds(SC_HALF, SC_HALF)]],
                                           rows_v.at[1], sem1)
            pltpu.sync_copy(w_hbm.at[local + i], w_v)
            pltpu.sync_copy(x_hbm.at[tok], out_v)
            first_half.wait()
            accumulate(0)
            second_half.wait()
            accumulate(1)
            pltpu.sync_copy(out_v, out_hbm.at[tok])

    return run(expert, w_lanes, x2d, table)


def _peer_constants():
    col = jnp.arange(DOWN_K)
    src = ((col // SUBLANES) % 2) * PEER_SLOTS + col // PAIR_ROWS
    spread = (jnp.arange(2 * PEER_SLOTS)[:, None] == src[None, :]).astype(BF16)
    diag = (col[None, :] % SUBLANES == jnp.arange(SUBLANES)[:, None]).astype(F32)
    lane_repeat = jnp.repeat(jnp.eye(PEER_SLOTS, dtype=BF16), SC_LANES, axis=1)
    return spread, diag, lane_repeat


def _pair_slabs(v):
    e, d = v.shape
    return v.astype(BF16).reshape(e // 2, 2 * d // LANES, LANES)


def _peer(x2d, g, wq_bf16, keys_bf16, u_packed, u_words, v_slabs, v_words, consts, *, tm_route,
          tm_expert, n_sparsecore):
    t, d = x2d.shape
    spread, diag, lane_repeat = consts
    n_tc = t - n_sparsecore
    assert n_sparsecore > 0 and n_sparsecore % SC_WORKERS == 0 and n_tc % tm_expert == 0
    xn, pair, parity, gate = _peer_route(x2d, g, wq_bf16, keys_bf16, tm=tm_route)
    pair, parity, gate = pair.T, parity.T, gate.T
    xn3 = xn.reshape(t, d // LANES, LANES)
    x3 = x2d.reshape(t, d // LANES, LANES)
    expert = pair * 2 + parity
    h_sc = _sc_peer_up(expert, xn, u_words, first=n_tc, n_tokens=n_sparsecore)
    wexp = _peer_up(pair, parity, xn3, u_packed, gate, spread, tm=tm_expert, n_tokens=n_tc)
    w_lanes = _peer_act(h_sc, gate, lane_repeat, wexp, first=n_tc, tm=tm_expert)
    out = _sc_peer_down(expert, w_lanes, x2d, v_words, first=n_tc, n_tokens=n_sparsecore)
    out3 = _peer_down(pair, wexp, diag, x3, v_slabs, tm=tm_expert, n_tokens=n_tc)
    return lax.dynamic_update_slice(out, out3.reshape(n_tc, d), (0, 0))


def _final_norm_kernel(x_ref, g_ref, o_ref):
    o_ref[...] = _rms_scale(x_ref[...], g_ref[...])


def _final_norm(x2d, g, *, tm):
    t, d = x2d.shape
    return pl.pallas_call(
        _final_norm_kernel,
        grid=(t // tm,),
        in_specs=[pl.BlockSpec((tm, d), lambda i: (i, 0)), _resident((1, d))],
        out_specs=pl.BlockSpec((tm, d), lambda i: (i, 0)),
        out_shape=jax.ShapeDtypeStruct((t, d), F32),
        compiler_params=_params(1),
        name="final_norm",
    )(x2d, g.reshape(1, d))


def _tiles(seq):
    return dict(tm_proj=min(256, seq), tm_out=min(512, seq), tq=min(512, seq), tk=min(1024, seq),
                tm_route=LANES, tm_expert=LANES, tm_norm=min(512, seq))


SC_SHARE = 0.3515625


def _sparsecore_tokens(t, tm):
    unit = tm * SC_WORKERS // math.gcd(tm, SC_WORKERS)
    return int(t * SC_SHARE) // unit * unit


def _to_residue_major(a2d, batch, seq, dilation, c0, c1):
    cols = a2d[:, c0:c1].reshape(batch, seq // dilation, dilation, c1 - c0)
    return cols.transpose(0, 2, 1, 3)


def _from_residue_major(a4):
    b, d, length, c = a4.shape
    return a4.transpose(0, 2, 1, 3).reshape(b * d * length, c)


def kernel(x, mem, norm_mix, a_w_in, a_lambda, a_subln, b_w_in, norm_mem, w_mem_kv, w_out,
           norm_ffn, peer_wq, peer_keys, peer_u, peer_v, shared_norm, shared_w_kv, final_norm):
    batch, seq, d = x.shape
    mem_tokens = mem.shape[1]
    depth = norm_mix.shape[0]
    n_a = a_w_in.shape[0]
    t = batch * seq
    tiles = _tiles(seq)
    rope = _rope_tables(seq)
    scale = HEAD_DIM ** -0.5

    diff_qk = DIFF_HEADS * 2 * HEAD_DIM
    dil_qk = len(DIL_GROUPS) * DIL_GROUP_WIDTH
    peer_consts = _peer_constants()
    expand = jnp.repeat(jnp.eye(LANES, DIL_HEADS, dtype=BF16), HEAD_DIM, axis=1)

    x2d = x.reshape(t, d)
    mem2d = mem.reshape(batch * mem_tokens, d)
    shared = None

    for l in range(depth):
        memkv = _proj(mem2d, norm_mem[l], w_mem_kv[l].astype(BF16), tm=mem_tokens)
        w_o = w_out[l].astype(BF16)
        if l < n_a:
            col_scale = jnp.concatenate([jnp.full((diff_qk,), scale, F32),
                                         jnp.ones((a_w_in.shape[2] - diff_qk - MEM_Q_WIDTH,), F32),
                                         jnp.full((MEM_Q_WIDTH,), scale, F32)])
            w_in = (a_w_in[l] * col_scale).astype(BF16)
            proj = _proj(x2d, norm_mix[l], w_in, tm=tiles["tm_proj"], n_rope=2 * diff_qk,
                         rope=rope, seq=seq)
            lambda_init = 0.8 - 0.6 * math.exp(-0.3 * l)
            mix = _diff_attention(proj, a_lambda[l], a_subln[l], batch=batch, seq=seq,
                                  lambda_init=lambda_init, tq=tiles["tq"], tk=tiles["tk"])
            x2d = _outproj(x2d, proj, memkv, w_o, seq=seq, mem_tokens=mem_tokens,
                           tm=tiles["tm_out"], mix=mix)
        else:
            w_in = (b_w_in[l - n_a] * scale).astype(BF16)
            proj = _proj(x2d, norm_mix[l], w_in, tm=tiles["tm_proj"], n_rope=dil_qk,
                         rope=rope, seq=seq)
            outs, stats = [], []
            for gi, (window, dilation) in enumerate(DIL_GROUPS):
                c0 = gi * DIL_GROUP_WIDTH
                blk0 = c0 // LANES
                if dilation == 1:
                    q4, q_off = proj.reshape(batch, 1, seq, proj.shape[1]), blk0
                else:
                    q4, q_off = _to_residue_major(proj, batch, seq, dilation,
                                                  c0, c0 + DIL_GROUP_WIDTH), 0
                k4, k_off, v4, v_off = shared[gi]
                o4, st4 = _dilated_group(q4, k4, v4, q_off=q_off, k_off=k_off, v_off=v_off,
                                         n_steps=window // dilation)
                outs.append(_from_residue_major(o4))
                stats.append(_from_residue_major(st4))
            x2d = _outproj(x2d, proj, memkv, w_o, seq=seq, mem_tokens=mem_tokens,
                           tm=tiles["tm_out"], group_out=outs, group_stats=stats, expand=expand)

        x2d = _peer(x2d, norm_ffn[l], peer_wq[l].astype(BF16),
                    peer_keys[l].reshape(2 * PEER_HEADS, PEER_N_KEYS, PEER_HALF).astype(BF16),
                    _pack_expert_table(peer_u[l]), _sc_word_table(peer_u[l]),
                    _pair_slabs(peer_v[l]), _sc_word_table(peer_v[l]), peer_consts,
                    tm_route=tiles["tm_route"], tm_expert=tiles["tm_expert"],
                    n_sparsecore=_sparsecore_tokens(t, tiles["tm_expert"]))

        if l == n_a - 1:
            kv = _proj(x2d, shared_norm, shared_w_kv.astype(BF16), tm=tiles["tm_proj"],
                       n_rope=dil_qk, rope=rope, seq=seq)
            shared = []
            for gi, (window, dilation) in enumerate(DIL_GROUPS):
                c0 = gi * DIL_GROUP_WIDTH
                if dilation == 1:
                    kv4 = kv.reshape(batch, 1, seq, kv.shape[1])
                    shared.append((kv4, c0 // LANES, kv4, (dil_qk + c0) // LANES))
                else:
                    k4 = _to_residue_major(kv, batch, seq, dilation, c0, c0 + DIL_GROUP_WIDTH)
                    v4 = _to_residue_major(kv, batch, seq, dilation, dil_qk + c0,
                                           dil_qk + c0 + DIL_GROUP_WIDTH)
                    shared.append((k4, 0, v4, 0))

    return _final_norm(x2d, final_norm, tm=tiles["tm_norm"]).reshape(batch, seq, d)
```

```python
import functools
import math

import jax
import jax.numpy as jnp
from jax import lax
from jax.experimental import pallas as pl
from jax.experimental.pallas import tpu as pltpu
from jax.experimental.pallas import tpu_sc as plsc

F32 = jnp.float32
BF16 = jnp.bfloat16

HEAD_DIM = 64
ROPE_DIMS = HEAD_DIM // 4
ROPE_HALF = ROPE_DIMS // 2
ROPE_THETA = 500000.0
NORM_EPS = 1e-5
DIFF_HEADS = 6
MEM_HEADS = 4
MEM_Q_WIDTH = MEM_HEADS * HEAD_DIM
DIL_GROUPS = ((128, 1), (512, 4), (2048, 16))
DIL_HEADS = 12
DIL_BLOCK = 128
DIL_GROUP_WIDTH = DIL_HEADS * HEAD_DIM
PEER_HEADS = 8
PEER_N_KEYS = 128
PEER_TOPK = 16
PEER_HALF = 128
PEER_SLOTS = PEER_HEADS * PEER_TOPK

LANES = 128
SUBLANES = 8
VMEM_LIMIT_BYTES = 56 * 1024 * 1024

_NT = (((1,), (1,)), ((), ()))


def _params(n_axes):
    return pltpu.CompilerParams(dimension_semantics=("arbitrary",) * n_axes,
                                vmem_limit_bytes=VMEM_LIMIT_BYTES)


def _resident(shape):
    zeros = (0,) * len(shape)
    return pl.BlockSpec(shape, lambda *_: zeros, pipeline_mode=pl.Buffered(1))


def _rms_scale(x, g):
    ms = jnp.mean(x * x, axis=-1, keepdims=True)
    return x * lax.rsqrt(ms + NORM_EPS) * g


def _rope_tables(seq):
    inv = ROPE_THETA ** (-jnp.arange(0, ROPE_DIMS, 2, dtype=F32) / ROPE_DIMS)
    ang = jnp.arange(seq, dtype=F32)[:, None] * inv[None, :]
    cos, sin = jnp.cos(ang), jnp.sin(ang)
    ones = jnp.ones((seq, HEAD_DIM - ROPE_DIMS), F32)
    zeros = jnp.zeros((seq, HEAD_DIM - ROPE_DIMS), F32)
    zh = jnp.zeros((seq, ROPE_HALF), F32)
    c = jnp.concatenate([cos, cos, ones], axis=1)
    sa = jnp.concatenate([-sin, zh, zeros], axis=1)
    sb = jnp.concatenate([zh, sin, zeros], axis=1)
    rep = LANES // HEAD_DIM
    return tuple(jnp.tile(t, (1, rep)) for t in (c, sa, sb))


def _proj_kernel(*refs, n_rope, chunk):
    if n_rope:
        x_ref, g_ref, w_ref, c_ref, sa_ref, sb_ref, o_ref = refs
    else:
        x_ref, g_ref, w_ref, o_ref = refs
    y = _rms_scale(x_ref[...], g_ref[...]).astype(BF16)
    n = o_ref.shape[1]
    for c0 in range(0, n, chunk):
        acc = jnp.dot(y, w_ref[:, c0:c0 + chunk], preferred_element_type=F32)
        if c0 < n_rope:
            for k0 in range(0, chunk, LANES):
                a = acc[:, k0:k0 + LANES]
                a = (a * c_ref[...]
                     + pltpu.roll(a, LANES - ROPE_HALF, 1) * sa_ref[...]
                     + pltpu.roll(a, ROPE_HALF, 1) * sb_ref[...])
                o_ref[:, c0 + k0:c0 + k0 + LANES] = a.astype(o_ref.dtype)
        else:
            o_ref[:, c0:c0 + chunk] = acc.astype(o_ref.dtype)


def _proj(x2d, g, w_bf16, *, tm, n_rope=0, rope=None, seq=None):
    t, d = x2d.shape
    n = w_bf16.shape[1]
    chunk = 256
    assert t % tm == 0 and n % chunk == 0 and n_rope % chunk == 0
    in_specs = [pl.BlockSpec((tm, d), lambda i: (i, 0)),
                _resident((1, d)),
                _resident((d, n))]
    args = [x2d, g.reshape(1, d), w_bf16]
    if n_rope:
        nblk = seq // tm
        spec = pl.BlockSpec((tm, LANES), lambda i: (i % nblk, 0))
        in_specs += [spec, spec, spec]
        args += list(rope)
    return pl.pallas_call(
        functools.partial(_proj_kernel, n_rope=n_rope, chunk=chunk),
        grid=(t // tm,),
        in_specs=in_specs,
        out_specs=pl.BlockSpec((tm, n), lambda i: (i, 0)),
        out_shape=jax.ShapeDtypeStruct((t, n), BF16),
        compiler_params=_params(1),
        name="norm_proj",
    )(*args)


def _online_softmax_step(q, k, v, carry, mask):
    m, l, acc = carry
    s = lax.dot_general(q, k, _NT, preferred_element_type=F32)
    if mask is not None:
        s = jnp.where(mask, s, -jnp.inf)
    m_new = jnp.maximum(m, jnp.max(s, axis=-1, keepdims=True))
    alpha = jnp.exp(m - m_new)
    p = jnp.exp(s - m_new)
    l = alpha * l + jnp.sum(p, axis=-1, keepdims=True)
    acc = alpha * acc + jnp.dot(p.astype(BF16), v, preferred_element_type=F32)
    return m_new, l, acc


def _diff_attn_kernel(q_ref, k_ref, v_ref, lp_ref, g_ref, o_ref, *, tq, tk, lambda_init):
    qi = pl.program_id(2)
    q = q_ref[...]
    lane = lax.broadcasted_iota(jnp.int32, q.shape, 1)
    zero = jnp.zeros_like(q)
    qa = jnp.where(lane < HEAD_DIM, q, zero)
    qb = jnp.where(lane >= HEAD_DIM, q, zero)

    def init():
        return (jnp.full((tq, 1), -jnp.inf, F32), jnp.zeros((tq, 1), F32),
                jnp.zeros((tq, 2 * HEAD_DIM), F32))

    def body(j, carry):
        c1, c2 = carry
        k = k_ref[pl.ds(pl.multiple_of(j * tk, tk), tk), :]
        v = v_ref[pl.ds(pl.multiple_of(j * tk, tk), tk), :]
        return (_online_softmax_step(qa, k, v, c1, None),
                _online_softmax_step(qb, k, v, c2, None))

    n_full = (qi * tq) // tk
    c1, c2 = lax.fori_loop(0, n_full, body, (init(), init()))
    row = qi * tq + lax.broadcasted_iota(jnp.int32, (tq, tk), 0)
    col = n_full * tk + lax.broadcasted_iota(jnp.int32, (tq, tk), 1)
    causal = col <= row
    start = pl.multiple_of(n_full * tk, tk)
    k = k_ref[pl.ds(start, tk), :]
    v = v_ref[pl.ds(start, tk), :]
    m1, l1, a1 = _online_softmax_step(qa, k, v, c1, causal)
    m2, l2, a2 = _online_softmax_step(qb, k, v, c2, causal)

    lp = lp_ref[...]
    lam = (jnp.exp(jnp.sum(lp[0:1] * lp[1:2], axis=-1, keepdims=True))
           - jnp.exp(jnp.sum(lp[2:3] * lp[3:4], axis=-1, keepdims=True)) + lambda_init)
    o = a1 / l1 - lam * (a2 / l2)
    o = _rms_scale(o, g_ref[...]) * (1.0 - lambda_init)
    o_ref[...] = o.astype(o_ref.dtype)


def _diff_attention(proj, lp, subln, *, batch, seq, lambda_init, tq, tk):
    assert tk % tq == 0 and seq % tk == 0
    nq = seq // tq
    width = 2 * HEAD_DIM
    return pl.pallas_call(
        functools.partial(_diff_attn_kernel, tq=tq, tk=tk, lambda_init=lambda_init),
        grid=(batch, DIFF_HEADS, nq),
        in_specs=[
            pl.BlockSpec((tq, width), lambda b, h, i: (b * nq + i, h)),
            pl.BlockSpec((seq, width), lambda b, h, i: (b, DIFF_HEADS + h)),
            pl.BlockSpec((seq, width), lambda b, h, i: (b, 2 * DIFF_HEADS + h)),
            _resident((4, HEAD_DIM)),
            _resident((1, width)),
        ],
        out_specs=pl.BlockSpec((tq, width), lambda b, h, i: (b * nq + i, h)),
        out_shape=jax.ShapeDtypeStruct((batch * seq, DIFF_HEADS * width), BF16),
        compiler_params=_params(3),
        name="diff_attention",
    )(proj, proj, proj, lp, subln.reshape(1, width))


def _dilated_kernel(q_ref, kc_ref, kp_ref, vc_ref, vp_ref, o_ref, st_ref, *, tq, n_steps):
    n = pl.program_id(2)
    hp = pl.program_id(3)
    blk = DIL_BLOCK

    @pl.when(hp == 0)
    def _():
        st_ref[...] = jnp.zeros_like(st_ref)

    qrow = lax.broadcasted_iota(jnp.int32, (blk, 2 * blk), 0)
    kcol = lax.broadcasted_iota(jnp.int32, (blk, 2 * blk), 1)
    dist = qrow + blk - kcol
    band = (dist >= 0) & (dist <= n_steps)
    lane_q = lax.broadcasted_iota(jnp.int32, (blk, LANES), 1)
    lane_s = lax.broadcasted_iota(jnp.int32, (blk, LANES), 1)

    for r, i in [(r, i) for r in range(q_ref.shape[0]) for i in range(tq // blk)]:
        rows = slice(i * blk, (i + 1) * blk)
        q = q_ref[r, rows, :]
        if i == 0:
            k_prev, v_prev = kp_ref[r], vp_ref[r]
            valid = band & ((n > 0) | (kcol >= blk))
        else:
            prev = slice((i - 1) * blk, i * blk)
            k_prev, v_prev = kc_ref[r, prev, :], vc_ref[r, prev, :]
            valid = band
        keys = jnp.concatenate([k_prev, kc_ref[r, rows, :]], axis=0)
        vals = jnp.concatenate([v_prev, vc_ref[r, rows, :]], axis=0)
        out = jnp.zeros((blk, LANES), F32)
        stats = st_ref[r, rows, :]
        for e in range(LANES // HEAD_DIM):
            head_lanes = (lane_q // HEAD_DIM) == e
            qe = jnp.where(head_lanes, q, jnp.zeros_like(q))
            s = lax.dot_general(qe, keys, _NT, preferred_element_type=F32)
            s = jnp.where(valid, s, -jnp.inf)
            m = jnp.max(s, axis=-1, keepdims=True)
            p = jnp.exp(s - m)
            den = jnp.sum(p, axis=-1, keepdims=True)
            oe = jnp.dot((p / den).astype(BF16), vals, preferred_element_type=F32)
            out = jnp.where(head_lanes, oe, out)
            lse = m + jnp.log(den)
            stats = jnp.where(lane_s == hp * (LANES // HEAD_DIM) + e, lse, stats)
        o_ref[r, rows, :] = out.astype(o_ref.dtype)
        st_ref[r, rows, :] = stats


def _dilated_group(q4, k4, v4, *, q_off, k_off, v_off, n_steps):
    b, d, length, _ = q4.shape
    rows_per_step = 1024
    tq = min(rows_per_step, length)
    nres = min(d, rows_per_step // tq)
    assert length % tq == 0 and tq % DIL_BLOCK == 0 and d % nres == 0
    sub = tq // DIL_BLOCK
    pairs = DIL_GROUP_WIDTH // LANES

    def cur(off):
        return pl.BlockSpec((None, nres, tq, LANES), lambda bi, r, n, h: (bi, r, n, off + h))

    def prev(off):
        return pl.BlockSpec((None, nres, DIL_BLOCK, LANES),
                            lambda bi, r, n, h: (bi, r, jnp.maximum(n * sub - 1, 0), off + h))

    return pl.pallas_call(
        functools.partial(_dilated_kernel, tq=tq, n_steps=n_steps),
        grid=(b, d // nres, length // tq, pairs),
        in_specs=[cur(q_off), cur(k_off), prev(k_off), cur(v_off), prev(v_off)],
        out_specs=[pl.BlockSpec((None, nres, tq, LANES), lambda bi, r, n, h: (bi, r, n, h)),
                   pl.BlockSpec((None, nres, tq, LANES), lambda bi, r, n, h: (bi, r, n, 0))],
        out_shape=[jax.ShapeDtypeStruct((b, d, length, DIL_GROUP_WIDTH), BF16),
                   jax.ShapeDtypeStruct((b, d, length, LANES), F32)],
        compiler_params=_params(4),
        name="dilated_attention",
    )(q4, k4, k4, v4, v4)


def _memory_attention(q, k, v):
    lane = lax.broadcasted_iota(jnp.int32, q.shape, 1)
    out = jnp.zeros(q.shape, F32)
    for h in range(MEM_HEADS):
        head_lanes = (lane // HEAD_DIM) == h
        qh = jnp.where(head_lanes, q, jnp.zeros_like(q))
        s = lax.dot_general(qh, k, _NT, preferred_element_type=F32)
        m = jnp.max(s, axis=-1, keepdims=True)
        p = jnp.exp(s - m)
        den = jnp.sum(p, axis=-1, keepdims=True)
        oh = jnp.dot((p / den).astype(BF16), v, preferred_element_type=F32)
        out = jnp.where(head_lanes, oh, out)
    return out


def _split_bf16(x):
    hi = x.astype(BF16)
    lo = (x - hi.astype(F32)).astype(BF16)
    return hi, lo


def _outproj_kernel(*refs, n_groups):
    if n_groups:
        x_ref, qm_ref, mk_ref, mv_ref, w_ref = refs[:5]
        o_refs = refs[5:5 + n_groups]
        st_refs = refs[5 + n_groups:5 + 2 * n_groups]
        ex_ref, out_ref = refs[5 + 2 * n_groups:]
        lses = [r[...] for r in st_refs]
        top = functools.reduce(jnp.maximum, lses)
        es = [jnp.exp(l - top) for l in lses]
        den = functools.reduce(lambda a, b: a + b, es)
        mix = None
        for e, o_ref in zip(es, o_refs):
            hi, lo = _split_bf16(e / den)
            wide = (jnp.dot(hi, ex_ref[...], preferred_element_type=F32)
                    + jnp.dot(lo, ex_ref[...], preferred_element_type=F32))
            term = wide * o_ref[...].astype(F32)
            mix = term if mix is None else mix + term
        mix = mix.astype(BF16)
    else:
        x_ref, qm_ref, mk_ref, mv_ref, w_ref, mix_ref, out_ref = refs
        mix = mix_ref[...]
    mo = _memory_attention(qm_ref[...], mk_ref[...], mv_ref[...]).astype(BF16)
    k_mix = mix.shape[1]
    acc = jnp.dot(mix, w_ref[:k_mix, :], preferred_element_type=F32)
    acc += jnp.dot(mo, w_ref[k_mix:, :], preferred_element_type=F32)
    out_ref[...] = x_ref[...] + acc


def _outproj(x2d, proj, memkv, w_bf16, *, seq, mem_tokens, tm, mix=None, group_out=None,
             group_stats=None, expand=None):
    t, d = x2d.shape
    per_batch = seq // tm
    qm_block = (proj.shape[1] - MEM_Q_WIDTH) // MEM_Q_WIDTH
    row = lambda i: (i, 0)
    in_specs = [pl.BlockSpec((tm, d), row),
                pl.BlockSpec((tm, MEM_Q_WIDTH), lambda i: (i, qm_block)),
                pl.BlockSpec((mem_tokens, MEM_Q_WIDTH), lambda i: (i // per_batch, 0)),
                pl.BlockSpec((mem_tokens, MEM_Q_WIDTH), lambda i: (i // per_batch, 1)),
                _resident(w_bf16.shape)]
    args = [x2d, proj, memkv, memkv, w_bf16]
    if mix is not None:
        n_groups = 0
        in_specs.append(pl.BlockSpec((tm, mix.shape[1]), row))
        args.append(mix)
    else:
        n_groups = len(group_out)
        in_specs += [pl.BlockSpec((tm, DIL_GROUP_WIDTH), row)] * n_groups
        in_specs += [pl.BlockSpec((tm, LANES), row)] * n_groups
        in_specs.append(_resident(expand.shape))
        args += list(group_out) + list(group_stats) + [expand]
    return pl.pallas_call(
        functools.partial(_outproj_kernel, n_groups=n_groups),
        grid=(t // tm,),
        in_specs=in_specs,
        out_specs=pl.BlockSpec((tm, d), row),
        out_shape=jax.ShapeDtypeStruct((t, d), F32),
        compiler_params=_params(1),
        name="mix_outproj",
    )(*args)


def _topk_rows(s, k, payload=None):
    n = s.shape[0]
    row = lax.broadcasted_iota(jnp.int32, s.shape, 0)
    vals, picked = [], []
    for _ in range(k):
        m = jnp.max(s, axis=0, keepdims=True)
        i = jnp.min(jnp.where(s == m, row, n), axis=0, keepdims=True)
        hit = row == i
        vals.append(m)
        if payload is None:
            picked.append(i)
        else:
            picked.append(jnp.sum(jnp.where(hit, payload, 0), axis=0, keepdims=True))
        s = jnp.where(hit, -jnp.inf, s)
    return jnp.concatenate(vals, axis=0), jnp.concatenate(picked, axis=0)


def _peer_route_kernel(x_ref, g_ref, wq_ref, keys_ref, xn_ref, pair_ref, parity_ref, gate_ref):
    xn = _rms_scale(x_ref[...], g_ref[...])
    xn_ref[...] = xn
    q = jnp.dot(xn.astype(BF16), wq_ref[...], preferred_element_type=F32).astype(BF16)
    kk = PEER_TOPK
    for h in range(PEER_HEADS):
        tops = []
        for half in range(2):
            c0 = (2 * h + half) * PEER_HALF
            s = lax.dot_general(keys_ref[2 * h + half], q[:, c0:c0 + PEER_HALF], _NT,
                                preferred_element_type=F32)
            tops.append(_topk_rows(s, kk))
        (s1, i1), (s2, i2) = tops
        half = kk // 2
        sub = lax.broadcasted_iota(jnp.int32, (half, s1.shape[1]), 0)
        blocks = [s1[0:1] + s2]
        experts = [i1[0:1] * PEER_N_KEYS + i2]
        for a in range(1, half):
            blocks.append(jnp.where(sub < kk // (a + 1), s1[a:a + 1] + s2[:half], -jnp.inf))
            experts.append(i1[a:a + 1] * PEER_N_KEYS + i2[:half])
        blocks.append(s1[half:] + s2[0:1])
        experts.append(i1[half:] * PEER_N_KEYS + i2[0:1])
        cand = jnp.concatenate(blocks, axis=0)
        cand_expert = jnp.concatenate(experts, axis=0)
        top_s, expert = _topk_rows(cand, kk, payload=cand_expert)
        e = jnp.exp(top_s - top_s[0:1])
        gate = e / jnp.sum(e, axis=0, keepdims=True)
        rows = slice(h * kk, (h + 1) * kk)
        pair_ref[rows, :] = expert >> 1
        parity_ref[rows, :] = expert & 1
        gate_ref[rows, :] = gate


def _peer_route(x2d, g, wq_bf16, keys_bf16, *, tm):
    t, d = x2d.shape
    col = lambda i: (0, i)
    slot_shape = jax.ShapeDtypeStruct((PEER_SLOTS, t), jnp.int32)
    return pl.pallas_call(
        _peer_route_kernel,
        grid=(t // tm,),
        in_specs=[pl.BlockSpec((tm, d), lambda i: (i, 0)),
                  _resident((1, d)),
                  _resident(wq_bf16.shape),
                  _resident(keys_bf16.shape)],
        out_specs=[pl.BlockSpec((tm, d), lambda i: (i, 0)),
                   pl.BlockSpec((PEER_SLOTS, tm), col),
                   pl.BlockSpec((PEER_SLOTS, tm), col),
                   pl.BlockSpec((PEER_SLOTS, tm), col)],
        out_shape=[jax.ShapeDtypeStruct((t, d), F32), slot_shape, slot_shape,
                   jax.ShapeDtypeStruct((PEER_SLOTS, t), F32)],
        compiler_params=_params(1),
        name="peer_route",
    )(x2d, g.reshape(1, d), wq_bf16, keys_bf16)


_HI_MASK = 0xFFFF0000


def _pack_expert_table(u):
    e, d = u.shape
    bits = lax.bitcast_convert_type(u.astype(BF16), jnp.uint16).astype(jnp.uint32)
    bits = bits.reshape(e // 2, 2, d)
    packed = (bits[:, 0] << 16) | bits[:, 1]
    return packed.reshape(e // 2, d // LANES, LANES)


_BIT_REVERSED = (0, 4, 2, 6, 1, 5, 3, 7)


def _bf16_pair_words(x):
    bits = pltpu.bitcast(x, jnp.uint32)
    top = (bits + jnp.uint32(0x7FFF) + ((bits >> 16) & jnp.uint32(1))) >> 16
    return (top << 16) | top


def _packed_add(a, b):
    return pltpu.bitcast(pltpu.bitcast(a, BF16) + pltpu.bitcast(b, BF16), jnp.uint32)


def _sublane_sums(words, sub):
    level = [words[i] for i in _BIT_REVERSED]
    for k in (4, 2, 1):
        low = (sub & k) == 0
        merged = []
        for a, b in zip(level[0::2], level[1::2]):
            if k == 4:
                merged.append(_packed_add(jnp.where(low, a, b),
                                          pltpu.roll(jnp.where(low, b, a), k, 0)))
            else:
                merged.append(jnp.where(low,
                                        _packed_add(a, pltpu.roll(a, SUBLANES - k, 0)),
                                        _packed_add(b, pltpu.roll(b, k, 0))))
        level = merged
    return level[0]


def _peer_up_kernel(pair_ref, x_ref, tab_ref, gate_ref, par_ref, spread_ref, wexp_ref, w2_ref):
    tm = x_ref.shape[0]
    sub = lax.broadcasted_iota(jnp.int32, (SUBLANES, LANES), 0)
    hi_mask = jnp.uint32(_HI_MASK)

    def token_sums(t, m):
        xx = pltpu.bitcast(_bf16_pair_words(x_ref[t]), BF16)
        even, odd = [], []
        for g in range(PEER_SLOTS // SUBLANES):
            words = []
            for i in range(SUBLANES):
                slab = pltpu.bitcast(tab_ref[pair_ref[t, g * SUBLANES + i]], BF16)
                words.append(pltpu.bitcast(slab * xx, jnp.uint32))
            q = _sublane_sums(words, sub)
            even.append(pltpu.bitcast(q & hi_mask, F32))
            odd.append(pltpu.bitcast(q << 16, F32))
        lane_parts = jnp.concatenate(even + odd, axis=0).astype(BF16)
        onehot = (sub == m).astype(BF16)
        return lax.dot_general(onehot, lane_parts, _NT, preferred_element_type=F32)

    def group(gi, _):
        base = pl.multiple_of(gi * SUBLANES, SUBLANES)
        h2 = token_sums(base, 0)
        for m in range(1, SUBLANES):
            h2 = h2 + token_sums(base + m, m)
        odd_slot = par_ref[pl.ds(base, SUBLANES), :] != 0
        h = jnp.where(odd_slot, h2[:, PEER_SLOTS:], h2[:, :PEER_SLOTS])
        act = 0.5 * h * (1.0 + lax.erf(h * (1.0 / math.sqrt(2.0))))
        w = gate_ref[pl.ds(base, SUBLANES), :] * act
        zero = jnp.zeros_like(w)
        w2_ref[pl.ds(base, SUBLANES), 0:PEER_SLOTS] = jnp.where(odd_slot, zero, w)
        w2_ref[pl.ds(base, SUBLANES), PEER_SLOTS:2 * PEER_SLOTS] = jnp.where(odd_slot, w, zero)
        return 0

    lax.fori_loop(0, tm // SUBLANES, group, 0)
    hi, lo = _split_bf16(w2_ref[...])
    wexp_ref[...] = (jnp.dot(hi, spread_ref[...], preferred_element_type=F32)
                     + jnp.dot(lo, spread_ref[...], preferred_element_type=F32))


PAIR_ROWS = 2 * SUBLANES
DOWN_K = PEER_SLOTS * PAIR_ROWS
DOWN_PAIRS_PER_STEP = 4


def _peer_down_kernel(pair_ref, wexp_ref, diag_ref, x_ref, tab_ref, o_ref):
    tm = x_ref.shape[0]

    def left_rows(t):
        row = wexp_ref[pl.ds(t, 1), :]
        hi, lo = _split_bf16(row * diag_ref[...])
        return [hi, lo]

    def token_pair(ta):
        tb = ta + 1
        rhs = jnp.concatenate(
            [jnp.concatenate([tab_ref[pair_ref[ta, j]], tab_ref[pair_ref[tb, j]]], axis=1)
             for j in range(PEER_SLOTS)], axis=0)
        lhs = jnp.concatenate(left_rows(ta) + left_rows(tb), axis=0)
        out = jnp.dot(lhs, rhs, preferred_element_type=F32)
        s = SUBLANES
        o_ref[ta] = x_ref[ta] + (out[0:s, :LANES] + out[s:2 * s, :LANES])
        o_ref[tb] = x_ref[tb] + (out[2 * s:3 * s, LANES:] + out[3 * s:4 * s, LANES:])

    def step(i, _):
        for k in range(DOWN_PAIRS_PER_STEP):
            token_pair(2 * (DOWN_PAIRS_PER_STEP * i + k))
        return 0

    lax.fori_loop(0, tm // (2 * DOWN_PAIRS_PER_STEP), step, 0)


def _smem_rows(tm):
    return pl.BlockSpec((tm, PEER_SLOTS), lambda i: (i, 0), memory_space=pltpu.SMEM)


def _peer_up(pair, parity, xn3, table, gate, spread, *, tm, n_tokens):
    rows = pl.BlockSpec((tm, PEER_SLOTS), lambda i: (i, 0))
    return pl.pallas_call(
        _peer_up_kernel,
        grid=(n_tokens // tm,),
        in_specs=[_smem_rows(tm),
                  pl.BlockSpec((tm, SUBLANES, LANES), lambda i: (i, 0, 0)),
                  _resident(table.shape), rows, rows, _resident(spread.shape)],
        out_specs=pl.BlockSpec((tm, DOWN_K), lambda i: (i, 0)),
        out_shape=jax.ShapeDtypeStruct((n_tokens, DOWN_K), F32),
        scratch_shapes=[pltpu.VMEM((tm, 2 * PEER_SLOTS), F32)],
        compiler_params=_params(1),
        name="peer_up",
    )(pair, xn3, table, gate, parity, spread)


def _peer_down(pair, wexp, diag, x3, table, *, tm, n_tokens):
    tile = pl.BlockSpec((tm, SUBLANES, LANES), lambda i: (i, 0, 0))
    return pl.pallas_call(
        _peer_down_kernel,
        grid=(n_tokens // tm,),
        in_specs=[_smem_rows(tm), pl.BlockSpec((tm, DOWN_K), lambda i: (i, 0)),
                  _resident(diag.shape), tile, _resident(table.shape)],
        out_specs=tile,
        out_shape=jax.ShapeDtypeStruct((n_tokens,) + x3.shape[1:], F32),
        compiler_params=_params(1),
        name="peer_down",
    )(pair, wexp, diag, x3, table)


SC_CORES = 2
SC_SUBCORES = 16
SC_WORKERS = SC_CORES * SC_SUBCORES
SC_LANES = 16
SC_HALF = PEER_SLOTS // 2
SC_ROWS_PER_STEP = 4
SC_CHUNKS = 8


SC_TABLE_ROWS = 64


def _sc_word_table(v):
    e, d = v.shape
    per_worker = e // SC_WORKERS
    words = d // 2
    mesh = plsc.VectorSubcoreMesh(core_axis_name="c", subcore_axis_name="s")

    @functools.partial(
        pl.kernel, mesh=mesh,
        out_type=jax.ShapeDtypeStruct((e, words), jnp.int32),
        scratch_types=[pltpu.VMEM((SC_TABLE_ROWS, d), F32),
                       pltpu.VMEM((SC_TABLE_ROWS, words), jnp.int32)],
        compiler_params=pltpu.CompilerParams(needs_layout_passes=False),
        name="sc_word_table",
    )
    def run(v_hbm, out_hbm, in_v, out_v):
        worker = lax.axis_index("s") * SC_CORES + lax.axis_index("c")
        base = worker * per_worker

        def bf16_bits(x):
            bits = lax.bitcast_convert_type(x, jnp.int32)
            lsb = lax.shift_right_logical(bits, 16) & 1
            rounded = (bits + 0x7FFF + lsb) & jnp.int32(-65536)
            subnormal = (bits & jnp.int32(0x7F800000)) == 0
            return jnp.where(subnormal, bits & jnp.int32(-2147483648), rounded)

        @pl.loop(0, per_worker, step=SC_TABLE_ROWS)
        def _(r0):
            pltpu.sync_copy(v_hbm.at[pl.ds(base + r0, SC_TABLE_ROWS)], in_v)

            @pl.loop(0, SC_TABLE_ROWS)
            def _(r):
                for m in range(words // SC_LANES):
                    hi = bf16_bits(in_v[r, pl.ds(2 * m * SC_LANES, SC_LANES)])
                    lo = bf16_bits(in_v[r, pl.ds((2 * m + 1) * SC_LANES, SC_LANES)])
                    out_v[r, pl.ds(m * SC_LANES, SC_LANES)] = hi | lax.shift_right_logical(lo, 16)

            pltpu.sync_copy(out_v, out_hbm.at[pl.ds(base + r0, SC_TABLE_ROWS)])

    return run(v)


def _sc_peer_up(expert, xn2d, table, *, first, n_tokens):
    t, d = xn2d.shape
    per_worker = n_tokens // SC_WORKERS
    words = d // 2
    mesh = plsc.VectorSubcoreMesh(core_axis_name="c", subcore_axis_name="s")

    @functools.partial(
        pl.kernel, mesh=mesh,
        out_type=jax.ShapeDtypeStruct((n_tokens, PEER_SLOTS), F32),
        scratch_types=[
            pltpu.VMEM((PEER_SLOTS,), jnp.int32),
            pltpu.VMEM((d,), F32),
            pltpu.VMEM((2, SC_HALF, words), jnp.int32),
            pltpu.VMEM((PEER_SLOTS * SC_LANES,), F32),
            pltpu.VMEM((PEER_SLOTS,), F32),
            pltpu.SemaphoreType.DMA,
            pltpu.SemaphoreType.DMA,
        ],
        compiler_params=pltpu.CompilerParams(needs_layout_passes=False),
        name="sc_peer_up",
    )
    def run(expert_hbm, x_hbm, table_hbm, h_hbm, idx_v, x_v, rows_v, part_v, h_v, sem0, sem1):
        worker = lax.axis_index("s") * SC_CORES + lax.axis_index("c")
        local = worker * per_worker
        base = first + local
        lane = lax.iota(jnp.int32, SC_LANES)

        def partial_sums(h):
            @pl.loop(0, SC_HALF, step=SC_ROWS_PER_STEP)
            def _(j0):
                sums = [None] * SC_ROWS_PER_STEP
                for m in range(words // SC_LANES):
                    xa = x_v[pl.ds(2 * m * SC_LANES, SC_LANES)]
                    xb = x_v[pl.ds((2 * m + 1) * SC_LANES, SC_LANES)]
                    for r in range(SC_ROWS_PER_STEP):
                        word = rows_v[h, j0 + r, pl.ds(m * SC_LANES, SC_LANES)]
                        term = (xa * lax.bitcast_convert_type(word & jnp.int32(-65536), F32)
                                + xb * lax.bitcast_convert_type(word << 16, F32))
                        sums[r] = term if sums[r] is None else sums[r] + term
                for r in range(SC_ROWS_PER_STEP):
                    part_v[pl.ds((h * SC_HALF + j0 + r) * SC_LANES, SC_LANES)] = sums[r]

        @pl.loop(0, per_worker)
        def _(i):
            tok = base + i
            pltpu.sync_copy(expert_hbm.at[tok], idx_v)
            first_half = pltpu.async_copy(table_hbm.at[idx_v.at[pl.ds(0, SC_HALF)]],
                                          rows_v.at[0], sem0)
            second_half = pltpu.async_copy(table_hbm.at[idx_v.at[pl.ds(SC_HALF, SC_HALF)]],
                                           rows_v.at[1], sem1)
            pltpu.sync_copy(x_hbm.at[tok], x_v)
            first_half.wait()
            partial_sums(0)
            second_half.wait()
            partial_sums(1)
            for g in range(PEER_SLOTS // SC_LANES):
                total = None
                for k in range(SC_LANES):
                    column = plsc.load_gather(
                        part_v, [g * SC_LANES * SC_LANES + lane * SC_LANES + k])
                    total = column if total is None else total + column
                h_v[pl.ds(g * SC_LANES, SC_LANES)] = total
            pltpu.sync_copy(h_v, h_hbm.at[local + i])

    return run(expert, xn2d, table)


def _peer_act_kernel(h_ref, gate_ref, rep_ref, after_ref, o_ref):
    del after_ref
    h = h_ref[...]
    w = gate_ref[...] * (0.5 * h * (1.0 + lax.erf(h * (1.0 / math.sqrt(2.0)))))
    hi, lo = _split_bf16(w)
    o_ref[...] = (jnp.dot(hi, rep_ref[...], preferred_element_type=F32)
                  + jnp.dot(lo, rep_ref[...], preferred_element_type=F32))


def _peer_act(h, gate, rep, after, *, first, tm):
    n = h.shape[0]
    off = first // tm
    return pl.pallas_call(
        _peer_act_kernel,
        grid=(n // tm,),
        in_specs=[pl.BlockSpec((tm, PEER_SLOTS), lambda i: (i, 0)),
                  pl.BlockSpec((tm, PEER_SLOTS), lambda i: (i + off, 0)),
                  _resident(rep.shape),
                  pl.BlockSpec(memory_space=pl.ANY)],
        out_specs=pl.BlockSpec((tm, PEER_SLOTS * SC_LANES), lambda i: (i, 0)),
        out_shape=jax.ShapeDtypeStruct((n, PEER_SLOTS * SC_LANES), F32),
        compiler_params=_params(1),
        name="peer_act",
    )(h, gate, rep, after)


def _sc_peer_down(expert, w_lanes, x2d, table, *, first, n_tokens):
    t, d = x2d.shape
    per_worker = n_tokens // SC_WORKERS
    words = d // 2
    mesh = plsc.VectorSubcoreMesh(core_axis_name="c", subcore_axis_name="s")

    @functools.partial(
        pl.kernel, mesh=mesh,
        out_type=jax.ShapeDtypeStruct((t, d), F32),
        scratch_types=[
            pltpu.VMEM((PEER_SLOTS,), jnp.int32),
            pltpu.VMEM((PEER_SLOTS * SC_LANES,), F32),
            pltpu.VMEM((d,), F32),
            pltpu.VMEM((2, SC_HALF, words), jnp.int32),
            pltpu.SemaphoreType.DMA,
            pltpu.SemaphoreType.DMA,
        ],
        compiler_params=pltpu.CompilerParams(needs_layout_passes=False),
        name="sc_peer_down",
    )
    def run(expert_hbm, w_hbm, x_hbm, table_hbm, out_hbm, idx_v, w_v, out_v, rows_v, sem0, sem1):
        worker = lax.axis_index("s") * SC_CORES + lax.axis_index("c")
        local = worker * per_worker
        base = first + local

        def accumulate(h):
            @pl.loop(0, SC_HALF, step=SC_ROWS_PER_STEP)
            def _(j0):
                weights = [w_v[pl.ds((h * SC_HALF + j0 + r) * SC_LANES, SC_LANES)]
                           for r in range(SC_ROWS_PER_STEP)]
                for g in range(words // SC_LANES // SC_CHUNKS):
                    loaded = [[rows_v[h, j0 + r, pl.ds((g * SC_CHUNKS + q) * SC_LANES, SC_LANES)]
                               for q in range(SC_CHUNKS)] for r in range(SC_ROWS_PER_STEP)]
                    for q in range(SC_CHUNKS):
                        hi_sum = lo_sum = None
                        for r in range(SC_ROWS_PER_STEP):
                            word = loaded[r][q]
                            hi = weights[r] * lax.bitcast_convert_type(word & jnp.int32(-65536), F32)
                            lo = weights[r] * lax.bitcast_convert_type(word << 16, F32)
                            hi_sum = hi if hi_sum is None else hi_sum + hi
                            lo_sum = lo if lo_sum is None else lo_sum + lo
                        col = (g * SC_CHUNKS + q) * 2 * SC_LANES
                        plsc.addupdate(out_v.at[pl.ds(col, SC_LANES)], hi_sum)
                        plsc.addupdate(out_v.at[pl.ds(col + SC_LANES, SC_LANES)], lo_sum)

        @pl.loop(0, per_worker)
        def _(i):
            tok = base + i
            pltpu.sync_copy(expert_hbm.at[tok], idx_v)
            first_half = pltpu.async_copy(table_hbm.at[idx_v.at[pl.ds(0, SC_HALF)]],
                                          rows_v.at[0], sem0)
            second_half = pltpu.async_copy(table_hbm.at[idx_v.at[pl.ds(SC_HALF, SC_HALF)]],
                                           rows_v.at[1], sem1)
            pltpu.sync_copy(w_hbm.at[local + i], w_v)
            pltpu.sync_copy(x_hbm.at[tok], out_v)
            first_half.wait()
            accumulate(0)
            second_half.wait()
            accumulate(1)
            pltpu.sync_copy(out_v, out_hbm.at[tok])

    return run(expert, w_lanes, x2d, table)


def _peer_constants():
    col = jnp.arange(DOWN_K)
    src = ((col // SUBLANES) % 2) * PEER_SLOTS + col // PAIR_ROWS
    spread = (jnp.arange(2 * PEER_SLOTS)[:, None] == src[None, :]).astype(BF16)
    diag = (col[None, :] % SUBLANES == jnp.arange(SUBLANES)[:, None]).astype(F32)
    lane_repeat = jnp.repeat(jnp.eye(PEER_SLOTS, dtype=BF16), SC_LANES, axis=1)
    return spread, diag, lane_repeat


def _pair_slabs(v):
    e, d = v.shape
    return v.astype(BF16).reshape(e // 2, 2 * d // LANES, LANES)


def _peer(x2d, g, wq_bf16, keys_bf16, u_packed, u_words, v_slabs, v_words, consts, *, tm_route,
          tm_expert, n_sparsecore):
    t, d = x2d.shape
    spread, diag, lane_repeat = consts
    n_tc = t - n_sparsecore
    assert n_sparsecore > 0 and n_sparsecore % SC_WORKERS == 0 and n_tc % tm_expert == 0
    xn, pair, parity, gate = _peer_route(x2d, g, wq_bf16, keys_bf16, tm=tm_route)
    pair, parity, gate = pair.T, parity.T, gate.T
    xn3 = xn.reshape(t, d // LANES, LANES)
    x3 = x2d.reshape(t, d // LANES, LANES)
    expert = pair * 2 + parity
    h_sc = _sc_peer_up(expert, xn, u_words, first=n_tc, n_tokens=n_sparsecore)
    wexp = _peer_up(pair, parity, xn3, u_packed, gate, spread, tm=tm_expert, n_tokens=n_tc)
    w_lanes = _peer_act(h_sc, gate, lane_repeat, wexp, first=n_tc, tm=tm_expert)
    out = _sc_peer_down(expert, w_lanes, x2d, v_words, first=n_tc, n_tokens=n_sparsecore)
    out3 = _peer_down(pair, wexp, diag, x3, v_slabs, tm=tm_expert, n_tokens=n_tc)
    return lax.dynamic_update_slice(out, out3.reshape(n_tc, d), (0, 0))


def _final_norm_kernel(x_ref, g_ref, o_ref):
    o_ref[...] = _rms_scale(x_ref[...], g_ref[...])


def _final_norm(x2d, g, *, tm):
    t, d = x2d.shape
    return pl.pallas_call(
        _final_norm_kernel,
        grid=(t // tm,),
        in_specs=[pl.BlockSpec((tm, d), lambda i: (i, 0)), _resident((1, d))],
        out_specs=pl.BlockSpec((tm, d), lambda i: (i, 0)),
        out_shape=jax.ShapeDtypeStruct((t, d), F32),
        compiler_params=_params(1),
        name="final_norm",
    )(x2d, g.reshape(1, d))


def _tiles(seq):
    return dict(tm_proj=min(256, seq), tm_out=min(512, seq), tq=min(512, seq), tk=min(1024, seq),
                tm_route=LANES, tm_expert=LANES, tm_norm=min(512, seq))


SC_SHARE = 0.359375


def _sparsecore_tokens(t, tm):
    unit = tm * SC_WORKERS // math.gcd(tm, SC_WORKERS)
    return int(t * SC_SHARE) // unit * unit


def _to_residue_major(a2d, batch, seq, dilation, c0, c1):
    cols = a2d[:, c0:c1].reshape(batch, seq // dilation, dilation, c1 - c0)
    return cols.transpose(0, 2, 1, 3)


def _from_residue_major(a4):
    b, d, length, c = a4.shape
    return a4.transpose(0, 2, 1, 3).reshape(b * d * length, c)


def kernel(x, mem, norm_mix, a_w_in, a_lambda, a_subln, b_w_in, norm_mem, w_mem_kv, w_out,
           norm_ffn, peer_wq, peer_keys, peer_u, peer_v, shared_norm, shared_w_kv, final_norm):
    batch, seq, d = x.shape
    mem_tokens = mem.shape[1]
    depth = norm_mix.shape[0]
    n_a = a_w_in.shape[0]
    t = batch * seq
    tiles = _tiles(seq)
    rope = _rope_tables(seq)
    scale = HEAD_DIM ** -0.5

    diff_qk = DIFF_HEADS * 2 * HEAD_DIM
    dil_qk = len(DIL_GROUPS) * DIL_GROUP_WIDTH
    peer_consts = _peer_constants()
    expand = jnp.repeat(jnp.eye(LANES, DIL_HEADS, dtype=BF16), HEAD_DIM, axis=1)

    x2d = x.reshape(t, d)
    mem2d = mem.reshape(batch * mem_tokens, d)
    shared = None

    for l in range(depth):
        memkv = _proj(mem2d, norm_mem[l], w_mem_kv[l].astype(BF16), tm=mem_tokens)
        w_o = w_out[l].astype(BF16)
        if l < n_a:
            col_scale = jnp.concatenate([jnp.full((diff_qk,), scale, F32),
                                         jnp.ones((a_w_in.shape[2] - diff_qk - MEM_Q_WIDTH,), F32),
                                         jnp.full((MEM_Q_WIDTH,), scale, F32)])
            w_in = (a_w_in[l] * col_scale).astype(BF16)
            proj = _proj(x2d, norm_mix[l], w_in, tm=tiles["tm_proj"], n_rope=2 * diff_qk,
                         rope=rope, seq=seq)
            lambda_init = 0.8 - 0.6 * math.exp(-0.3 * l)
            mix = _diff_attention(proj, a_lambda[l], a_subln[l], batch=batch, seq=seq,
                                  lambda_init=lambda_init, tq=tiles["tq"], tk=tiles["tk"])
            x2d = _outproj(x2d, proj, memkv, w_o, seq=seq, mem_tokens=mem_tokens,
                           tm=tiles["tm_out"], mix=mix)
        else:
            w_in = (b_w_in[l - n_a] * scale).astype(BF16)
            proj = _proj(x2d, norm_mix[l], w_in, tm=tiles["tm_proj"], n_rope=dil_qk,
                         rope=rope, seq=seq)
            outs, stats = [], []
            for gi, (window, dilation) in enumerate(DIL_GROUPS):
                c0 = gi * DIL_GROUP_WIDTH
                blk0 = c0 // LANES
                if dilation == 1:
                    q4, q_off = proj.reshape(batch, 1, seq, proj.shape[1]), blk0
                else:
                    q4, q_off = _to_residue_major(proj, batch, seq, dilation,
                                                  c0, c0 + DIL_GROUP_WIDTH), 0
                k4, k_off, v4, v_off = shared[gi]
                o4, st4 = _dilated_group(q4, k4, v4, q_off=q_off, k_off=k_off, v_off=v_off,
                                         n_steps=window // dilation)
                outs.append(_from_residue_major(o4))
                stats.append(_from_residue_major(st4))
            x2d = _outproj(x2d, proj, memkv, w_o, seq=seq, mem_tokens=mem_tokens,
                           tm=tiles["tm_out"], group_out=outs, group_stats=stats, expand=expand)

        x2d = _peer(x2d, norm_ffn[l], peer_wq[l].astype(BF16),
                    peer_keys[l].reshape(2 * PEER_HEADS, PEER_N_KEYS, PEER_HALF).astype(BF16),
                    _pack_expert_table(peer_u[l]), _sc_word_table(peer_u[l]),
                    _pair_slabs(peer_v[l]), _sc_word_table(peer_v[l]), peer_consts,
                    tm_route=tiles["tm_route"], tm_expert=tiles["tm_expert"],
                    n_sparsecore=_sparsecore_tokens(t, tiles["tm_expert"]))

        if l == n_a - 1:
            kv = _proj(x2d, shared_norm, shared_w_kv.astype(BF16), tm=tiles["tm_proj"],
                       n_rope=dil_qk, rope=rope, seq=seq)
            shared = []
            for gi, (window, dilation) in enumerate(DIL_GROUPS):
                c0 = gi * DIL_GROUP_WIDTH
                if dilation == 1:
                    kv4 = kv.reshape(batch, 1, seq, kv.shape[1])
                    shared.append((kv4, c0 // LANES, kv4, (dil_qk + c0) // LANES))
                else:
                    k4 = _to_residue_major(kv, batch, seq, dilation, c0, c0 + DIL_GROUP_WIDTH)
                    v4 = _to_residue_major(kv, batch, seq, dilation, dil_qk + c0,
                                           dil_qk + c0 + DIL_GROUP_WIDTH)
                    shared.append((k4, 0, v4, 0))

    return _final_norm(x2d, final_norm, tm=tiles["tm_norm"]).reshape(batch, seq, d)
```
